```python
import jax, jax.numpy as jnp
from jax import lax
import numpy as np

D_MODEL = 1024
BATCH = 16
SEQ = 4096
DEPTH = 1
DEC_BATCH = 2
DEC_SEQ = 16384
PAST_LEN = 128

HEAD_DIM = 64
N_HEADS = D_MODEL // HEAD_DIM
SWA_HEADS = N_HEADS // 2
GDN_HEADS = N_HEADS - SWA_HEADS
SWA_WIDTH = SWA_HEADS * HEAD_DIM
GDN_WIDTH = GDN_HEADS * HEAD_DIM
DILATED_PATTERNS = ((128, 1), (512, 4), (2048, 16))
CONV_K = 5
CHUNK = 64
MEM_TOKENS = 256
MEM_HEADS = 4
MEM_HEAD_DIM = D_MODEL // MEM_HEADS
N_GROUPS = 4
EXPERTS_PER_GROUP = 8
N_EXPERTS = N_GROUPS * EXPERTS_PER_GROUP
TOP_K = 2
D_EXPERT = D_MODEL // 2
ROUTE_BLOCK = 256
EPS = 1e-6
IN_COLS = 3 * SWA_WIDTH + 4 * GDN_WIDTH + 4 * GDN_HEADS

kernel_name = "hymba_dilated_gdn_hiermoe_encoder"


def rmsnorm(x, g):
    xf = x.astype(jnp.float32)
    y = xf * lax.rsqrt(jnp.mean(xf * xf, axis=-1, keepdims=True) + EPS)
    return (y * g.astype(jnp.float32)).astype(x.dtype)


def l2norm(t):
    return t * lax.rsqrt(jnp.sum(t * t, axis=-1, keepdims=True) + EPS)


def alibi_slopes(n):
    return jnp.exp2(-8.0 * jnp.arange(1, n + 1, dtype=jnp.float32) / n)


def dilated_branch(q, k, v, window, dilation, slopes):
    B, S, H, Dh = q.shape
    d = dilation
    W = window // (2 * d)
    QB = W
    L = S // d
    nb = -(-L // QB)
    Lp = nb * QB

    def sub(t):
        return t.reshape(B, L, d, H, Dh).transpose(0, 2, 1, 3, 4)

    qs = jnp.pad(sub(q), ((0, 0), (0, 0), (0, Lp - L), (0, 0), (0, 0))).reshape(B, d, nb, QB, H, Dh)

    def band(t):
        tp = jnp.pad(sub(t), ((0, 0), (0, 0), (QB, Lp - L + QB), (0, 0), (0, 0))).reshape(B, d, nb + 2, QB, H, Dh)
        return jnp.concatenate([tp[:, :, :-2], tp[:, :, 1:-1], tp[:, :, 2:]], axis=3)

    kb, vb = band(k), band(v)
    rel = jnp.arange(3 * QB)[None, :] - QB - jnp.arange(QB)[:, None]
    key_idx = (jnp.arange(nb)[:, None, None] - 1) * QB + jnp.arange(3 * QB)[None, None, :]
    valid = (jnp.abs(rel) <= W)[None] & (key_idx >= 0) & (key_idx < L)
    bias = -slopes[:, None, None] * (jnp.abs(rel) * d).astype(jnp.float32)[None]
    s = jnp.einsum('brnqhd,brnkhd->brnhqk', qs, kb) * (Dh ** -0.5) + bias
    s = jnp.where(valid[:, None], s, -jnp.inf)
    m = jnp.max(s, axis=-1, keepdims=True)
    e = jnp.exp(s - m)
    l = jnp.sum(e, axis=-1)
    o = jnp.einsum('brnhqk,brnkhd->brnqhd', e, vb) / jnp.swapaxes(l, -1, -2)[..., None]
    lse = jnp.swapaxes(m[..., 0] + jnp.log(l), -1, -2)
    o = o.reshape(B, d, Lp, H, Dh)[:, :, :L].transpose(0, 2, 1, 3, 4).reshape(B, S, H, Dh)
    lse = lse.reshape(B, d, Lp, H)[:, :, :L].transpose(0, 2, 1, 3).reshape(B, S, H)
    return o, lse


def dilated_attention(q, k, v):
    slopes = alibi_slopes(q.shape[2])
    outs, lses = [], []
    for window, dilation in DILATED_PATTERNS:
        o, lse = dilated_branch(q, k, v, window, dilation, slopes)
        outs.append(o)
        lses.append(lse)
    wts = jax.nn.softmax(jnp.stack(lses), axis=0)
    return jnp.einsum('ibsh,ibshd->bshd', wts, jnp.stack(outs))


def gated_delta_scan(q, k, v, g, beta):
    B, T, H, dk = q.shape
    dv = v.shape[-1]
    C = CHUNK
    N = T // C

    def chunks(t):
        return t.reshape(B, N, C, H, -1).transpose(1, 0, 3, 2, 4)

    qc, kc, vc = chunks(q), chunks(k), chunks(v)
    gc = jnp.cumsum(g.reshape(B, N, C, H).transpose(1, 0, 3, 2), axis=-1)
    bc = beta.reshape(B, N, C, H).transpose(1, 0, 3, 2)[..., None]
    lower = jnp.tril(jnp.ones((C, C), bool))
    strict = jnp.tril(jnp.ones((C, C), bool), -1)
    decay = jnp.exp(jnp.where(lower, gc[..., :, None] - gc[..., None, :], -jnp.inf))
    kbeta = kc * bc
    lmat = jnp.where(strict, jnp.einsum('nbhid,nbhjd->nbhij', kbeta, kc) * decay, 0.0)

    def solve(rhs):
        return lax.linalg.triangular_solve(lmat, rhs, left_side=True, lower=True, unit_diagonal=True)

    u = solve(vc * bc)
    w = solve(kbeta * jnp.exp(gc)[..., None])
    a_qk = jnp.einsum('nbhid,nbhjd->nbhij', qc, kc) * decay

    def step(state, inp):
        q_i, k_i, u_i, w_i, g_i, a_i = inp
        v_new = u_i - jnp.einsum('bhck,bhkv->bhcv', w_i, state)
        o_i = (jnp.einsum('bhck,bhkv->bhcv', q_i * jnp.exp(g_i)[..., None], state)
               + jnp.einsum('bhij,bhjv->bhiv', a_i, v_new))
        g_last = g_i[..., -1:]
        state = (state * jnp.exp(g_last)[..., None]
                 + jnp.einsum('bhck,bhcv->bhkv', k_i * jnp.exp(g_last - g_i)[..., None], v_new))
        return state, o_i

    s0 = jnp.zeros((B, H, dk, dv), jnp.float32)
    _, o = lax.scan(step, s0, (qc, kc, u, w, gc, a_qk))
    return o.transpose(1, 0, 3, 2, 4).reshape(B, T, H, dv)


def token_mixer(h, w_in, conv_w, a_log, dt_bias, gdn_norm, w_out):
    B, S, _ = h.shape
    proj = h @ w_in
    sa, gd_qkv, z, ab = jnp.split(
        proj, [3 * SWA_WIDTH, 3 * SWA_WIDTH + 3 * GDN_WIDTH, 3 * SWA_WIDTH + 4 * GDN_WIDTH], axis=-1)
    qa, ka, va = [t.astype(jnp.float32).reshape(B, S, SWA_HEADS, HEAD_DIM) for t in jnp.split(sa, 3, axis=-1)]
    ya = dilated_attention(qa, ka, va).reshape(B, S, SWA_WIDTH)
    c = lax.conv_general_dilated(
        gd_qkv.astype(jnp.float32), conv_w.astype(jnp.float32)[:, None, :], window_strides=(1,),
        padding=[(CONV_K // 2, CONV_K // 2)], dimension_numbers=('NWC', 'WIO', 'NWC'),
        feature_group_count=3 * GDN_WIDTH)
    c = jax.nn.silu(c)
    qb, kb, vb = [t.reshape(B, S, GDN_HEADS, HEAD_DIM) for t in jnp.split(c, 3, axis=-1)]
    qb = l2norm(qb) * (HEAD_DIM ** -0.5)
    kb = l2norm(kb)
    ab = ab.astype(jnp.float32).reshape(B, S, 4, GDN_HEADS)
    beta = jax.nn.sigmoid(ab[:, :, 0:2])
    g = -jnp.exp(a_log.astype(jnp.float32)) * jax.nn.softplus(ab[:, :, 2:4] + dt_bias.astype(jnp.float32))
    fwd = gated_delta_scan(qb, kb, vb, g[:, :, 0], beta[:, :, 0])
    flip = lambda t: t[:, ::-1]
    bwd = flip(gated_delta_scan(flip(qb), flip(kb), flip(vb), flip(g[:, :, 1]), flip(beta[:, :, 1])))
    ob = fwd + bwd
    ob = (ob * lax.rsqrt(jnp.mean(ob * ob, axis=-1, keepdims=True) + EPS) * gdn_norm.astype(jnp.float32)
          * jax.nn.silu(z.astype(jnp.float32).reshape(B, S, GDN_HEADS, HEAD_DIM)))
    y = jnp.concatenate([ya, ob.reshape(B, S, GDN_WIDTH)], axis=-1).astype(h.dtype)
    return y @ w_out


def memory_attention(h, mem_n, w_mq, w_mkv, w_mo):
    B, S, _ = h.shape
    M = mem_n.shape[1]
    q = (h @ w_mq).astype(jnp.float32).reshape(B, S, MEM_HEADS, MEM_HEAD_DIM)
    kv = (mem_n @ w_mkv).astype(jnp.float32).reshape(B, M, 2, MEM_HEADS, MEM_HEAD_DIM)
    s = jnp.einsum('bshd,bmhd->bhsm', q, kv[:, :, 0]) * (MEM_HEAD_DIM ** -0.5)
    p = jax.nn.softmax(s, axis=-1)
    o = jnp.einsum('bhsm,bmhd->bshd', p, kv[:, :, 1]).reshape(B, S, MEM_HEADS * MEM_HEAD_DIM)
    return o.astype(h.dtype) @ w_mo


def hierarchical_moe(h, w_router_g, w_router_e, w_gate, w_up, w_down):
    N, D = h.shape
    hf = h.astype(jnp.float32)
    p_group = jax.nn.softmax(hf @ w_router_g.astype(jnp.float32), axis=-1)
    g_idx = jnp.argmax(p_group, axis=-1)
    g_w = jnp.max(p_group, axis=-1)
    logits_e = jnp.einsum('nd,gde->nge', hf, w_router_e.astype(jnp.float32))
    logits_e = jnp.take_along_axis(logits_e, g_idx[:, None, None], axis=1)[:, 0]
    top_p, top_i = lax.top_k(jax.nn.softmax(logits_e, axis=-1), TOP_K)
    gates = g_w[:, None] * top_p / jnp.sum(top_p, axis=-1, keepdims=True)
    expert = (g_idx[:, None] * EXPERTS_PER_GROUP + top_i).reshape(-1)
    tok = jnp.repeat(jnp.arange(N, dtype=jnp.int32), TOP_K)
    gw = gates.reshape(-1)
    M = N * TOP_K
    order = jnp.argsort(expert)
    e_sorted = expert[order]
    counts = jnp.bincount(expert, length=N_EXPERTS)
    padded = (counts + ROUTE_BLOCK - 1) // ROUTE_BLOCK * ROUTE_BLOCK
    pad_end = jnp.cumsum(padded)
    rank = jnp.arange(M) - (jnp.cumsum(counts) - counts)[e_sorted]
    dest = (pad_end - padded)[e_sorted] + rank
    n_blocks = -(-(M + N_EXPERTS * (ROUTE_BLOCK - 1)) // ROUTE_BLOCK)
    P = n_blocks * ROUTE_BLOCK
    slot_tok = jnp.zeros((P,), jnp.int32).at[dest].set(tok[order])
    slot_w = jnp.zeros((P,), jnp.float32).at[dest].set(gw[order])
    block_expert = jnp.minimum(
        jnp.searchsorted(pad_end, jnp.arange(n_blocks) * ROUTE_BLOCK, side='right'), N_EXPERTS - 1)
    xb = h[slot_tok].reshape(n_blocks, ROUTE_BLOCK, D)

    def expert_block(args):
        xi, e = args
        return (jax.nn.silu(xi @ w_gate[e]) * (xi @ w_up[e])) @ w_down[e]

    yb = lax.map(expert_block, (xb, block_expert)).reshape(P, D)
    y = jnp.zeros((N, D), jnp.float32).at[slot_tok].add(yb.astype(jnp.float32) * slot_w[:, None])
    return y.astype(h.dtype)


def encoder_trunk(x, mem, norm_mix, w_in, conv_w, a_log, dt_bias, gdn_norm, w_out,
                  norm_mem_q, norm_mem_kv, w_mq, w_mkv, w_mo,
                  norm_ffn, w_router_g, w_router_e, w_gate, w_up, w_down, norm_final):
    B, S, D = x.shape
    for l in range(DEPTH):
        x = x + token_mixer(rmsnorm(x, norm_mix[l]), w_in[l], conv_w[l], a_log[l], dt_bias[l],
                            gdn_norm[l], w_out[l])
        x = x + memory_attention(rmsnorm(x, norm_mem_q[l]), rmsnorm(mem, norm_mem_kv[l]),
                                 w_mq[l], w_mkv[l], w_mo[l])
        x = x + hierarchical_moe(rmsnorm(x, norm_ffn[l]).reshape(B * S, D), w_router_g[l], w_router_e[l],
                                 w_gate[l], w_up[l], w_down[l]).reshape(B, S, D)
    return rmsnorm(x, norm_final)


def setup_inputs(seed: int = 0) -> dict:
    key = jax.random.key(seed)
    ks = jax.random.split(key, 24)
    f32 = jnp.float32

    def nrm(k, shape, fan_in):
        return jax.random.normal(k, shape, f32) * (fan_in ** -0.5)

    def gain(k, shape):
        return 1.0 + 0.02 * jax.random.normal(k, shape, f32)

    dt = jnp.exp(jax.random.uniform(ks[8], (DEPTH, 2, GDN_HEADS), f32, np.log(1e-3), np.log(1e-1)))
    return {
        "x_prompt": jax.random.normal(ks[0], (BATCH, SEQ, D_MODEL), f32),
        "x_sample": jax.random.normal(ks[1], (DEC_BATCH, DEC_SEQ, D_MODEL), f32),
        "mem_prompt": jax.random.normal(ks[2], (BATCH, MEM_TOKENS, D_MODEL), f32),
        "mem_sample": jax.random.normal(ks[3], (DEC_BATCH, MEM_TOKENS, D_MODEL), f32),
        "norm_mix": gain(ks[4], (DEPTH, D_MODEL)),
        "w_in": nrm(ks[5], (DEPTH, D_MODEL, IN_COLS), D_MODEL),
        "conv_w": nrm(ks[6], (DEPTH, CONV_K, 3 * GDN_WIDTH), CONV_K),
        "a_log": jnp.log(jax.random.uniform(ks[7], (DEPTH, 2, GDN_HEADS), f32, 1.0, 16.0)),
        "dt_bias": dt + jnp.log(-jnp.expm1(-dt)),
        "gdn_norm": gain(ks[9], (DEPTH, HEAD_DIM)),
        "w_out": nrm(ks[10], (DEPTH, D_MODEL, D_MODEL), D_MODEL),
        "norm_mem_q": gain(ks[11], (DEPTH, D_MODEL)),
        "norm_mem_kv": gain(ks[12], (DEPTH, D_MODEL)),
        "w_mq": nrm(ks[13], (DEPTH, D_MODEL, MEM_HEADS * MEM_HEAD_DIM), D_MODEL),
        "w_mkv": nrm(ks[14], (DEPTH, D_MODEL, 2 * MEM_HEADS * MEM_HEAD_DIM), D_MODEL),
        "w_mo": nrm(ks[15], (DEPTH, MEM_HEADS * MEM_HEAD_DIM, D_MODEL), MEM_HEADS * MEM_HEAD_DIM),
        "norm_ffn": gain(ks[16], (DEPTH, D_MODEL)),
        "w_router_g": nrm(ks[17], (DEPTH, D_MODEL, N_GROUPS), D_MODEL),
        "w_router_e": nrm(ks[18], (DEPTH, N_GROUPS, D_MODEL, EXPERTS_PER_GROUP), D_MODEL),
        "w_gate": nrm(ks[19], (DEPTH, N_EXPERTS, D_MODEL, D_EXPERT), D_MODEL),
        "w_up": nrm(ks[20], (DEPTH, N_EXPERTS, D_MODEL, D_EXPERT), D_MODEL),
        "w_down": nrm(ks[21], (DEPTH, N_EXPERTS, D_EXPERT, D_MODEL), D_EXPERT),
        "norm_final": gain(ks[22], (D_MODEL,)),
    }


def reference(x_prompt, x_sample, mem_prompt, mem_sample, norm_mix, w_in, conv_w, a_log, dt_bias,
              gdn_norm, w_out, norm_mem_q, norm_mem_kv, w_mq, w_mkv, w_mo, norm_ffn,
              w_router_g, w_router_e, w_gate, w_up, w_down, norm_final):
    params = (norm_mix, w_in, conv_w, a_log, dt_bias, gdn_norm, w_out, norm_mem_q, norm_mem_kv,
              w_mq, w_mkv, w_mo, norm_ffn, w_router_g, w_router_e, w_gate, w_up, w_down, norm_final)
    y_prompt = encoder_trunk(x_prompt, mem_prompt, *params)
    y_sample = encoder_trunk(x_sample, mem_sample, *params)
    return (y_prompt, y_sample)
```

```python
import functools

import jax
import jax.numpy as jnp
import numpy as np
from jax import lax
from jax.experimental import pallas as pl
from jax.experimental.pallas import tpu as pltpu

F32 = jnp.float32
BF16 = jnp.bfloat16

D_MODEL = 1024
HEAD_DIM = 64
SWA_HEADS = 8
GDN_HEADS = 8
SWA_WIDTH = SWA_HEADS * HEAD_DIM
GDN_WIDTH = GDN_HEADS * HEAD_DIM
DILATIONS = (1, 4, 16)
ATT_W = 64
CONV_K = 5
CHUNK = 64
MEM_HEADS = 4
MEM_HEAD_DIM = 256
N_GROUPS = 4
EXPERTS_PER_GROUP = 8
N_EXPERTS = 32
TOP_K = 2
D_EXPERT = 512
ROUTE_BLOCK = 256
EPS = 1e-6

LANES = 128
VMEM_LIMIT = 56 * 1024 * 1024
NEG_BIG = -1e30


def _dot(a, b):
    return jnp.dot(a, b, preferred_element_type=F32)


def _dot_nt(a, b):
    return lax.dot_general(a, b, (((1,), (1,)), ((), ())), preferred_element_type=F32)


def _dot_tn(a, b):
    return lax.dot_general(a, b, (((0,), (0,)), ((), ())), preferred_element_type=F32)


def _split2(x):
    hi = x.astype(BF16)
    lo = (x - hi.astype(F32)).astype(BF16)
    return hi, lo


def _split3(x):
    hi = x.astype(BF16)
    r = x - hi.astype(F32)
    mid = r.astype(BF16)
    lo = (r - mid.astype(F32)).astype(BF16)
    return hi, mid, lo


def _rms(x, g):
    return x * lax.rsqrt(jnp.mean(x * x, axis=-1, keepdims=True) + EPS) * g


def _params(*sem):
    return pltpu.CompilerParams(dimension_semantics=sem, vmem_limit_bytes=VMEM_LIMIT)


def _inproj_body(x_ref, g_ref, wa_ref, wgd_ref, wz_ref, wab_hi_ref, wab_lo_ref,
                 qkv_ref, gd_ref, z_ref, ab_ref):
    h = _rms(x_ref[...], g_ref[...])
    h_hi, h_lo = _split2(h)
    qkv_ref[...] = _dot(h_hi, wa_ref[...]).astype(BF16)
    gd_ref[...] = _dot(h_hi, wgd_ref[...]).astype(BF16)
    z_ref[...] = _dot(h_hi, wz_ref[...]).astype(BF16)
    wab_hi = wab_hi_ref[...]
    ab = _dot(h_hi, wab_hi) + _dot(h_lo, wab_hi) + _dot(h_hi, wab_lo_ref[...])
    ab_ref[...] = ab


def _inproj(x2, norm_mix, w_in, tm=512):
    n = x2.shape[0]
    wa = w_in[:, :3 * SWA_WIDTH].astype(BF16)
    wgd = w_in[:, 3 * SWA_WIDTH:3 * SWA_WIDTH + 3 * GDN_WIDTH].astype(BF16)
    wz = w_in[:, 3 * SWA_WIDTH + 3 * GDN_WIDTH:3 * SWA_WIDTH + 4 * GDN_WIDTH].astype(BF16)
    wab = jnp.pad(w_in[:, 3 * SWA_WIDTH + 4 * GDN_WIDTH:], ((0, 0), (0, LANES - 4 * GDN_HEADS)))
    wab_hi, wab_lo = _split2(wab)
    full = lambda shape: pl.BlockSpec(shape, lambda i: (0, 0))
    rows = lambda w: pl.BlockSpec((tm, w), lambda i: (i, 0))
    return pl.pallas_call(
        _inproj_body,
        grid=(n // tm,),
        in_specs=[rows(D_MODEL), full((1, D_MODEL)), full(wa.shape), full(wgd.shape), full(wz.shape),
                  full(wab_hi.shape), full(wab_lo.shape)],
        out_specs=[rows(3 * SWA_WIDTH), rows(3 * GDN_WIDTH), rows(GDN_WIDTH), rows(LANES)],
        out_shape=[jax.ShapeDtypeStruct((n, 3 * SWA_WIDTH), BF16),
                   jax.ShapeDtypeStruct((n, 3 * GDN_WIDTH), BF16),
                   jax.ShapeDtypeStruct((n, GDN_WIDTH), BF16),
                   jax.ShapeDtypeStruct((n, LANES), F32)],
        compiler_params=_params("parallel"),
        name="inproj",
    )(x2, norm_mix.reshape(1, D_MODEL), wa, wgd, wz, wab_hi, wab_lo)


ATT_QB = 128
ATT_KB = ATT_QB + 2 * ATT_W


def _attn_body(slope_ref, q_ref, k_ref, v_ref, o_ref, lse_ref, bias_ref, *, dil, n_qb, seq_l):
    pair = pl.program_id(2)
    lb = pl.program_id(3)
    lane = lax.broadcasted_iota(jnp.int32, (1, LANES), 1)
    left = lane < HEAD_DIM

    row = lax.broadcasted_iota(jnp.int32, (ATT_QB, ATT_KB), 0)
    col = lax.broadcasted_iota(jnp.int32, (ATT_QB, ATT_KB), 1)
    for var in range(3):
        rel = jnp.abs(col - row - var * ATT_W)
        dist = (rel * dil).astype(F32)
        for h in range(2):
            slope = slope_ref[pair * 2 + h]
            bias_ref[var * 2 + h] = jnp.where(rel <= ATT_W, -slope * dist, NEG_BIG)

    def qblock(qi, carry):
        n0 = (lb * n_qb + qi) * ATT_QB
        kstart = jnp.clip(n0 - ATT_W, 0, seq_l - ATT_KB)
        kstart = pl.multiple_of(kstart, ATT_W)
        var = (n0 - kstart) // ATT_W
        q = q_ref[pl.ds(pl.multiple_of(qi * ATT_QB, ATT_QB), ATT_QB), :]
        q = q * jnp.asarray(HEAD_DIM ** -0.5, BF16)
        kb = k_ref[pl.ds(kstart, ATT_KB), :]
        vb = v_ref[pl.ds(kstart, ATT_KB), :]
        accs, maxes = [], []
        for h in range(2):
            mine = left if h == 0 else jnp.logical_not(left)
            qh = jnp.where(mine, q, jnp.zeros_like(q))
            s = _dot_nt(qh, kb) + bias_ref[var * 2 + h]
            m = jnp.max(s, axis=-1, keepdims=True)
            p = jnp.exp(s - m).astype(BF16)
            vh = jnp.where(mine, vb, jnp.ones_like(vb))
            accs.append(_dot(p, vh))
            maxes.append(m)
        num = jnp.where(left, accs[0], accs[1])
        den = pltpu.roll(jnp.where(left, accs[1], accs[0]), HEAD_DIM, 1)
        mx = jnp.where(left, maxes[0], maxes[1])
        rows = pl.ds(pl.multiple_of(qi * ATT_QB, ATT_QB), ATT_QB)
        o_ref[rows, :] = (num / den).astype(BF16)
        lse_ref[rows, :] = mx + jnp.log(den)
        return carry

    lax.fori_loop(0, n_qb, qblock, 0)


def _attn_branch(qkv, slopes, dil, bsz, seq):
    seq_l = seq // dil
    lblk = min(2048, seq_l)
    n_qb = lblk // ATT_QB
    n_pairs = SWA_HEADS // 2
    cols = 3 * SWA_WIDTH // LANES
    view = qkv.reshape(bsz, seq_l, dil * 3 * SWA_WIDTH)
    qspec = pl.BlockSpec((None, lblk, LANES), lambda b, r, p, l: (b, l, r * cols + p))
    kspec = pl.BlockSpec((None, seq_l, LANES), lambda b, r, p, l: (b, 0, r * cols + n_pairs + p))
    vspec = pl.BlockSpec((None, seq_l, LANES), lambda b, r, p, l: (b, 0, r * cols + 2 * n_pairs + p))
    ospec = pl.BlockSpec((None, lblk, LANES), lambda b, r, p, l: (b, l, r * n_pairs + p))
    o, lse = pl.pallas_call(
        functools.partial(_attn_body, dil=dil, n_qb=n_qb, seq_l=seq_l),
        grid=(bsz, dil, n_pairs, seq_l // lblk),
        in_specs=[pl.BlockSpec(memory_space=pltpu.SMEM), qspec, kspec, vspec],
        out_specs=[ospec, ospec],
        out_shape=[jax.ShapeDtypeStruct((bsz, seq_l, dil * SWA_WIDTH), BF16),
                   jax.ShapeDtypeStruct((bsz, seq_l, dil * SWA_WIDTH), F32)],
        scratch_shapes=[pltpu.VMEM((6, ATT_QB, ATT_KB), F32)],
        compiler_params=_params("parallel", "parallel", "parallel", "arbitrary"),
        name=f"dilated_attn_d{dil}",
    )(slopes, view, view, view)
    return o.reshape(bsz * seq, SWA_WIDTH), lse.reshape(bsz * seq, SWA_WIDTH)


def _memkv_body(m_ref, g_ref, w_ref, k_ref, v_ref):
    h = _rms(m_ref[...], g_ref[...]).astype(BF16)
    kv = _dot(h, w_ref[...])
    width = MEM_HEADS * MEM_HEAD_DIM
    k_ref[...] = (kv[:, :width] * (MEM_HEAD_DIM ** -0.5)).astype(BF16)
    v_ref[...] = kv[:, width:].astype(BF16)


def _memkv(mem2, norm_kv, w_mkv, tm=256):
    n = mem2.shape[0]
    width = MEM_HEADS * MEM_HEAD_DIM
    return pl.pallas_call(
        _memkv_body,
        grid=(n // tm,),
        in_specs=[pl.BlockSpec((tm, D_MODEL), lambda i: (i, 0)),
                  pl.BlockSpec((1, D_MODEL), lambda i: (0, 0)),
                  pl.BlockSpec((D_MODEL, 2 * width), lambda i: (0, 0))],
        out_specs=[pl.BlockSpec((tm, width), lambda i: (i, 0))] * 2,
        out_shape=[jax.ShapeDtypeStruct((n, width), BF16)] * 2,
        compiler_params=_params("parallel"),
        name="mem_kv",
    )(mem2, norm_kv.reshape(1, D_MODEL), w_mkv.astype(BF16))


ROUTE_COLS = N_GROUPS + N_EXPERTS


def _trunk_body(x_ref, o1_ref, o2_ref, o3_ref, l1_ref, l2_ref, l3_ref, ob_ref,
                wo_a_ref, wo_b_ref, gq_ref, wq_ref, k_ref, v_ref, wmo_ref,
                gf_ref, wr_hi_ref, wr_lo_ref,
                x2_ref, h2_ref, route_ref, count_ref, run_ref, *, tm):
    first = jnp.logical_and(pl.program_id(0) == 0, pl.program_id(1) == 0)

    @pl.when(first)
    def _():
        run_ref[...] = jnp.zeros_like(run_ref)

    l1, l2, l3 = l1_ref[...], l2_ref[...], l3_ref[...]
    mx = jnp.maximum(jnp.maximum(l1, l2), l3)
    e1, e2, e3 = jnp.exp(l1 - mx), jnp.exp(l2 - mx), jnp.exp(l3 - mx)
    ya = (e1 * o1_ref[...].astype(F32) + e2 * o2_ref[...].astype(F32) + e3 * o3_ref[...].astype(F32))
    ya = (ya / (e1 + e2 + e3)).astype(BF16)
    x1 = x_ref[...] + _dot(ya, wo_a_ref[...]) + _dot(ob_ref[...], wo_b_ref[...])

    q = _dot(_rms(x1, gq_ref[...]).astype(BF16), wq_ref[...]).astype(BF16)
    heads = []
    for h in range(MEM_HEADS):
        cs = slice(h * MEM_HEAD_DIM, (h + 1) * MEM_HEAD_DIM)
        s = _dot_nt(q[:, cs], k_ref[:, cs])
        p = jnp.exp(s - jnp.max(s, axis=-1, keepdims=True))
        den = jnp.sum(p, axis=-1, keepdims=True)
        heads.append((_dot(p.astype(BF16), v_ref[:, cs]) / den).astype(BF16))
    x2 = x1 + _dot(jnp.concatenate(heads, axis=-1), wmo_ref[...])
    x2_ref[...] = x2

    h2 = _rms(x2, gf_ref[...])
    h2_ref[...] = h2
    h_hi, h_lo = _split2(h2)
    wr_hi = wr_hi_ref[...]
    logits = _dot(h_hi, wr_hi) + _dot(h_lo, wr_hi) + _dot(h_hi, wr_lo_ref[...])
    lane = lax.broadcasted_iota(jnp.int32, (tm, LANES), 1)
    big = jnp.int32(LANES)
    is_g = lane < N_GROUPS
    lg = jnp.where(is_g, logits, NEG_BIG)
    mg = jnp.max(lg, axis=-1, keepdims=True)
    g_idx = jnp.min(jnp.where(jnp.logical_and(is_g, lg == mg), lane, big), axis=-1, keepdims=True)
    g_w = 1.0 / jnp.sum(jnp.exp(lg - mg), axis=-1, keepdims=True)
    lo_lane = N_GROUPS + g_idx * EXPERTS_PER_GROUP
    in_grp = jnp.logical_and(lane >= lo_lane, lane < lo_lane + EXPERTS_PER_GROUP)
    le = jnp.where(in_grp, logits, NEG_BIG)
    m1 = jnp.max(le, axis=-1, keepdims=True)
    i1 = jnp.min(jnp.where(jnp.logical_and(in_grp, le == m1), lane, big), axis=-1, keepdims=True)
    le2 = jnp.where(lane == i1, NEG_BIG, le)
    m2 = jnp.max(le2, axis=-1, keepdims=True)
    i2 = jnp.min(jnp.where(jnp.logical_and(in_grp, le2 == m2), lane, big), axis=-1, keepdims=True)
    r2 = jnp.exp(m2 - m1)
    gate1 = g_w / (1.0 + r2)
    gate2 = g_w * r2 / (1.0 + r2)

    onehot = jnp.logical_or(lane == i1, lane == i2)
    oh = jnp.where(onehot, 1.0, 0.0)
    r_i = lax.broadcasted_iota(jnp.int32, (tm, tm), 0)
    c_i = lax.broadcasted_iota(jnp.int32, (tm, tm), 1)
    tri = jnp.where(c_i < r_i, 1.0, 0.0).astype(BF16)
    before = _dot(tri, oh.astype(BF16)) + run_ref[...]
    rank1 = jnp.sum(jnp.where(lane == i1, before, 0.0), axis=-1, keepdims=True)
    rank2 = jnp.sum(jnp.where(lane == i2, before, 0.0), axis=-1, keepdims=True)
    run_ref[...] = run_ref[...] + jnp.sum(oh, axis=0, keepdims=True)
    count_ref[...] = run_ref[...]

    e1f = (i1 - N_GROUPS).astype(F32)
    e2f = (i2 - N_GROUPS).astype(F32)
    route = jnp.where(lane == 0, e1f, 0.0)
    for j, val in enumerate((e2f, gate1, gate2, rank1, rank2), start=1):
        route = jnp.where(lane == j, val, route)
    route_ref[...] = route


def _trunk(x2d, branches, ob, mem_k, mem_v, w_out, norm_mem_q, w_mq, w_mo, norm_ffn,
           w_router_g, w_router_e, bsz, seq, tm=256):
    n = bsz * seq
    nblk = seq // tm
    (o1, l1), (o2, l2), (o3, l3) = branches
    wo = w_out.astype(BF16)
    wr = jnp.concatenate([w_router_g, jnp.moveaxis(w_router_e, 0, 1).reshape(D_MODEL, N_EXPERTS)], axis=1)
    wr = jnp.pad(wr, ((0, 0), (0, LANES - ROUTE_COLS)))
    wr_hi, wr_lo = _split2(wr)
    rows = lambda w: pl.BlockSpec((tm, w), lambda b, i: (b * nblk + i, 0))
    full = lambda shape: pl.BlockSpec(shape, lambda b, i: (0, 0))
    memspec = pl.BlockSpec((None, mem_k.shape[1], mem_k.shape[2]), lambda b, i: (b, 0, 0))
    return pl.pallas_call(
        functools.partial(_trunk_body, tm=tm),
        grid=(bsz, nblk),
        in_specs=[rows(D_MODEL)] + [rows(SWA_WIDTH)] * 7
                 + [full((SWA_WIDTH, D_MODEL)), full((GDN_WIDTH, D_MODEL)), full((1, D_MODEL)),
                    full((D_MODEL, D_MODEL)), memspec, memspec, full((D_MODEL, D_MODEL)),
                    full((1, D_MODEL)), full((D_MODEL, LANES)), full((D_MODEL, LANES))],
        out_specs=[rows(D_MODEL), rows(D_MODEL), rows(LANES), full((1, LANES))],
        out_shape=[jax.ShapeDtypeStruct((n, D_MODEL), F32), jax.ShapeDtypeStruct((n, D_MODEL), F32),
                   jax.ShapeDtypeStruct((n, LANES), F32), jax.ShapeDtypeStruct((1, LANES), F32)],
        scratch_shapes=[pltpu.VMEM((1, LANES), F32)],
        compiler_params=_params("arbitrary", "arbitrary"),
        name="trunk",
    )(x2d, o1, o2, o3, l1, l2, l3, ob, wo[:SWA_WIDTH], wo[SWA_WIDTH:], norm_mem_q.reshape(1, D_MODEL),
      w_mq.astype(BF16), mem_k, mem_v, w_mo.astype(BF16), norm_ffn.reshape(1, D_MODEL), wr_hi, wr_lo)


def _dispatch_body(dest_ref, h_ref, xs_in_ref, xs_ref, sem, *, tm):
    del xs_in_ref

    def copy(j):
        t = j // TOP_K
        return pltpu.make_async_copy(h_ref.at[pl.ds(t, 1)], xs_ref.at[pl.ds(dest_ref[0, 0, j], 1)], sem)

    def issue(j, c):
        copy(j).start()
        return c

    def drain(j, c):
        copy(j).wait()
        return c

    lax.fori_loop(0, tm * TOP_K, issue, 0)
    lax.fori_loop(0, tm * TOP_K, drain, 0)


def _dispatch(h2, dest, n_slots, tm=512):
    n = h2.shape[0]
    dest3 = dest.reshape(n // tm, 1, tm * TOP_K)
    xs0 = jnp.zeros((n_slots, D_MODEL), F32)
    return pl.pallas_call(
        functools.partial(_dispatch_body, tm=tm),
        grid=(n // tm,),
        in_specs=[pl.BlockSpec((1, 1, tm * TOP_K), lambda i: (i, 0, 0), memory_space=pltpu.SMEM),
                  pl.BlockSpec((tm, D_MODEL), lambda i: (i, 0)),
                  pl.BlockSpec(memory_space=pl.ANY)],
        out_specs=pl.BlockSpec(memory_space=pl.ANY),
        out_shape=jax.ShapeDtypeStruct((n_slots, D_MODEL), F32),
        scratch_shapes=[pltpu.SemaphoreType.DMA(())],
        input_output_aliases={2: 0},
        compiler_params=_params("arbitrary"),
        name="moe_dispatch",
    )(dest3, h2, xs0)


def _expert_body(be_ref, nused_ref, x_ref, wg_ref, wu_ref, wd_ref, y_ref):
    i = pl.program_id(0)

    @pl.when(i < nused_ref[0])
    def _():
        x = x_ref[...].astype(BF16)
        a = _dot(x, wg_ref[...])
        b = _dot(x, wu_ref[...])
        hid = (a * jax.nn.sigmoid(a) * b).astype(BF16)
        y_ref[...] = _dot(hid, wd_ref[...])

    @pl.when(i >= nused_ref[0])
    def _():
        y_ref[...] = jnp.zeros_like(y_ref)


def _experts(xs, block_expert, n_used, w_gate, w_up, w_down):
    n_slots = xs.shape[0]
    n_blocks = n_slots // ROUTE_BLOCK
    grid_spec = pltpu.PrefetchScalarGridSpec(
        num_scalar_prefetch=2,
        grid=(n_blocks,),
        in_specs=[pl.BlockSpec((ROUTE_BLOCK, D_MODEL), lambda i, be, nu: (i, 0)),
                  pl.BlockSpec((None, D_MODEL, D_EXPERT), lambda i, be, nu: (be[i], 0, 0)),
                  pl.BlockSpec((None, D_MODEL, D_EXPERT), lambda i, be, nu: (be[i], 0, 0)),
                  pl.BlockSpec((None, D_EXPERT, D_MODEL), lambda i, be, nu: (be[i], 0, 0))],
        out_specs=pl.BlockSpec((ROUTE_BLOCK, D_MODEL), lambda i, be, nu: (i, 0)),
    )
    return pl.pallas_call(
        _expert_body,
        grid_spec=grid_spec,
        out_shape=jax.ShapeDtypeStruct((n_slots, D_MODEL), F32),
        compiler_params=_params("arbitrary"),
        name="moe_experts",
    )(block_expert, n_used, xs, w_gate, w_up, w_down)


def _combine_body(dest_ref, x_ref, route_ref, g_ref, yb_ref, o_ref, buf_ref, sem, *, tm):
    def copy(j):
        t = j // TOP_K
        k = j % TOP_K
        return pltpu.make_async_copy(yb_ref.at[pl.ds(dest_ref[0, 0, j], 1)], buf_ref.at[k, pl.ds(t, 1)], sem)

    def issue(j, c):
        copy(j).start()
        return c

    def drain(j, c):
        copy(j).wait()
        return c

    lax.fori_loop(0, tm * TOP_K, issue, 0)
    lax.fori_loop(0, tm * TOP_K, drain, 0)
    route = route_ref[...]
    y = route[:, 2:3] * buf_ref[0] + route[:, 3:4] * buf_ref[1]
    o_ref[...] = _rms(x_ref[...] + y, g_ref[...])


def _combine(x2, route, dest, yb, norm_final, tm=256):
    n = x2.shape[0]
    dest3 = dest.reshape(n // tm, 1, tm * TOP_K)
    return pl.pallas_call(
        functools.partial(_combine_body, tm=tm),
        grid=(n // tm,),
        in_specs=[pl.BlockSpec((1, 1, tm * TOP_K), lambda i: (i, 0, 0), memory_space=pltpu.SMEM),
                  pl.BlockSpec((tm, D_MODEL), lambda i: (i, 0)),
                  pl.BlockSpec((tm, LANES), lambda i: (i, 0)),
                  pl.BlockSpec((1, D_MODEL), lambda i: (0, 0)),
                  pl.BlockSpec(memory_space=pl.ANY)],
        out_specs=pl.BlockSpec((tm, D_MODEL), lambda i: (i, 0)),
        out_shape=jax.ShapeDtypeStruct((n, D_MODEL), F32),
        scratch_shapes=[pltpu.VMEM((TOP_K, tm, D_MODEL), F32), pltpu.SemaphoreType.DMA(())],
        compiler_params=_params("arbitrary"),
        name="moe_combine",
    )(dest3, x2, route, norm_final.reshape(1, D_MODEL), yb)


def _moe(x2, h2, route, counts, w_gate, w_up, w_down, norm_final):
    n = x2.shape[0]
    m_slots = n * TOP_K
    n_blocks = -(-(m_slots + N_EXPERTS * (ROUTE_BLOCK - 1)) // ROUTE_BLOCK)
    cnt = counts[0, N_GROUPS:N_GROUPS + N_EXPERTS].astype(jnp.int32)
    padded = (cnt + ROUTE_BLOCK - 1) // ROUTE_BLOCK * ROUTE_BLOCK
    pad_end = jnp.cumsum(padded)
    seg_start = pad_end - padded
    eid = route[:, 0:TOP_K].astype(jnp.int32)
    rank = route[:, 4:4 + TOP_K].astype(jnp.int32)
    dest = (seg_start[eid] + rank).reshape(-1)
    block_expert = jnp.minimum(
        jnp.searchsorted(pad_end, jnp.arange(n_blocks, dtype=jnp.int32) * ROUTE_BLOCK, side='right'),
        N_EXPERTS - 1).astype(jnp.int32)
    n_used = (pad_end[-1:] // ROUTE_BLOCK).astype(jnp.int32)
    xs = _dispatch(h2, dest, n_blocks * ROUTE_BLOCK)
    yb = _experts(xs, block_expert, n_used, w_gate, w_up, w_down)
    return _combine(x2, route, dest, yb, norm_final)


def _gdn_scan_jax(q, k, v, g, beta):
    B, T, H, dk = q.shape
    C = CHUNK
    N = T // C
    chunks = lambda t: t.reshape(B, N, C, H, -1).transpose(1, 0, 3, 2, 4)
    qc, kc, vc = chunks(q), chunks(k), chunks(v)
    gc = jnp.cumsum(g.reshape(B, N, C, H).transpose(1, 0, 3, 2), axis=-1)
    bc = beta.reshape(B, N, C, H).transpose(1, 0, 3, 2)[..., None]
    lower = jnp.tril(jnp.ones((C, C), bool))
    strict = jnp.tril(jnp.ones((C, C), bool), -1)
    decay = jnp.exp(jnp.where(lower, gc[..., :, None] - gc[..., None, :], -jnp.inf))
    kbeta = kc * bc
    lmat = jnp.where(strict, jnp.einsum('nbhid,nbhjd->nbhij', kbeta, kc) * decay, 0.0)
    solve = lambda rhs: lax.linalg.triangular_solve(lmat, rhs, left_side=True, lower=True, unit_diagonal=True)
    u = solve(vc * bc)
    w = solve(kbeta * jnp.exp(gc)[..., None])
    a_qk = jnp.einsum('nbhid,nbhjd->nbhij', qc, kc) * decay

    def step(state, inp):
        q_i, k_i, u_i, w_i, g_i, a_i = inp
        v_new = u_i - jnp.einsum('bhck,bhkv->bhcv', w_i, state)
        o_i = (jnp.einsum('bhck,bhkv->bhcv', q_i * jnp.exp(g_i)[..., None], state)
               + jnp.einsum('bhij,bhjv->bhiv', a_i, v_new))
        g_last = g_i[..., -1:]
        state = (state * jnp.exp(g_last)[..., None]
                 + jnp.einsum('bhck,bhcv->bhkv', k_i * jnp.exp(g_last - g_i)[..., None], v_new))
        return state, o_i

    s0 = jnp.zeros((B, H, dk, dk), F32)
    _, o = lax.scan(step, s0, (qc, kc, u, w, gc, a_qk))
    return o.transpose(1, 0, 3, 2, 4).reshape(B, T, H, dk)


def _gdn_jax(gd, z, ab, conv_w, a_log, dt_bias, gdn_norm, bsz, seq):
    c = lax.conv_general_dilated(
        gd.astype(F32).reshape(bsz, seq, 3 * GDN_WIDTH), conv_w[:, None, :], window_strides=(1,),
        padding=[(CONV_K // 2, CONV_K // 2)], dimension_numbers=('NWC', 'WIO', 'NWC'),
        feature_group_count=3 * GDN_WIDTH)
    c = jax.nn.silu(c)
    l2 = lambda t: t * lax.rsqrt(jnp.sum(t * t, axis=-1, keepdims=True) + EPS)
    qb, kb, vb = [t.reshape(bsz, seq, GDN_HEADS, HEAD_DIM) for t in jnp.split(c, 3, axis=-1)]
    qb = l2(qb) * (HEAD_DIM ** -0.5)
    kb = l2(kb)
    ab4 = ab[:, :4 * GDN_HEADS].reshape(bsz, seq, 4, GDN_HEADS)
    beta = jax.nn.sigmoid(ab4[:, :, 0:2])
    g = -jnp.exp(a_log) * jax.nn.softplus(ab4[:, :, 2:4] + dt_bias)
    fwd = _gdn_scan_jax(qb, kb, vb, g[:, :, 0], beta[:, :, 0])
    flip = lambda t: t[:, ::-1]
    bwd = flip(_gdn_scan_jax(flip(qb), flip(kb), flip(vb), flip(g[:, :, 1]), flip(beta[:, :, 1])))
    ob = fwd + bwd
    ob = (ob * lax.rsqrt(jnp.mean(ob * ob, axis=-1, keepdims=True) + EPS) * gdn_norm
          * jax.nn.silu(z.astype(F32).reshape(bsz, seq, GDN_HEADS, HEAD_DIM)))
    return ob.reshape(bsz * seq, GDN_WIDTH).astype(BF16)


def _encoder(x, mem, norm_mix, w_in, conv_w, a_log, dt_bias, gdn_norm, w_out, norm_mem_q, norm_mem_kv,
             w_mq, w_mkv, w_mo, norm_ffn, w_router_g, w_router_e, w_gate, w_up, w_down, norm_final):
    bsz, seq, _ = x.shape
    x2d = x.reshape(bsz * seq, D_MODEL)
    qkv, gd, z, ab = _inproj(x2d, norm_mix, w_in)
    slopes = jnp.exp2(-8.0 * jnp.arange(1, SWA_HEADS + 1, dtype=F32) / SWA_HEADS)
    branches = [_attn_branch(qkv, slopes, d, bsz, seq) for d in DILATIONS]
    ob = _gdn_jax(gd, z, ab, conv_w, a_log, dt_bias, gdn_norm, bsz, seq)
    mem_k, mem_v = _memkv(mem.reshape(-1, D_MODEL), norm_mem_kv, w_mkv)
    mem_k = mem_k.reshape(bsz, -1, MEM_HEADS * MEM_HEAD_DIM)
    mem_v = mem_v.reshape(bsz, -1, MEM_HEADS * MEM_HEAD_DIM)
    x2, h2, route, counts = _trunk(x2d, branches, ob, mem_k, mem_v, w_out, norm_mem_q, w_mq, w_mo,
                                   norm_ffn, w_router_g, w_router_e, bsz, seq)
    y = _moe(x2, h2, route, counts, w_gate, w_up, w_down, norm_final)
    return y.reshape(bsz, seq, D_MODEL)


def kernel(x_prompt, x_sample, mem_prompt, mem_sample, norm_mix, w_in, conv_w, a_log, dt_bias, gdn_norm,
           w_out, norm_mem_q, norm_mem_kv, w_mq, w_mkv, w_mo, norm_ffn, w_router_g, w_router_e,
           w_gate, w_up, w_down, norm_final):
    p = dict(norm_mix=norm_mix[0], w_in=w_in[0], conv_w=conv_w[0], a_log=a_log[0], dt_bias=dt_bias[0],
             gdn_norm=gdn_norm[0], w_out=w_out[0], norm_mem_q=norm_mem_q[0], norm_mem_kv=norm_mem_kv[0],
             w_mq=w_mq[0], w_mkv=w_mkv[0], w_mo=w_mo[0], norm_ffn=norm_ffn[0], w_router_g=w_router_g[0],
             w_router_e=w_router_e[0], w_gate=w_gate[0].astype(BF16), w_up=w_up[0].astype(BF16),
             w_down=w_down[0].astype(BF16), norm_final=norm_final)
    return (_encoder(x_prompt, mem_prompt, **p), _encoder(x_sample, mem_sample, **p))
```

```python
import functools

import jax
import jax.numpy as jnp
import numpy as np
from jax import lax
from jax.experimental import pallas as pl
from jax.experimental.pallas import tpu as pltpu

F32 = jnp.float32
BF16 = jnp.bfloat16

D_MODEL = 1024
HEAD_DIM = 64
SWA_HEADS = 8
GDN_HEADS = 8
SWA_WIDTH = SWA_HEADS * HEAD_DIM
GDN_WIDTH = GDN_HEADS * HEAD_DIM
DILATIONS = (1, 4, 16)
ATT_W = 64
CONV_K = 5
CHUNK = 64
MEM_HEADS = 4
MEM_HEAD_DIM = 256
N_GROUPS = 4
EXPERTS_PER_GROUP = 8
N_EXPERTS = 32
TOP_K = 2
D_EXPERT = 512
ROUTE_BLOCK = 256
EPS = 1e-6

LANES = 128
VMEM_LIMIT = 56 * 1024 * 1024
NEG_BIG = -1e30


def _dot(a, b):
    return jnp.dot(a, b, preferred_element_type=F32)


def _dot_nt(a, b):
    return lax.dot_general(a, b, (((1,), (1,)), ((), ())), preferred_element_type=F32)


def _dot_tn(a, b):
    return lax.dot_general(a, b, (((0,), (0,)), ((), ())), preferred_element_type=F32)


def _split2(x):
    hi = x.astype(BF16)
    lo = (x - hi.astype(F32)).astype(BF16)
    return hi, lo


def _split3(x):
    hi = x.astype(BF16)
    r = x - hi.astype(F32)
    mid = r.astype(BF16)
    lo = (r - mid.astype(F32)).astype(BF16)
    return hi, mid, lo


def _rms(x, g):
    return x * lax.rsqrt(jnp.mean(x * x, axis=-1, keepdims=True) + EPS) * g


def _params(*sem):
    return pltpu.CompilerParams(dimension_semantics=sem, vmem_limit_bytes=VMEM_LIMIT)


def _inproj_body(x_ref, g_ref, wa_ref, wgd_ref, wz_ref, wab_hi_ref, wab_lo_ref,
                 qkv_ref, gd_ref, z_ref, ab_ref):
    h = _rms(x_ref[...], g_ref[...])
    h_hi, h_lo = _split2(h)
    qkv_ref[...] = _dot(h_hi, wa_ref[...]).astype(BF16)
    gd_ref[...] = _dot(h_hi, wgd_ref[...]).astype(BF16)
    z_ref[...] = _dot(h_hi, wz_ref[...]).astype(BF16)
    wab_hi = wab_hi_ref[...]
    ab = _dot(h_hi, wab_hi) + _dot(h_lo, wab_hi) + _dot(h_hi, wab_lo_ref[...])
    ab_ref[...] = ab


def _inproj(x2, norm_mix, w_in, tm=512):
    n = x2.shape[0]
    wa = w_in[:, :3 * SWA_WIDTH].astype(BF16)
    wgd = w_in[:, 3 * SWA_WIDTH:3 * SWA_WIDTH + 3 * GDN_WIDTH].astype(BF16)
    wz = w_in[:, 3 * SWA_WIDTH + 3 * GDN_WIDTH:3 * SWA_WIDTH + 4 * GDN_WIDTH].astype(BF16)
    wab = jnp.pad(w_in[:, 3 * SWA_WIDTH + 4 * GDN_WIDTH:], ((0, 0), (0, LANES - 4 * GDN_HEADS)))
    wab_hi, wab_lo = _split2(wab)
    full = lambda shape: pl.BlockSpec(shape, lambda i: (0, 0))
    rows = lambda w: pl.BlockSpec((tm, w), lambda i: (i, 0))
    return pl.pallas_call(
        _inproj_body,
        grid=(n // tm,),
        in_specs=[rows(D_MODEL), full((1, D_MODEL)), full(wa.shape), full(wgd.shape), full(wz.shape),
                  full(wab_hi.shape), full(wab_lo.shape)],
        out_specs=[rows(3 * SWA_WIDTH), rows(3 * GDN_WIDTH), rows(GDN_WIDTH), rows(LANES)],
        out_shape=[jax.ShapeDtypeStruct((n, 3 * SWA_WIDTH), BF16),
                   jax.ShapeDtypeStruct((n, 3 * GDN_WIDTH), BF16),
                   jax.ShapeDtypeStruct((n, GDN_WIDTH), BF16),
                   jax.ShapeDtypeStruct((n, LANES), F32)],
        compiler_params=_params("parallel"),
        name="inproj",
    )(x2, norm_mix.reshape(1, D_MODEL), wa, wgd, wz, wab_hi, wab_lo)


ATT_QB = 128
ATT_KB = ATT_QB + 2 * ATT_W


def _attn_body(slope_ref, q_ref, k_ref, v_ref, o_ref, lse_ref, bias_ref, *, dil, n_qb, seq_l):
    pair = pl.program_id(2)
    lb = pl.program_id(3)
    lane = lax.broadcasted_iota(jnp.int32, (1, LANES), 1)
    left = lane < HEAD_DIM

    row = lax.broadcasted_iota(jnp.int32, (ATT_QB, ATT_KB), 0)
    col = lax.broadcasted_iota(jnp.int32, (ATT_QB, ATT_KB), 1)
    for var in range(3):
        rel = jnp.abs(col - row - var * ATT_W)
        dist = (rel * dil).astype(F32)
        for h in range(2):
            slope = slope_ref[pair * 2 + h]
            bias_ref[var * 2 + h] = jnp.where(rel <= ATT_W, -slope * dist, NEG_BIG)

    def qblock(qi, carry):
        n0 = (lb * n_qb + qi) * ATT_QB
        kstart = jnp.clip(n0 - ATT_W, 0, seq_l - ATT_KB)
        kstart = pl.multiple_of(kstart, ATT_W)
        var = (n0 - kstart) // ATT_W
        q = q_ref[pl.ds(pl.multiple_of(qi * ATT_QB, ATT_QB), ATT_QB), :]
        q = q * jnp.asarray(HEAD_DIM ** -0.5, BF16)
        kb = k_ref[pl.ds(kstart, ATT_KB), :]
        vb = v_ref[pl.ds(kstart, ATT_KB), :]
        accs, maxes = [], []
        for h in range(2):
            mine = left if h == 0 else jnp.logical_not(left)
            qh = jnp.where(mine, q, jnp.zeros_like(q))
            s = _dot_nt(qh, kb) + bias_ref[var * 2 + h]
            m = jnp.max(s, axis=-1, keepdims=True)
            p = jnp.exp(s - m).astype(BF16)
            vh = jnp.where(mine, vb, jnp.ones_like(vb))
            accs.append(_dot(p, vh))
            maxes.append(m)
        num = jnp.where(left, accs[0], accs[1])
        den = pltpu.roll(jnp.where(left, accs[1], accs[0]), HEAD_DIM, 1)
        mx = jnp.where(left, maxes[0], maxes[1])
        rows = pl.ds(pl.multiple_of(qi * ATT_QB, ATT_QB), ATT_QB)
        o_ref[rows, :] = (num / den).astype(BF16)
        lse_ref[rows, :] = mx + jnp.log(den)
        return carry

    lax.fori_loop(0, n_qb, qblock, 0)


def _attn_branch(qkv, slopes, dil, bsz, seq):
    seq_l = seq // dil
    lblk = min(2048, seq_l)
    n_qb = lblk // ATT_QB
    n_pairs = SWA_HEADS // 2
    cols = 3 * SWA_WIDTH // LANES
    view = qkv.reshape(bsz, seq_l, dil * 3 * SWA_WIDTH)
    qspec = pl.BlockSpec((None, lblk, LANES), lambda b, r, p, l: (b, l, r * cols + p))
    kspec = pl.BlockSpec((None, seq_l, LANES), lambda b, r, p, l: (b, 0, r * cols + n_pairs + p))
    vspec = pl.BlockSpec((None, seq_l, LANES), lambda b, r, p, l: (b, 0, r * cols + 2 * n_pairs + p))
    ospec = pl.BlockSpec((None, lblk, LANES), lambda b, r, p, l: (b, l, r * n_pairs + p))
    o, lse = pl.pallas_call(
        functools.partial(_attn_body, dil=dil, n_qb=n_qb, seq_l=seq_l),
        grid=(bsz, dil, n_pairs, seq_l // lblk),
        in_specs=[pl.BlockSpec(memory_space=pltpu.SMEM), qspec, kspec, vspec],
        out_specs=[ospec, ospec],
        out_shape=[jax.ShapeDtypeStruct((bsz, seq_l, dil * SWA_WIDTH), BF16),
                   jax.ShapeDtypeStruct((bsz, seq_l, dil * SWA_WIDTH), F32)],
        scratch_shapes=[pltpu.VMEM((6, ATT_QB, ATT_KB), F32)],
        compiler_params=_params("parallel", "parallel", "parallel", "arbitrary"),
        name=f"dilated_attn_d{dil}",
    )(slopes, view, view, view)
    return o.reshape(bsz * seq, SWA_WIDTH), lse.reshape(bsz * seq, SWA_WIDTH)


def _memkv_body(m_ref, g_ref, w_ref, k_ref, v_ref):
    h = _rms(m_ref[...], g_ref[...]).astype(BF16)
    kv = _dot(h, w_ref[...])
    width = MEM_HEADS * MEM_HEAD_DIM
    k_ref[...] = (kv[:, :width] * (MEM_HEAD_DIM ** -0.5)).astype(BF16)
    v_ref[...] = kv[:, width:].astype(BF16)


def _memkv(mem2, norm_kv, w_mkv, tm=256):
    n = mem2.shape[0]
    width = MEM_HEADS * MEM_HEAD_DIM
    return pl.pallas_call(
        _memkv_body,
        grid=(n // tm,),
        in_specs=[pl.BlockSpec((tm, D_MODEL), lambda i: (i, 0)),
                  pl.BlockSpec((1, D_MODEL), lambda i: (0, 0)),
                  pl.BlockSpec((D_MODEL, 2 * width), lambda i: (0, 0))],
        out_specs=[pl.BlockSpec((tm, width), lambda i: (i, 0))] * 2,
        out_shape=[jax.ShapeDtypeStruct((n, width), BF16)] * 2,
        compiler_params=_params("parallel"),
        name="mem_kv",
    )(mem2, norm_kv.reshape(1, D_MODEL), w_mkv.astype(BF16))


ROUTE_COLS = N_GROUPS + N_EXPERTS


def _trunk_body(x_ref, o1_ref, o2_ref, o3_ref, l1_ref, l2_ref, l3_ref, ob_ref,
                wo_a_ref, wo_b_ref, gq_ref, wq_ref, k_ref, v_ref, wmo_ref,
                gf_ref, wr_hi_ref, wr_lo_ref,
                x2_ref, h2_ref, route_ref, count_ref, run_ref, *, tm):
    first = jnp.logical_and(pl.program_id(0) == 0, pl.program_id(1) == 0)

    @pl.when(first)
    def _():
        run_ref[...] = jnp.zeros_like(run_ref)

    l1, l2, l3 = l1_ref[...], l2_ref[...], l3_ref[...]
    mx = jnp.maximum(jnp.maximum(l1, l2), l3)
    e1, e2, e3 = jnp.exp(l1 - mx), jnp.exp(l2 - mx), jnp.exp(l3 - mx)
    ya = (e1 * o1_ref[...].astype(F32) + e2 * o2_ref[...].astype(F32) + e3 * o3_ref[...].astype(F32))
    ya = (ya / (e1 + e2 + e3)).astype(BF16)
    x1 = x_ref[...] + _dot(ya, wo_a_ref[...]) + _dot(ob_ref[...], wo_b_ref[...])

    q = _dot(_rms(x1, gq_ref[...]).astype(BF16), wq_ref[...]).astype(BF16)
    heads = []
    for h in range(MEM_HEADS):
        cs = slice(h * MEM_HEAD_DIM, (h + 1) * MEM_HEAD_DIM)
        s = _dot_nt(q[:, cs], k_ref[:, cs])
        p = jnp.exp(s - jnp.max(s, axis=-1, keepdims=True))
        den = jnp.sum(p, axis=-1, keepdims=True)
        heads.append((_dot(p.astype(BF16), v_ref[:, cs]) / den).astype(BF16))
    x2 = x1 + _dot(jnp.concatenate(heads, axis=-1), wmo_ref[...])
    x2_ref[...] = x2

    h2 = _rms(x2, gf_ref[...])
    h2_ref[...] = h2
    h_hi, h_lo = _split2(h2)
    wr_hi = wr_hi_ref[...]
    logits = _dot(h_hi, wr_hi) + _dot(h_lo, wr_hi) + _dot(h_hi, wr_lo_ref[...])
    lane = lax.broadcasted_iota(jnp.int32, (tm, LANES), 1)
    big = jnp.int32(LANES)
    is_g = lane < N_GROUPS
    lg = jnp.where(is_g, logits, NEG_BIG)
    mg = jnp.max(lg, axis=-1, keepdims=True)
    g_idx = jnp.min(jnp.where(jnp.logical_and(is_g, lg == mg), lane, big), axis=-1, keepdims=True)
    g_w = 1.0 / jnp.sum(jnp.exp(lg - mg), axis=-1, keepdims=True)
    lo_lane = N_GROUPS + g_idx * EXPERTS_PER_GROUP
    in_grp = jnp.logical_and(lane >= lo_lane, lane < lo_lane + EXPERTS_PER_GROUP)
    le = jnp.where(in_grp, logits, NEG_BIG)
    m1 = jnp.max(le, axis=-1, keepdims=True)
    i1 = jnp.min(jnp.where(jnp.logical_and(in_grp, le == m1), lane, big), axis=-1, keepdims=True)
    le2 = jnp.where(lane == i1, NEG_BIG, le)
    m2 = jnp.max(le2, axis=-1, keepdims=True)
    i2 = jnp.min(jnp.where(jnp.logical_and(in_grp, le2 == m2), lane, big), axis=-1, keepdims=True)
    r2 = jnp.exp(m2 - m1)
    gate1 = g_w / (1.0 + r2)
    gate2 = g_w * r2 / (1.0 + r2)

    onehot = jnp.logical_or(lane == i1, lane == i2)
    oh = jnp.where(onehot, 1.0, 0.0)
    r_i = lax.broadcasted_iota(jnp.int32, (tm, tm), 0)
    c_i = lax.broadcasted_iota(jnp.int32, (tm, tm), 1)
    tri = jnp.where(c_i < r_i, 1.0, 0.0).astype(BF16)
    before = _dot(tri, oh.astype(BF16)) + run_ref[...]
    rank1 = jnp.sum(jnp.where(lane == i1, before, 0.0), axis=-1, keepdims=True)
    rank2 = jnp.sum(jnp.where(lane == i2, before, 0.0), axis=-1, keepdims=True)
    run_ref[...] = run_ref[...] + jnp.sum(oh, axis=0, keepdims=True)
    count_ref[...] = run_ref[...]

    e1f = (i1 - N_GROUPS).astype(F32)
    e2f = (i2 - N_GROUPS).astype(F32)
    route = jnp.where(lane == 0, e1f, 0.0)
    for j, val in enumerate((e2f, gate1, gate2, rank1, rank2), start=1):
        route = jnp.where(lane == j, val, route)
    route_ref[...] = route


def _trunk(x2d, branches, ob, mem_k, mem_v, w_out, norm_mem_q, w_mq, w_mo, norm_ffn,
           w_router_g, w_router_e, bsz, seq, tm=256):
    n = bsz * seq
    nblk = seq // tm
    (o1, l1), (o2, l2), (o3, l3) = branches
    wo = w_out.astype(BF16)
    wr = jnp.concatenate([w_router_g, jnp.moveaxis(w_router_e, 0, 1).reshape(D_MODEL, N_EXPERTS)], axis=1)
    wr = jnp.pad(wr, ((0, 0), (0, LANES - ROUTE_COLS)))
    wr_hi, wr_lo = _split2(wr)
    rows = lambda w: pl.BlockSpec((tm, w), lambda b, i: (b * nblk + i, 0))
    full = lambda shape: pl.BlockSpec(shape, lambda b, i: (0, 0))
    memspec = pl.BlockSpec((None, mem_k.shape[1], mem_k.shape[2]), lambda b, i: (b, 0, 0))
    return pl.pallas_call(
        functools.partial(_trunk_body, tm=tm),
        grid=(bsz, nblk),
        in_specs=[rows(D_MODEL)] + [rows(SWA_WIDTH)] * 7
                 + [full((SWA_WIDTH, D_MODEL)), full((GDN_WIDTH, D_MODEL)), full((1, D_MODEL)),
                    full((D_MODEL, D_MODEL)), memspec, memspec, full((D_MODEL, D_MODEL)),
                    full((1, D_MODEL)), full((D_MODEL, LANES)), full((D_MODEL, LANES))],
        out_specs=[rows(D_MODEL), rows(D_MODEL), rows(LANES), full((1, LANES))],
        out_shape=[jax.ShapeDtypeStruct((n, D_MODEL), F32), jax.ShapeDtypeStruct((n, D_MODEL), F32),
                   jax.ShapeDtypeStruct((n, LANES), F32), jax.ShapeDtypeStruct((1, LANES), F32)],
        scratch_shapes=[pltpu.VMEM((1, LANES), F32)],
        compiler_params=_params("arbitrary", "arbitrary"),
        name="trunk",
    )(x2d, o1, o2, o3, l1, l2, l3, ob, wo[:SWA_WIDTH], wo[SWA_WIDTH:], norm_mem_q.reshape(1, D_MODEL),
      w_mq.astype(BF16), mem_k, mem_v, w_mo.astype(BF16), norm_ffn.reshape(1, D_MODEL), wr_hi, wr_lo)


def _dispatch_body(dest_ref, h_ref, xs_in_ref, xs_ref, sem, *, tm):
    del xs_in_ref

    def copy(j):
        t = j // TOP_K
        return pltpu.make_async_copy(h_ref.at[pl.ds(t, 1)], xs_ref.at[pl.ds(dest_ref[0, 0, j], 1)], sem)

    def issue(j, c):
        copy(j).start()
        return c

    def drain(j, c):
        copy(j).wait()
        return c

    lax.fori_loop(0, tm * TOP_K, issue, 0)
    lax.fori_loop(0, tm * TOP_K, drain, 0)


def _dispatch(h2, dest, n_slots, tm=512):
    n = h2.shape[0]
    dest3 = dest.reshape(n // tm, 1, tm * TOP_K)
    xs0 = jnp.zeros((n_slots, D_MODEL), F32)
    return pl.pallas_call(
        functools.partial(_dispatch_body, tm=tm),
        grid=(n // tm,),
        in_specs=[pl.BlockSpec((1, 1, tm * TOP_K), lambda i: (i, 0, 0), memory_space=pltpu.SMEM),
                  pl.BlockSpec((tm, D_MODEL), lambda i: (i, 0)),
                  pl.BlockSpec(memory_space=pl.ANY)],
        out_specs=pl.BlockSpec(memory_space=pl.ANY),
        out_shape=jax.ShapeDtypeStruct((n_slots, D_MODEL), F32),
        scratch_shapes=[pltpu.SemaphoreType.DMA(())],
        input_output_aliases={2: 0},
        compiler_params=_params("arbitrary"),
        name="moe_dispatch",
    )(dest3, h2, xs0)


def _expert_body(be_ref, nused_ref, x_ref, wg_ref, wu_ref, wd_ref, y_ref):
    i = pl.program_id(0)

    @pl.when(i < nused_ref[0])
    def _():
        x = x_ref[...].astype(BF16)
        a = _dot(x, wg_ref[...])
        b = _dot(x, wu_ref[...])
        hid = (a * jax.nn.sigmoid(a) * b).astype(BF16)
        y_ref[...] = _dot(hid, wd_ref[...])

    @pl.when(i >= nused_ref[0])
    def _():
        y_ref[...] = jnp.zeros_like(y_ref)


def _experts(xs, block_expert, n_used, w_gate, w_up, w_down):
    n_slots = xs.shape[0]
    n_blocks = n_slots // ROUTE_BLOCK
    grid_spec = pltpu.PrefetchScalarGridSpec(
        num_scalar_prefetch=2,
        grid=(n_blocks,),
        in_specs=[pl.BlockSpec((ROUTE_BLOCK, D_MODEL), lambda i, be, nu: (i, 0)),
                  pl.BlockSpec((None, D_MODEL, D_EXPERT), lambda i, be, nu: (be[i], 0, 0)),
                  pl.BlockSpec((None, D_MODEL, D_EXPERT), lambda i, be, nu: (be[i], 0, 0)),
                  pl.BlockSpec((None, D_EXPERT, D_MODEL), lambda i, be, nu: (be[i], 0, 0))],
        out_specs=pl.BlockSpec((ROUTE_BLOCK, D_MODEL), lambda i, be, nu: (i, 0)),
    )
    return pl.pallas_call(
        _expert_body,
        grid_spec=grid_spec,
        out_shape=jax.ShapeDtypeStruct((n_slots, D_MODEL), F32),
        compiler_params=_params("arbitrary"),
        name="moe_experts",
    )(block_expert, n_used, xs, w_gate, w_up, w_down)


def _combine_body(dest_ref, x_ref, route_ref, g_ref, yb_ref, o_ref, buf_ref, sem, *, tm):
    def copy(j):
        t = j // TOP_K
        k = j % TOP_K
        return pltpu.make_async_copy(yb_ref.at[pl.ds(dest_ref[0, 0, j], 1)], buf_ref.at[k, pl.ds(t, 1)], sem)

    def issue(j, c):
        copy(j).start()
        return c

    def drain(j, c):
        copy(j).wait()
        return c

    lax.fori_loop(0, tm * TOP_K, issue, 0)
    lax.fori_loop(0, tm * TOP_K, drain, 0)
    route = route_ref[...]
    y = route[:, 2:3] * buf_ref[0] + route[:, 3:4] * buf_ref[1]
    o_ref[...] = _rms(x_ref[...] + y, g_ref[...])


def _combine(x2, route, dest, yb, norm_final, tm=256):
    n = x2.shape[0]
    dest3 = dest.reshape(n // tm, 1, tm * TOP_K)
    return pl.pallas_call(
        functools.partial(_combine_body, tm=tm),
        grid=(n // tm,),
        in_specs=[pl.BlockSpec((1, 1, tm * TOP_K), lambda i: (i, 0, 0), memory_space=pltpu.SMEM),
                  pl.BlockSpec((tm, D_MODEL), lambda i: (i, 0)),
                  pl.BlockSpec((tm, LANES), lambda i: (i, 0)),
                  pl.BlockSpec((1, D_MODEL), lambda i: (0, 0)),
                  pl.BlockSpec(memory_space=pl.ANY)],
        out_specs=pl.BlockSpec((tm, D_MODEL), lambda i: (i, 0)),
        out_shape=jax.ShapeDtypeStruct((n, D_MODEL), F32),
        scratch_shapes=[pltpu.VMEM((TOP_K, tm, D_MODEL), F32), pltpu.SemaphoreType.DMA(())],
        compiler_params=_params("arbitrary"),
        name="moe_combine",
    )(dest3, x2, route, norm_final.reshape(1, D_MODEL), yb)


def _moe(x2, h2, route, counts, w_gate, w_up, w_down, norm_final):
    n = x2.shape[0]
    m_slots = n * TOP_K
    n_blocks = -(-(m_slots + N_EXPERTS * (ROUTE_BLOCK - 1)) // ROUTE_BLOCK)
    cnt = counts[0, N_GROUPS:N_GROUPS + N_EXPERTS].astype(jnp.int32)
    padded = (cnt + ROUTE_BLOCK - 1) // ROUTE_BLOCK * ROUTE_BLOCK
    pad_end = jnp.cumsum(padded)
    seg_start = pad_end - padded
    eid = route[:, 0:TOP_K].astype(jnp.int32)
    rank = route[:, 4:4 + TOP_K].astype(jnp.int32)
    dest = (seg_start[eid] + rank).reshape(-1)
    block_expert = jnp.minimum(
        jnp.searchsorted(pad_end, jnp.arange(n_blocks, dtype=jnp.int32) * ROUTE_BLOCK, side='right'),
        N_EXPERTS - 1).astype(jnp.int32)
    n_used = (pad_end[-1:] // ROUTE_BLOCK).astype(jnp.int32)
    xs = _dispatch(h2, dest, n_blocks * ROUTE_BLOCK)
    yb = _experts(xs, block_expert, n_used, w_gate, w_up, w_down)
    return _combine(x2, route, dest, yb, norm_final)


GDN_TB = 512
GDN_HALO = 16
GDN_GROUP = 2


def _gdn_body(gd_ref, prev_ref, next_ref, abr_ref, cw_ref, alog_ref, dt_ref, *rest, rev, final):
    if final:
        z_ref, oprev_ref, gn_ref, o_ref, xpad_ref, state_ref = rest
    else:
        o_ref, xpad_ref, state_ref = rest
    i = pl.program_id(1)
    nblk = pl.num_programs(1)
    n_chunks = GDN_TB // CHUNK
    n_pairs = GDN_HEADS // 2
    width3 = 3 * GDN_WIDTH

    @pl.when(i == 0)
    def _():
        state_ref[...] = jnp.zeros_like(state_ref)

    blk = (nblk - 1 - i) if rev else i
    pad = GDN_HALO // 2
    prev = prev_ref[pad:, :].astype(F32)
    nxt = next_ref[:pad, :].astype(F32)
    xpad_ref[:pad, :] = jnp.where(blk == 0, 0.0, prev)
    xpad_ref[pad:pad + GDN_TB, :] = gd_ref[...].astype(F32)
    xpad_ref[pad + GDN_TB:, :] = jnp.where(blk == nblk - 1, 0.0, nxt)

    lane = lax.broadcasted_iota(jnp.int32, (1, LANES), 1)
    left = lane < CHUNK
    tok = lane % CHUNK
    row = lax.broadcasted_iota(jnp.int32, (CHUNK, 1), 0)
    causal = (tok >= row) if rev else (tok <= row)
    strict = (tok > row) if rev else (tok < row)
    eye = jnp.where(tok == row, 1.0, 0.0)
    r2 = lax.broadcasted_iota(jnp.int32, (LANES, 1), 0)
    same_head = (r2 // CHUNK) == (lane // CHUNK)
    bd_ones = jnp.where(same_head, 1.0, 0.0).astype(BF16)
    tri_in = (r2 % CHUNK >= tok) if rev else (r2 % CHUNK <= tok)
    tri_bd = jnp.where(jnp.logical_and(same_head, tri_in), 1.0, 0.0).astype(BF16)
    row8 = lax.broadcasted_iota(jnp.int32, (8, 1), 0)

    def blockdiag(x):
        xb = x.astype(BF16)
        zero = jnp.zeros_like(xb)
        return jnp.concatenate([jnp.where(left, xb, zero), jnp.where(left, zero, xb)], axis=0)

    def wide_mm(x, y):
        return _dot(x.astype(BF16), blockdiag(y))

    def head_sums(x):
        hi, lo = _split2(x)
        return _dot(hi, bd_ones) + _dot(lo, bd_ones)

    def group(gi, carry):
        g0 = (n_chunks // GDN_GROUP - 1 - gi) if rev else gi
        chunks = [g0 * GDN_GROUP + ((GDN_GROUP - 1 - j) if rev else j) for j in range(GDN_GROUP)]
        units = [(ci, p) for ci in range(GDN_GROUP) for p in range(n_pairs)]
        r0 = [pl.multiple_of(c * CHUNK, CHUNK) for c in chunks]

        gates, gc_rows, gl_rows = [], [], []
        for c in chunks:
            graw = abr_ref[c]
            xg = graw + dt_ref[...]
            softplus = jnp.maximum(xg, 0.0) + jnp.log(1.0 + jnp.exp(-jnp.abs(xg)))
            g = jnp.where(row8 < n_pairs, jax.nn.sigmoid(graw), -jnp.exp(alog_ref[...]) * softplus)
            g3 = _split3(g)
            gates.append(g)
            gc_rows.append(sum(_dot(t, tri_bd) for t in g3))
            gl_rows.append(sum(_dot(t, bd_ones) for t in g3))

        def conv(ci, p, col0):
            cols = slice(col0 + p * LANES, col0 + (p + 1) * LANES)
            xa = xpad_ref[pl.ds(r0[ci], CHUNK + GDN_HALO), cols]
            acc = None
            for j in range(CONV_K):
                term = xa[pad - CONV_K // 2 + j:pad - CONV_K // 2 + j + CHUNK] * cw_ref[j:j + 1, cols]
                acc = term if acc is None else acc + term
            return acc * jax.nn.sigmoid(acc)

        qkv = {u: [conv(*u, col0) for col0 in (0, GDN_WIDTH, 2 * GDN_WIDTH)] for u in units}
        ss = {u: head_sums(jnp.concatenate([qkv[u][0] * qkv[u][0], qkv[u][1] * qkv[u][1]], axis=0))
              for u in units}
        prep = {}
        for (ci, p) in units:
            beta_r = gates[ci][p:p + 1]
            g_r = gates[ci][n_pairs + p:n_pairs + p + 1]
            lhs = jnp.concatenate([jnp.where(causal, g_r, 0.0), eye * beta_r], axis=0)
            prep[(ci, p)] = sum(_dot(t, bd_ones) for t in _split3(lhs))

        kq, kn_bd, e_gc, k_upd, decay_b = {}, {}, {}, {}, {}
        for u in units:
            ci, p = u
            qn = qkv[u][0] * lax.rsqrt(ss[u][:CHUNK] + EPS) * (HEAD_DIM ** -0.5)
            kn = qkv[u][1] * lax.rsqrt(ss[u][CHUNK:] + EPS)
            kq[u] = jnp.concatenate([kn, qn], axis=0).astype(BF16)
            kn_bd[u] = blockdiag(kn)
            beta_r = gates[ci][p:p + 1]
            gc_r = gc_rows[ci][n_pairs + p:n_pairs + p + 1]
            gl_r = gl_rows[ci][n_pairs + p:n_pairs + p + 1]
            gc_b, beta_b = prep[u][:CHUNK], prep[u][CHUNK:]
            decay_b[u] = jnp.exp(jnp.where(causal, gc_b - gc_r, NEG_BIG)) * beta_r
            e_gc[u] = jnp.exp(gc_b)
            k_upd[u] = (kn * (jnp.exp(gl_r - gc_b) * beta_b)).astype(BF16)

        kk_qk = {u: _dot_nt(kq[u], kn_bd[u]) for u in units}
        lpow = {u: jnp.where(strict, kk_qk[u][:CHUNK] * decay_b[u], 0.0) for u in units}
        amat = {u: kk_qk[u][CHUNK:] * decay_b[u] for u in units}
        tinv = {u: eye - lpow[u] for u in units}
        for _ in range(5):
            lpow = {u: wide_mm(lpow[u], lpow[u]) for u in units}
            tinv = {u: tinv[u] + wide_mm(tinv[u], lpow[u]) for u in units}

        for ci in range(GDN_GROUP):
            us = [(ci, p) for p in range(n_pairs)]
            state = {u: state_ref[u[1]] for u in us}
            pq = {u: _dot(kq[u], state[u].astype(BF16)) for u in us}
            vhat = {u: wide_mm(tinv[u], qkv[u][2] - e_gc[u] * pq[u][:CHUNK]) for u in us}
            upd = {u: _dot_tn(k_upd[u], vhat[u].astype(BF16)) for u in us}
            for u in us:
                gl_r = gl_rows[ci][n_pairs + u[1]:n_pairs + u[1] + 1]
                state_ref[u[1]] = state[u] * jnp.exp(gl_r) + jnp.where(same_head, upd[u], 0.0)
            o = {u: e_gc[u] * pq[u][CHUNK:] + wide_mm(amat[u], vhat[u]) for u in us}
            for u in us:
                cs = slice(u[1] * LANES, (u[1] + 1) * LANES)
                rows = pl.ds(r0[ci], CHUNK)
                if final:
                    ob = o[u] + oprev_ref[rows, cs]
                    ms = head_sums(ob * ob) * (1.0 / HEAD_DIM)
                    zz = z_ref[rows, cs].astype(F32)
                    out = ob * lax.rsqrt(ms + EPS) * gn_ref[:, cs] * (zz * jax.nn.sigmoid(zz))
                    o_ref[rows, cs] = out.astype(o_ref.dtype)
                else:
                    o_ref[rows, cs] = o[u]
        return carry

    lax.fori_loop(0, n_chunks // GDN_GROUP, group, 0)


def _gdn_pass(gd3, abr, conv_w, alog_row, dt_row, rev, final_args=None):
    bsz, seq, width3 = gd3.shape
    nblk = seq // GDN_TB
    hpb = GDN_TB // GDN_HALO
    blk = (lambda i: nblk - 1 - i) if rev else (lambda i: i)
    main = lambda w: pl.BlockSpec((None, GDN_TB, w), lambda b, i: (b, blk(i), 0))
    full = lambda shape: pl.BlockSpec(shape, lambda b, i: (0,) * len(shape))
    in_specs = [main(width3),
                pl.BlockSpec((None, GDN_HALO, width3), lambda b, i: (b, jnp.maximum(blk(i) * hpb - 1, 0), 0)),
                pl.BlockSpec((None, GDN_HALO, width3),
                             lambda b, i: (b, jnp.minimum((blk(i) + 1) * hpb, seq // GDN_HALO - 1), 0)),
                pl.BlockSpec((None, GDN_TB // CHUNK, 8, LANES), lambda b, i: (b, blk(i), 0, 0)),
                full((CONV_K, width3)), full((8, LANES)), full((8, LANES))]
    args = [gd3, gd3, gd3, abr, conv_w, alog_row, dt_row]
    final = final_args is not None
    if final:
        z3, o_prev, gn = final_args
        in_specs += [main(GDN_WIDTH), main(GDN_WIDTH), full((1, GDN_WIDTH))]
        args += [z3, o_prev, gn]
    return pl.pallas_call(
        functools.partial(_gdn_body, rev=rev, final=final),
        grid=(bsz, nblk),
        in_specs=in_specs,
        out_specs=main(GDN_WIDTH),
        out_shape=jax.ShapeDtypeStruct((bsz, seq, GDN_WIDTH), BF16 if final else F32),
        scratch_shapes=[pltpu.VMEM((GDN_TB + GDN_HALO, width3), F32),
                        pltpu.VMEM((GDN_HEADS // 2, LANES, LANES), F32)],
        compiler_params=_params("parallel", "arbitrary"),
        name="gdn_bwd" if rev else "gdn_fwd",
    )(*args)


def _pair_rows(x, bsz, seq):
    x = x.reshape(bsz, seq // CHUNK, CHUNK, GDN_HEADS // 2, 2)
    return x.transpose(0, 1, 3, 4, 2).reshape(bsz, seq // CHUNK, GDN_HEADS // 2, LANES)


def _gdn(gd, z, ab, conv_w, a_log, dt_bias, gdn_norm, bsz, seq):
    gd3 = gd.reshape(bsz, seq, 3 * GDN_WIDTH)
    z3 = z.reshape(bsz, seq, GDN_WIDTH)
    gn = jnp.tile(gdn_norm, GDN_HEADS).reshape(1, GDN_WIDTH)
    out = None
    for d in range(2):
        beta = _pair_rows(ab[:, d * GDN_HEADS:(d + 1) * GDN_HEADS], bsz, seq)
        araw = _pair_rows(ab[:, (2 + d) * GDN_HEADS:(3 + d) * GDN_HEADS], bsz, seq)
        abr = jnp.concatenate([beta, araw], axis=2)
        per_head = lambda v: jnp.concatenate(
            [jnp.zeros((GDN_HEADS // 2, LANES), F32),
             jnp.repeat(v[d], CHUNK).reshape(GDN_HEADS // 2, LANES)], axis=0)
        final_args = None if d == 0 else (z3, out, gn)
        out = _gdn_pass(gd3, abr, conv_w, per_head(a_log), per_head(dt_bias), rev=(d == 1), final_args=final_args)
    return out.reshape(bsz * seq, GDN_WIDTH)


def _gdn_scan_jax(q, k, v, g, beta):
    B, T, H, dk = q.shape
    C = CHUNK
    N = T // C
    chunks = lambda t: t.reshape(B, N, C, H, -1).transpose(1, 0, 3, 2, 4)
    qc, kc, vc = chunks(q), chunks(k), chunks(v)
    gc = jnp.cumsum(g.reshape(B, N, C, H).transpose(1, 0, 3, 2), axis=-1)
    bc = beta.reshape(B, N, C, H).transpose(1, 0, 3, 2)[..., None]
    lower = jnp.tril(jnp.ones((C, C), bool))
    strict = jnp.tril(jnp.ones((C, C), bool), -1)
    decay = jnp.exp(jnp.where(lower, gc[..., :, None] - gc[..., None, :], -jnp.inf))
    kbeta = kc * bc
    lmat = jnp.where(strict, jnp.einsum('nbhid,nbhjd->nbhij', kbeta, kc) * decay, 0.0)
    solve = lambda rhs: lax.linalg.triangular_solve(lmat, rhs, left_side=True, lower=True, unit_diagonal=True)
    u = solve(vc * bc)
    w = solve(kbeta * jnp.exp(gc)[..., None])
    a_qk = jnp.einsum('nbhid,nbhjd->nbhij', qc, kc) * decay

    def step(state, inp):
        q_i, k_i, u_i, w_i, g_i, a_i = inp
        v_new = u_i - jnp.einsum('bhck,bhkv->bhcv', w_i, state)
        o_i = (jnp.einsum('bhck,bhkv->bhcv', q_i * jnp.exp(g_i)[..., None], state)
               + jnp.einsum('bhij,bhjv->bhiv', a_i, v_new))
        g_last = g_i[..., -1:]
        state = (state * jnp.exp(g_last)[..., None]
                 + jnp.einsum('bhck,bhcv->bhkv', k_i * jnp.exp(g_last - g_i)[..., None], v_new))
        return state, o_i

    s0 = jnp.zeros((B, H, dk, dk), F32)
    _, o = lax.scan(step, s0, (qc, kc, u, w, gc, a_qk))
    return o.transpose(1, 0, 3, 2, 4).reshape(B, T, H, dk)


def _gdn_jax(gd, z, ab, conv_w, a_log, dt_bias, gdn_norm, bsz, seq):
    c = lax.conv_general_dilated(
        gd.astype(F32).reshape(bsz, seq, 3 * GDN_WIDTH), conv_w[:, None, :], window_strides=(1,),
        padding=[(CONV_K // 2, CONV_K // 2)], dimension_numbers=('NWC', 'WIO', 'NWC'),
        feature_group_count=3 * GDN_WIDTH)
    c = jax.nn.silu(c)
    l2 = lambda t: t * lax.rsqrt(jnp.sum(t * t, axis=-1, keepdims=True) + EPS)
    qb, kb, vb = [t.reshape(bsz, seq, GDN_HEADS, HEAD_DIM) for t in jnp.split(c, 3, axis=-1)]
    qb = l2(qb) * (HEAD_DIM ** -0.5)
    kb = l2(kb)
    ab4 = ab[:, :4 * GDN_HEADS].reshape(bsz, seq, 4, GDN_HEADS)
    beta = jax.nn.sigmoid(ab4[:, :, 0:2])
    g = -jnp.exp(a_log) * jax.nn.softplus(ab4[:, :, 2:4] + dt_bias)
    fwd = _gdn_scan_jax(qb, kb, vb, g[:, :, 0], beta[:, :, 0])
    flip = lambda t: t[:, ::-1]
    bwd = flip(_gdn_scan_jax(flip(qb), flip(kb), flip(vb), flip(g[:, :, 1]), flip(beta[:, :, 1])))
    ob = fwd + bwd
    ob = (ob * lax.rsqrt(jnp.mean(ob * ob, axis=-1, keepdims=True) + EPS) * gdn_norm
          * jax.nn.silu(z.astype(F32).reshape(bsz, seq, GDN_HEADS, HEAD_DIM)))
    return ob.reshape(bsz * seq, GDN_WIDTH).astype(BF16)


def _encoder(x, mem, norm_mix, w_in, conv_w, a_log, dt_bias, gdn_norm, w_out, norm_mem_q, norm_mem_kv,
             w_mq, w_mkv, w_mo, norm_ffn, w_router_g, w_router_e, w_gate, w_up, w_down, norm_final):
    bsz, seq, _ = x.shape
    x2d = x.reshape(bsz * seq, D_MODEL)
    qkv, gd, z, ab = _inproj(x2d, norm_mix, w_in)
    slopes = jnp.exp2(-8.0 * jnp.arange(1, SWA_HEADS + 1, dtype=F32) / SWA_HEADS)
    branches = [_attn_branch(qkv, slopes, d, bsz, seq) for d in DILATIONS]
    ob = _gdn(gd, z, ab, conv_w, a_log, dt_bias, gdn_norm, bsz, seq)
    mem_k, mem_v = _memkv(mem.reshape(-1, D_MODEL), norm_mem_kv, w_mkv)
    mem_k = mem_k.reshape(bsz, -1, MEM_HEADS * MEM_HEAD_DIM)
    mem_v = mem_v.reshape(bsz, -1, MEM_HEADS * MEM_HEAD_DIM)
    x2, h2, route, counts = _trunk(x2d, branches, ob, mem_k, mem_v, w_out, norm_mem_q, w_mq, w_mo,
                                   norm_ffn, w_router_g, w_router_e, bsz, seq)
    y = _moe(x2, h2, route, counts, w_gate, w_up, w_down, norm_final)
    return y.reshape(bsz, seq, D_MODEL)


def kernel(x_prompt, x_sample, mem_prompt, mem_sample, norm_mix, w_in, conv_w, a_log, dt_bias, gdn_norm,
           w_out, norm_mem_q, norm_mem_kv, w_mq, w_mkv, w_mo, norm_ffn, w_router_g, w_router_e,
           w_gate, w_up, w_down, norm_final):
    p = dict(norm_mix=norm_mix[0], w_in=w_in[0], conv_w=conv_w[0], a_log=a_log[0], dt_bias=dt_bias[0],
             gdn_norm=gdn_norm[0], w_out=w_out[0], norm_mem_q=norm_mem_q[0], norm_mem_kv=norm_mem_kv[0],
             w_mq=w_mq[0], w_mkv=w_mkv[0], w_mo=w_mo[0], norm_ffn=norm_ffn[0], w_router_g=w_router_g[0],
             w_router_e=w_router_e[0], w_gate=w_gate[0].astype(BF16), w_up=w_up[0].astype(BF16),
             w_down=w_down[0].astype(BF16), norm_final=norm_final)
    return (_encoder(x_prompt, mem_prompt, **p), _encoder(x_sample, mem_sample, **p))
```

```python
import functools

import jax
import jax.numpy as jnp
import numpy as np
from jax import lax
from jax.experimental import pallas as pl
from jax.experimental.pallas import tpu as pltpu

F32 = jnp.float32
BF16 = jnp.bfloat16

D_MODEL = 1024
HEAD_DIM = 64
SWA_HEADS = 8
GDN_HEADS = 8
SWA_WIDTH = SWA_HEADS * HEAD_DIM
GDN_WIDTH = GDN_HEADS * HEAD_DIM
DILATIONS = (1, 4, 16)
ATT_W = 64
CONV_K = 5
CHUNK = 64
MEM_HEADS = 4
MEM_HEAD_DIM = 256
N_GROUPS = 4
EXPERTS_PER_GROUP = 8
N_EXPERTS = 32
TOP_K = 2
D_EXPERT = 512
ROUTE_BLOCK = 256
EPS = 1e-6

LANES = 128
VMEM_LIMIT = 56 * 1024 * 1024
NEG_BIG = -1e30


def _dot(a, b):
    return jnp.dot(a, b, preferred_element_type=F32)


def _dot_nt(a, b):
    return lax.dot_general(a, b, (((1,), (1,)), ((), ())), preferred_element_type=F32)


def _dot_tn(a, b):
    return lax.dot_general(a, b, (((0,), (0,)), ((), ())), preferred_element_type=F32)


def _split2(x):
    hi = x.astype(BF16)
    lo = (x - hi.astype(F32)).astype(BF16)
    return hi, lo


def _split3(x):
    hi = x.astype(BF16)
    r = x - hi.astype(F32)
    mid = r.astype(BF16)
    lo = (r - mid.astype(F32)).astype(BF16)
    return hi, mid, lo


def _rms(x, g):
    return x * lax.rsqrt(jnp.mean(x * x, axis=-1, keepdims=True) + EPS) * g


def _params(*sem):
    return pltpu.CompilerParams(dimension_semantics=sem, vmem_limit_bytes=VMEM_LIMIT)


def _inproj_body(x_ref, g_ref, wa_ref, wgd_ref, wz_ref, wab_hi_ref, wab_lo_ref,
                 qkv_ref, gd_ref, z_ref, ab_ref):
    h = _rms(x_ref[...], g_ref[...])
    h_hi, h_lo = _split2(h)
    qkv_ref[...] = _dot(h_hi, wa_ref[...]).astype(BF16)
    gd_ref[...] = _dot(h_hi, wgd_ref[...]).astype(BF16)
    z_ref[...] = _dot(h_hi, wz_ref[...]).astype(BF16)
    wab_hi = wab_hi_ref[...]
    ab = _dot(h_hi, wab_hi) + _dot(h_lo, wab_hi) + _dot(h_hi, wab_lo_ref[...])
    ab_ref[...] = ab


def _inproj(x2, norm_mix, w_in, tm=512):
    n = x2.shape[0]
    wa = w_in[:, :3 * SWA_WIDTH].astype(BF16)
    wgd = w_in[:, 3 * SWA_WIDTH:3 * SWA_WIDTH + 3 * GDN_WIDTH].astype(BF16)
    wz = w_in[:, 3 * SWA_WIDTH + 3 * GDN_WIDTH:3 * SWA_WIDTH + 4 * GDN_WIDTH].astype(BF16)
    wab = jnp.pad(w_in[:, 3 * SWA_WIDTH + 4 * GDN_WIDTH:], ((0, 0), (0, LANES - 4 * GDN_HEADS)))
    wab_hi, wab_lo = _split2(wab)
    full = lambda shape: pl.BlockSpec(shape, lambda i: (0, 0))
    rows = lambda w: pl.BlockSpec((tm, w), lambda i: (i, 0))
    return pl.pallas_call(
        _inproj_body,
        grid=(n // tm,),
        in_specs=[rows(D_MODEL), full((1, D_MODEL)), full(wa.shape), full(wgd.shape), full(wz.shape),
                  full(wab_hi.shape), full(wab_lo.shape)],
        out_specs=[rows(3 * SWA_WIDTH), rows(3 * GDN_WIDTH), rows(GDN_WIDTH), rows(LANES)],
        out_shape=[jax.ShapeDtypeStruct((n, 3 * SWA_WIDTH), BF16),
                   jax.ShapeDtypeStruct((n, 3 * GDN_WIDTH), BF16),
                   jax.ShapeDtypeStruct((n, GDN_WIDTH), BF16),
                   jax.ShapeDtypeStruct((n, LANES), F32)],
        compiler_params=_params("parallel"),
        name="inproj",
    )(x2, norm_mix.reshape(1, D_MODEL), wa, wgd, wz, wab_hi, wab_lo)


ATT_QB = 128
ATT_KB = ATT_QB + 2 * ATT_W
ATT_GROUP = 4


def _attn_body(slope_ref, q_ref, k_ref, v_ref, o_ref, lse_ref, bias_ref, *, dil, n_qb, seq_l):
    pair = pl.program_id(2)
    lb = pl.program_id(3)
    lane = lax.broadcasted_iota(jnp.int32, (1, LANES), 1)
    left = lane < HEAD_DIM

    row = lax.broadcasted_iota(jnp.int32, (ATT_QB, ATT_KB), 0)
    col = lax.broadcasted_iota(jnp.int32, (ATT_QB, ATT_KB), 1)
    for var in range(3):
        rel = jnp.abs(col - row - var * ATT_W)
        dist = (rel * dil).astype(F32)
        for h in range(2):
            slope = slope_ref[pair * 2 + h]
            bias_ref[var * 2 + h] = jnp.where(rel <= ATT_W, -slope * dist, NEG_BIG)

    group = min(ATT_GROUP, n_qb)

    def qgroup(gi, carry):
        rows, var, kb, vb, q = [], [], [], [], []
        for j in range(group):
            qi = gi * group + j
            n0 = (lb * n_qb + qi) * ATT_QB
            kstart = pl.multiple_of(jnp.clip(n0 - ATT_W, 0, seq_l - ATT_KB), ATT_W)
            var.append((n0 - kstart) // ATT_W)
            rows.append(pl.ds(pl.multiple_of(qi * ATT_QB, ATT_QB), ATT_QB))
            q.append(q_ref[rows[j], :] * jnp.asarray(HEAD_DIM ** -0.5, BF16))
            kb.append(k_ref[pl.ds(kstart, ATT_KB), :])
            vb.append(v_ref[pl.ds(kstart, ATT_KB), :])
        units = [(j, h) for j in range(group) for h in range(2)]
        mine = [left, jnp.logical_not(left)]
        s = {(j, h): _dot_nt(jnp.where(mine[h], q[j], jnp.zeros_like(q[j])), kb[j]) + bias_ref[var[j] * 2 + h]
             for (j, h) in units}
        m = {u: jnp.max(s[u], axis=-1, keepdims=True) for u in units}
        p = {u: jnp.exp(s[u] - m[u]).astype(BF16) for u in units}
        acc = {(j, h): _dot(p[(j, h)], jnp.where(mine[h], vb[j], jnp.ones_like(vb[j]))) for (j, h) in units}
        for j in range(group):
            num = jnp.where(left, acc[(j, 0)], acc[(j, 1)])
            den = pltpu.roll(jnp.where(left, acc[(j, 1)], acc[(j, 0)]), HEAD_DIM, 1)
            mx = jnp.where(left, m[(j, 0)], m[(j, 1)])
            o_ref[rows[j], :] = (num / den).astype(BF16)
            lse_ref[rows[j], :] = mx + jnp.log(den)
        return carry

    lax.fori_loop(0, n_qb // group, qgroup, 0)


def _attn_branch(qkv, slopes, dil, bsz, seq):
    seq_l = seq // dil
    lblk = min(2048, seq_l)
    n_qb = lblk // ATT_QB
    n_pairs = SWA_HEADS // 2
    cols = 3 * SWA_WIDTH // LANES
    view = qkv.reshape(bsz, seq_l, dil * 3 * SWA_WIDTH)
    qspec = pl.BlockSpec((None, lblk, LANES), lambda b, r, p, l: (b, l, r * cols + p))
    kspec = pl.BlockSpec((None, seq_l, LANES), lambda b, r, p, l: (b, 0, r * cols + n_pairs + p))
    vspec = pl.BlockSpec((None, seq_l, LANES), lambda b, r, p, l: (b, 0, r * cols + 2 * n_pairs + p))
    ospec = pl.BlockSpec((None, lblk, LANES), lambda b, r, p, l: (b, l, r * n_pairs + p))
    o, lse = pl.pallas_call(
        functools.partial(_attn_body, dil=dil, n_qb=n_qb, seq_l=seq_l),
        grid=(bsz, dil, n_pairs, seq_l // lblk),
        in_specs=[pl.BlockSpec(memory_space=pltpu.SMEM), qspec, kspec, vspec],
        out_specs=[ospec, ospec],
        out_shape=[jax.ShapeDtypeStruct((bsz, seq_l, dil * SWA_WIDTH), BF16),
                   jax.ShapeDtypeStruct((bsz, seq_l, dil * SWA_WIDTH), F32)],
        scratch_shapes=[pltpu.VMEM((6, ATT_QB, ATT_KB), F32)],
        compiler_params=_params("parallel", "parallel", "parallel", "arbitrary"),
        name=f"dilated_attn_d{dil}",
    )(slopes, view, view, view)
    return o.reshape(bsz * seq, SWA_WIDTH), lse.reshape(bsz * seq, SWA_WIDTH)


def _memkv_body(m_ref, g_ref, w_ref, k_ref, v_ref):
    h = _rms(m_ref[...], g_ref[...]).astype(BF16)
    kv = _dot(h, w_ref[...])
    width = MEM_HEADS * MEM_HEAD_DIM
    k_ref[...] = (kv[:, :width] * (MEM_HEAD_DIM ** -0.5)).astype(BF16)
    v_ref[...] = kv[:, width:].astype(BF16)


def _memkv(mem2, norm_kv, w_mkv, tm=256):
    n = mem2.shape[0]
    width = MEM_HEADS * MEM_HEAD_DIM
    return pl.pallas_call(
        _memkv_body,
        grid=(n // tm,),
        in_specs=[pl.BlockSpec((tm, D_MODEL), lambda i: (i, 0)),
                  pl.BlockSpec((1, D_MODEL), lambda i: (0, 0)),
                  pl.BlockSpec((D_MODEL, 2 * width), lambda i: (0, 0))],
        out_specs=[pl.BlockSpec((tm, width), lambda i: (i, 0))] * 2,
        out_shape=[jax.ShapeDtypeStruct((n, width), BF16)] * 2,
        compiler_params=_params("parallel"),
        name="mem_kv",
    )(mem2, norm_kv.reshape(1, D_MODEL), w_mkv.astype(BF16))


ROUTE_COLS = N_GROUPS + N_EXPERTS


def _trunk_body(x_ref, o1_ref, o2_ref, o3_ref, l1_ref, l2_ref, l3_ref, ob_ref,
                wo_a_ref, wo_b_ref, gq_ref, wq_ref, k_ref, v_ref, wmo_ref,
                gf_ref, wr_hi_ref, wr_lo_ref,
                x2_ref, h2_ref, route_ref, count_ref, run_ref, *, tm):
    first = jnp.logical_and(pl.program_id(0) == 0, pl.program_id(1) == 0)

    @pl.when(first)
    def _():
        run_ref[...] = jnp.zeros_like(run_ref)

    l1, l2, l3 = l1_ref[...], l2_ref[...], l3_ref[...]
    mx = jnp.maximum(jnp.maximum(l1, l2), l3)
    e1, e2, e3 = jnp.exp(l1 - mx), jnp.exp(l2 - mx), jnp.exp(l3 - mx)
    ya = (e1 * o1_ref[...].astype(F32) + e2 * o2_ref[...].astype(F32) + e3 * o3_ref[...].astype(F32))
    ya = (ya / (e1 + e2 + e3)).astype(BF16)
    x1 = x_ref[...] + _dot(ya, wo_a_ref[...]) + _dot(ob_ref[...], wo_b_ref[...])

    q = _dot(_rms(x1, gq_ref[...]).astype(BF16), wq_ref[...]).astype(BF16)
    cols = [slice(h * MEM_HEAD_DIM, (h + 1) * MEM_HEAD_DIM) for h in range(MEM_HEADS)]
    s = [_dot_nt(q[:, cs], k_ref[:, cs]) for cs in cols]
    p = [jnp.exp(sh - jnp.max(sh, axis=-1, keepdims=True)) for sh in s]
    den = [jnp.sum(ph, axis=-1, keepdims=True) for ph in p]
    pv = [_dot(ph.astype(BF16), v_ref[:, cs]) for ph, cs in zip(p, cols)]
    heads = [(a / d).astype(BF16) for a, d in zip(pv, den)]
    x2 = x1 + _dot(jnp.concatenate(heads, axis=-1), wmo_ref[...])
    x2_ref[...] = x2

    h2 = _rms(x2, gf_ref[...])
    h2_ref[...] = h2
    h_hi, h_lo = _split2(h2)
    wr_hi = wr_hi_ref[...]
    logits = _dot(h_hi, wr_hi) + _dot(h_lo, wr_hi) + _dot(h_hi, wr_lo_ref[...])
    lane = lax.broadcasted_iota(jnp.int32, (tm, LANES), 1)
    big = jnp.int32(LANES)
    is_g = lane < N_GROUPS
    lg = jnp.where(is_g, logits, NEG_BIG)
    mg = jnp.max(lg, axis=-1, keepdims=True)
    g_idx = jnp.min(jnp.where(jnp.logical_and(is_g, lg == mg), lane, big), axis=-1, keepdims=True)
    g_w = 1.0 / jnp.sum(jnp.exp(lg - mg), axis=-1, keepdims=True)
    lo_lane = N_GROUPS + g_idx * EXPERTS_PER_GROUP
    in_grp = jnp.logical_and(lane >= lo_lane, lane < lo_lane + EXPERTS_PER_GROUP)
    le = jnp.where(in_grp, logits, NEG_BIG)
    m1 = jnp.max(le, axis=-1, keepdims=True)
    i1 = jnp.min(jnp.where(jnp.logical_and(in_grp, le == m1), lane, big), axis=-1, keepdims=True)
    le2 = jnp.where(lane == i1, NEG_BIG, le)
    m2 = jnp.max(le2, axis=-1, keepdims=True)
    i2 = jnp.min(jnp.where(jnp.logical_and(in_grp, le2 == m2), lane, big), axis=-1, keepdims=True)
    r2 = jnp.exp(m2 - m1)
    gate1 = g_w / (1.0 + r2)
    gate2 = g_w * r2 / (1.0 + r2)

    onehot = jnp.logical_or(lane == i1, lane == i2)
    oh = jnp.where(onehot, 1.0, 0.0)
    r_i = lax.broadcasted_iota(jnp.int32, (tm, tm), 0)
    c_i = lax.broadcasted_iota(jnp.int32, (tm, tm), 1)
    tri = jnp.where(c_i < r_i, 1.0, 0.0).astype(BF16)
    before = _dot(tri, oh.astype(BF16)) + run_ref[...]
    rank1 = jnp.sum(jnp.where(lane == i1, before, 0.0), axis=-1, keepdims=True)
    rank2 = jnp.sum(jnp.where(lane == i2, before, 0.0), axis=-1, keepdims=True)
    run_ref[...] = run_ref[...] + jnp.sum(oh, axis=0, keepdims=True)
    count_ref[...] = run_ref[...]

    e1f = (i1 - N_GROUPS).astype(F32)
    e2f = (i2 - N_GROUPS).astype(F32)
    route = jnp.where(lane == 0, e1f, 0.0)
    for j, val in enumerate((e2f, gate1, gate2, rank1, rank2), start=1):
        route = jnp.where(lane == j, val, route)
    route_ref[...] = route


def _trunk(x2d, branches, ob, mem_k, mem_v, w_out, norm_mem_q, w_mq, w_mo, norm_ffn,
           w_router_g, w_router_e, bsz, seq, tm=256):
    n = bsz * seq
    nblk = seq // tm
    (o1, l1), (o2, l2), (o3, l3) = branches
    wo = w_out.astype(BF16)
    wr = jnp.concatenate([w_router_g, jnp.moveaxis(w_router_e, 0, 1).reshape(D_MODEL, N_EXPERTS)], axis=1)
    wr = jnp.pad(wr, ((0, 0), (0, LANES - ROUTE_COLS)))
    wr_hi, wr_lo = _split2(wr)
    rows = lambda w: pl.BlockSpec((tm, w), lambda b, i: (b * nblk + i, 0))
    full = lambda shape: pl.BlockSpec(shape, lambda b, i: (0, 0))
    memspec = pl.BlockSpec((None, mem_k.shape[1], mem_k.shape[2]), lambda b, i: (b, 0, 0))
    return pl.pallas_call(
        functools.partial(_trunk_body, tm=tm),
        grid=(bsz, nblk),
        in_specs=[rows(D_MODEL)] + [rows(SWA_WIDTH)] * 7
                 + [full((SWA_WIDTH, D_MODEL)), full((GDN_WIDTH, D_MODEL)), full((1, D_MODEL)),
                    full((D_MODEL, D_MODEL)), memspec, memspec, full((D_MODEL, D_MODEL)),
                    full((1, D_MODEL)), full((D_MODEL, LANES)), full((D_MODEL, LANES))],
        out_specs=[rows(D_MODEL), rows(D_MODEL), rows(LANES), full((1, LANES))],
        out_shape=[jax.ShapeDtypeStruct((n, D_MODEL), F32), jax.ShapeDtypeStruct((n, D_MODEL), F32),
                   jax.ShapeDtypeStruct((n, LANES), F32), jax.ShapeDtypeStruct((1, LANES), F32)],
        scratch_shapes=[pltpu.VMEM((1, LANES), F32)],
        compiler_params=_params("arbitrary", "arbitrary"),
        name="trunk",
    )(x2d, o1, o2, o3, l1, l2, l3, ob, wo[:SWA_WIDTH], wo[SWA_WIDTH:], norm_mem_q.reshape(1, D_MODEL),
      w_mq.astype(BF16), mem_k, mem_v, w_mo.astype(BF16), norm_ffn.reshape(1, D_MODEL), wr_hi, wr_lo)


DMA_UNROLL = 8


def _dispatch_body(dest_ref, h_ref, xs_in_ref, xs_ref, sem, *, tm):
    del xs_in_ref

    def issue(t, c):
        for k in range(TOP_K):
            dst = xs_ref.at[pl.ds(dest_ref[0, 0, t * TOP_K + k], 1)]
            pltpu.make_async_copy(h_ref.at[pl.ds(t, 1)], dst, sem).start()
        return c

    lax.fori_loop(0, tm, issue, 0, unroll=DMA_UNROLL)
    for _ in range(TOP_K):
        pltpu.make_async_copy(h_ref, xs_ref.at[pl.ds(0, tm)], sem).wait()


def _dispatch(h2, dest, n_slots, tm=512):
    n = h2.shape[0]
    dest3 = dest.reshape(n // tm, 1, tm * TOP_K)
    xs0 = jnp.zeros((n_slots, D_MODEL), F32)
    return pl.pallas_call(
        functools.partial(_dispatch_body, tm=tm),
        grid=(n // tm,),
        in_specs=[pl.BlockSpec((1, 1, tm * TOP_K), lambda i: (i, 0, 0), memory_space=pltpu.SMEM),
                  pl.BlockSpec((tm, D_MODEL), lambda i: (i, 0)),
                  pl.BlockSpec(memory_space=pl.ANY)],
        out_specs=pl.BlockSpec(memory_space=pl.ANY),
        out_shape=jax.ShapeDtypeStruct((n_slots, D_MODEL), F32),
        scratch_shapes=[pltpu.SemaphoreType.DMA(())],
        input_output_aliases={2: 0},
        compiler_params=_params("arbitrary"),
        name="moe_dispatch",
    )(dest3, h2, xs0)


def _expert_body(be_ref, nused_ref, x_ref, wg_ref, wu_ref, wd_ref, y_ref):
    i = pl.program_id(0)

    @pl.when(i < nused_ref[0])
    def _():
        x = x_ref[...].astype(BF16)
        a = _dot(x, wg_ref[...])
        b = _dot(x, wu_ref[...])
        hid = (a * jax.nn.sigmoid(a) * b).astype(BF16)
        y_ref[...] = _dot(hid, wd_ref[...])

    @pl.when(i >= nused_ref[0])
    def _():
        y_ref[...] = jnp.zeros_like(y_ref)


def _experts(xs, block_expert, n_used, w_gate, w_up, w_down):
    n_slots = xs.shape[0]
    n_blocks = n_slots // ROUTE_BLOCK
    grid_spec = pltpu.PrefetchScalarGridSpec(
        num_scalar_prefetch=2,
        grid=(n_blocks,),
        in_specs=[pl.BlockSpec((ROUTE_BLOCK, D_MODEL), lambda i, be, nu: (i, 0)),
                  pl.BlockSpec((None, D_MODEL, D_EXPERT), lambda i, be, nu: (be[i], 0, 0)),
                  pl.BlockSpec((None, D_MODEL, D_EXPERT), lambda i, be, nu: (be[i], 0, 0)),
                  pl.BlockSpec((None, D_EXPERT, D_MODEL), lambda i, be, nu: (be[i], 0, 0))],
        out_specs=pl.BlockSpec((ROUTE_BLOCK, D_MODEL), lambda i, be, nu: (i, 0)),
    )
    return pl.pallas_call(
        _expert_body,
        grid_spec=grid_spec,
        out_shape=jax.ShapeDtypeStruct((n_slots, D_MODEL), F32),
        compiler_params=_params("arbitrary"),
        name="moe_experts",
    )(block_expert, n_used, xs, w_gate, w_up, w_down)


def _combine_body(dest_ref, x_ref, route_ref, g_ref, yb_ref, o_ref, buf_ref, sem, *, tm):
    def issue(t, c):
        for k in range(TOP_K):
            src = yb_ref.at[pl.ds(dest_ref[0, 0, t * TOP_K + k], 1)]
            pltpu.make_async_copy(src, buf_ref.at[k, pl.ds(t, 1)], sem).start()
        return c

    lax.fori_loop(0, tm, issue, 0, unroll=DMA_UNROLL)
    for k in range(TOP_K):
        pltpu.make_async_copy(yb_ref.at[pl.ds(0, tm)], buf_ref.at[k], sem).wait()
    route = route_ref[...]
    y = route[:, 2:3] * buf_ref[0] + route[:, 3:4] * buf_ref[1]
    o_ref[...] = _rms(x_ref[...] + y, g_ref[...])


def _combine(x2, route, dest, yb, norm_final, tm=256):
    n = x2.shape[0]
    dest3 = dest.reshape(n // tm, 1, tm * TOP_K)
    return pl.pallas_call(
        functools.partial(_combine_body, tm=tm),
        grid=(n // tm,),
        in_specs=[pl.BlockSpec((1, 1, tm * TOP_K), lambda i: (i, 0, 0), memory_space=pltpu.SMEM),
                  pl.BlockSpec((tm, D_MODEL), lambda i: (i, 0)),
                  pl.BlockSpec((tm, LANES), lambda i: (i, 0)),
                  pl.BlockSpec((1, D_MODEL), lambda i: (0, 0)),
                  pl.BlockSpec(memory_space=pl.ANY)],
        out_specs=pl.BlockSpec((tm, D_MODEL), lambda i: (i, 0)),
        out_shape=jax.ShapeDtypeStruct((n, D_MODEL), F32),
        scratch_shapes=[pltpu.VMEM((TOP_K, tm, D_MODEL), F32), pltpu.SemaphoreType.DMA(())],
        compiler_params=_params("arbitrary"),
        name="moe_combine",
    )(dest3, x2, route, norm_final.reshape(1, D_MODEL), yb)


def _moe(x2, h2, route, counts, w_gate, w_up, w_down, norm_final):
    n = x2.shape[0]
    m_slots = n * TOP_K
    n_blocks = -(-(m_slots + N_EXPERTS * (ROUTE_BLOCK - 1)) // ROUTE_BLOCK)
    cnt = counts[0, N_GROUPS:N_GROUPS + N_EXPERTS].astype(jnp.int32)
    padded = (cnt + ROUTE_BLOCK - 1) // ROUTE_BLOCK * ROUTE_BLOCK
    pad_end = jnp.cumsum(padded)
    seg_start = pad_end - padded
    eid = route[:, 0:TOP_K].astype(jnp.int32)
    rank = route[:, 4:4 + TOP_K].astype(jnp.int32)
    dest = (seg_start[eid] + rank).reshape(-1)
    block_expert = jnp.minimum(
        jnp.searchsorted(pad_end, jnp.arange(n_blocks, dtype=jnp.int32) * ROUTE_BLOCK, side='right'),
        N_EXPERTS - 1).astype(jnp.int32)
    n_used = (pad_end[-1:] // ROUTE_BLOCK).astype(jnp.int32)
    xs = _dispatch(h2, dest, n_blocks * ROUTE_BLOCK)
    yb = _experts(xs, block_expert, n_used, w_gate, w_up, w_down)
    return _combine(x2, route, dest, yb, norm_final)


GDN_TB = 512
GDN_HALO = 16
GDN_GROUP = 4


def _gdn_body(gd_ref, prev_ref, next_ref, abr_ref, cw_ref, alog_ref, dt_ref, *rest, rev, final):
    if final:
        z_ref, oprev_ref, gn_ref, o_ref, xpad_ref, state_ref = rest
    else:
        o_ref, xpad_ref, state_ref = rest
    i = pl.program_id(1)
    nblk = pl.num_programs(1)
    n_chunks = GDN_TB // CHUNK
    n_pairs = GDN_HEADS // 2
    width3 = 3 * GDN_WIDTH

    @pl.when(i == 0)
    def _():
        state_ref[...] = jnp.zeros_like(state_ref)

    blk = (nblk - 1 - i) if rev else i
    pad = GDN_HALO // 2
    prev = prev_ref[pad:, :].astype(F32)
    nxt = next_ref[:pad, :].astype(F32)
    xpad_ref[:pad, :] = jnp.where(blk == 0, 0.0, prev)
    xpad_ref[pad:pad + GDN_TB, :] = gd_ref[...].astype(F32)
    xpad_ref[pad + GDN_TB:, :] = jnp.where(blk == nblk - 1, 0.0, nxt)

    lane = lax.broadcasted_iota(jnp.int32, (1, LANES), 1)
    left = lane < CHUNK
    tok = lane % CHUNK
    row = lax.broadcasted_iota(jnp.int32, (CHUNK, 1), 0)
    causal = (tok >= row) if rev else (tok <= row)
    strict = (tok > row) if rev else (tok < row)
    eye = jnp.where(tok == row, 1.0, 0.0)
    r2 = lax.broadcasted_iota(jnp.int32, (LANES, 1), 0)
    same_head = (r2 // CHUNK) == (lane // CHUNK)
    bd_ones = jnp.where(same_head, 1.0, 0.0).astype(BF16)
    tri_in = (r2 % CHUNK >= tok) if rev else (r2 % CHUNK <= tok)
    tri_bd = jnp.where(jnp.logical_and(same_head, tri_in), 1.0, 0.0).astype(BF16)
    row8 = lax.broadcasted_iota(jnp.int32, (8, 1), 0)

    def blockdiag(x):
        xb = x.astype(BF16)
        zero = jnp.zeros_like(xb)
        return jnp.concatenate([jnp.where(left, xb, zero), jnp.where(left, zero, xb)], axis=0)

    def wide_mm(x, y):
        return _dot(x.astype(BF16), blockdiag(y))

    def head_sums(x):
        hi, lo = _split2(x)
        return _dot(hi, bd_ones) + _dot(lo, bd_ones)

    def group(gi, carry):
        g0 = (n_chunks // GDN_GROUP - 1 - gi) if rev else gi
        chunks = [g0 * GDN_GROUP + ((GDN_GROUP - 1 - j) if rev else j) for j in range(GDN_GROUP)]
        units = [(ci, p) for ci in range(GDN_GROUP) for p in range(n_pairs)]
        r0 = [pl.multiple_of(c * CHUNK, CHUNK) for c in chunks]

        gates, gc_rows, gl_rows = [], [], []
        for c in chunks:
            graw = abr_ref[c]
            xg = graw + dt_ref[...]
            softplus = jnp.maximum(xg, 0.0) + jnp.log(1.0 + jnp.exp(-jnp.abs(xg)))
            g = jnp.where(row8 < n_pairs, jax.nn.sigmoid(graw), -jnp.exp(alog_ref[...]) * softplus)
            g3 = _split3(g)
            gates.append(g)
            gc_rows.append(sum(_dot(t, tri_bd) for t in g3))
            gl_rows.append(sum(_dot(t, bd_ones) for t in g3))

        def conv(ci, p, col0):
            cols = slice(col0 + p * LANES, col0 + (p + 1) * LANES)
            xa = xpad_ref[pl.ds(r0[ci], CHUNK + GDN_HALO), cols]
            acc = None
            for j in range(CONV_K):
                term = xa[pad - CONV_K // 2 + j:pad - CONV_K // 2 + j + CHUNK] * cw_ref[j:j + 1, cols]
                acc = term if acc is None else acc + term
            return acc * jax.nn.sigmoid(acc)

        qkv = {u: [conv(*u, col0) for col0 in (0, GDN_WIDTH, 2 * GDN_WIDTH)] for u in units}
        ss = {u: head_sums(jnp.concatenate([qkv[u][0] * qkv[u][0], qkv[u][1] * qkv[u][1]], axis=0))
              for u in units}
        prep = {}
        for (ci, p) in units:
            beta_r = gates[ci][p:p + 1]
            g_r = gates[ci][n_pairs + p:n_pairs + p + 1]
            lhs = jnp.concatenate([jnp.where(causal, g_r, 0.0), eye * beta_r], axis=0)
            prep[(ci, p)] = sum(_dot(t, bd_ones) for t in _split3(lhs))

        kq, kn_bd, e_gc, k_upd, decay_b = {}, {}, {}, {}, {}
        for u in units:
            ci, p = u
            qn = qkv[u][0] * lax.rsqrt(ss[u][:CHUNK] + EPS) * (HEAD_DIM ** -0.5)
            kn = qkv[u][1] * lax.rsqrt(ss[u][CHUNK:] + EPS)
            kq[u] = jnp.concatenate([kn, qn], axis=0).astype(BF16)
            kn_bd[u] = blockdiag(kn)
            beta_r = gates[ci][p:p + 1]
            gc_r = gc_rows[ci][n_pairs + p:n_pairs + p + 1]
            gl_r = gl_rows[ci][n_pairs + p:n_pairs + p + 1]
            gc_b, beta_b = prep[u][:CHUNK], prep[u][CHUNK:]
            decay_b[u] = jnp.exp(jnp.where(causal, gc_b - gc_r, NEG_BIG)) * beta_r
            e_gc[u] = jnp.exp(gc_b)
            k_upd[u] = (kn * (jnp.exp(gl_r - gc_b) * beta_b)).astype(BF16)

        kk_qk = {u: _dot_nt(kq[u], kn_bd[u]) for u in units}
        lpow = {u: jnp.where(strict, kk_qk[u][:CHUNK] * decay_b[u], 0.0) for u in units}
        amat = {u: kk_qk[u][CHUNK:] * decay_b[u] for u in units}
        tinv = {u: eye - lpow[u] for u in units}
        for _ in range(5):
            lpow = {u: wide_mm(lpow[u], lpow[u]) for u in units}
            tinv = {u: tinv[u] + wide_mm(tinv[u], lpow[u]) for u in units}

        for ci in range(GDN_GROUP):
            us = [(ci, p) for p in range(n_pairs)]
            state = {u: state_ref[u[1]] for u in us}
            pq = {u: _dot(kq[u], state[u].astype(BF16)) for u in us}
            vhat = {u: wide_mm(tinv[u], qkv[u][2] - e_gc[u] * pq[u][:CHUNK]) for u in us}
            upd = {u: _dot_tn(k_upd[u], vhat[u].astype(BF16)) for u in us}
            for u in us:
                gl_r = gl_rows[ci][n_pairs + u[1]:n_pairs + u[1] + 1]
                state_ref[u[1]] = state[u] * jnp.exp(gl_r) + jnp.where(same_head, upd[u], 0.0)
            o = {u: e_gc[u] * pq[u][CHUNK:] + wide_mm(amat[u], vhat[u]) for u in us}
            for u in us:
                cs = slice(u[1] * LANES, (u[1] + 1) * LANES)
                rows = pl.ds(r0[ci], CHUNK)
                if final:
                    ob = o[u] + oprev_ref[rows, cs]
                    ms = head_sums(ob * ob) * (1.0 / HEAD_DIM)
                    zz = z_ref[rows, cs].astype(F32)
                    out = ob * lax.rsqrt(ms + EPS) * gn_ref[:, cs] * (zz * jax.nn.sigmoid(zz))
                    o_ref[rows, cs] = out.astype(o_ref.dtype)
                else:
                    o_ref[rows, cs] = o[u]
        return carry

    lax.fori_loop(0, n_chunks // GDN_GROUP, group, 0)


def _gdn_pass(gd3, abr, conv_w, alog_row, dt_row, rev, final_args=None):
    bsz, seq, width3 = gd3.shape
    nblk = seq // GDN_TB
    hpb = GDN_TB // GDN_HALO
    blk = (lambda i: nblk - 1 - i) if rev else (lambda i: i)
    main = lambda w: pl.BlockSpec((None, GDN_TB, w), lambda b, i: (b, blk(i), 0))
    full = lambda shape: pl.BlockSpec(shape, lambda b, i: (0,) * len(shape))
    in_specs = [main(width3),
                pl.BlockSpec((None, GDN_HALO, width3), lambda b, i: (b, jnp.maximum(blk(i) * hpb - 1, 0), 0)),
                pl.BlockSpec((None, GDN_HALO, width3),
                             lambda b, i: (b, jnp.minimum((blk(i) + 1) * hpb, seq // GDN_HALO - 1), 0)),
                pl.BlockSpec((None, GDN_TB // CHUNK, 8, LANES), lambda b, i: (b, blk(i), 0, 0)),
                full((CONV_K, width3)), full((8, LANES)), full((8, LANES))]
    args = [gd3, gd3, gd3, abr, conv_w, alog_row, dt_row]
    final = final_args is not None
    if final:
        z3, o_prev, gn = final_args
        in_specs += [main(GDN_WIDTH), main(GDN_WIDTH), full((1, GDN_WIDTH))]
        args += [z3, o_prev, gn]
    return pl.pallas_call(
        functools.partial(_gdn_body, rev=rev, final=final),
        grid=(bsz, nblk),
        in_specs=in_specs,
        out_specs=main(GDN_WIDTH),
        out_shape=jax.ShapeDtypeStruct((bsz, seq, GDN_WIDTH), BF16 if final else F32),
        scratch_shapes=[pltpu.VMEM((GDN_TB + GDN_HALO, width3), F32),
                        pltpu.VMEM((GDN_HEADS // 2, LANES, LANES), F32)],
        compiler_params=_params("parallel", "arbitrary"),
        name="gdn_bwd" if rev else "gdn_fwd",
    )(*args)


def _pair_rows(x, bsz, seq):
    x = x.reshape(bsz, seq // CHUNK, CHUNK, GDN_HEADS // 2, 2)
    return x.transpose(0, 1, 3, 4, 2).reshape(bsz, seq // CHUNK, GDN_HEADS // 2, LANES)


def _gdn(gd, z, ab, conv_w, a_log, dt_bias, gdn_norm, bsz, seq):
    gd3 = gd.reshape(bsz, seq, 3 * GDN_WIDTH)
    z3 = z.reshape(bsz, seq, GDN_WIDTH)
    gn = jnp.tile(gdn_norm, GDN_HEADS).reshape(1, GDN_WIDTH)
    out = None
    for d in range(2):
        beta = _pair_rows(ab[:, d * GDN_HEADS:(d + 1) * GDN_HEADS], bsz, seq)
        araw = _pair_rows(ab[:, (2 + d) * GDN_HEADS:(3 + d) * GDN_HEADS], bsz, seq)
        abr = jnp.concatenate([beta, araw], axis=2)
        per_head = lambda v: jnp.concatenate(
            [jnp.zeros((GDN_HEADS // 2, LANES), F32),
             jnp.repeat(v[d], CHUNK).reshape(GDN_HEADS // 2, LANES)], axis=0)
        final_args = None if d == 0 else (z3, out, gn)
        out = _gdn_pass(gd3, abr, conv_w, per_head(a_log), per_head(dt_bias), rev=(d == 1), final_args=final_args)
    return out.reshape(bsz * seq, GDN_WIDTH)


def _gdn_scan_jax(q, k, v, g, beta):
    B, T, H, dk = q.shape
    C = CHUNK
    N = T // C
    chunks = lambda t: t.reshape(B, N, C, H, -1).transpose(1, 0, 3, 2, 4)
    qc, kc, vc = chunks(q), chunks(k), chunks(v)
    gc = jnp.cumsum(g.reshape(B, N, C, H).transpose(1, 0, 3, 2), axis=-1)
    bc = beta.reshape(B, N, C, H).transpose(1, 0, 3, 2)[..., None]
    lower = jnp.tril(jnp.ones((C, C), bool))
    strict = jnp.tril(jnp.ones((C, C), bool), -1)
    decay = jnp.exp(jnp.where(lower, gc[..., :, None] - gc[..., None, :], -jnp.inf))
    kbeta = kc * bc
    lmat = jnp.where(strict, jnp.einsum('nbhid,nbhjd->nbhij', kbeta, kc) * decay, 0.0)
    solve = lambda rhs: lax.linalg.triangular_solve(lmat, rhs, left_side=True, lower=True, unit_diagonal=True)
    u = solve(vc * bc)
    w = solve(kbeta * jnp.exp(gc)[..., None])
    a_qk = jnp.einsum('nbhid,nbhjd->nbhij', qc, kc) * decay

    def step(state, inp):
        q_i, k_i, u_i, w_i, g_i, a_i = inp
        v_new = u_i - jnp.einsum('bhck,bhkv->bhcv', w_i, state)
        o_i = (jnp.einsum('bhck,bhkv->bhcv', q_i * jnp.exp(g_i)[..., None], state)
               + jnp.einsum('bhij,bhjv->bhiv', a_i, v_new))
        g_last = g_i[..., -1:]
        state = (state * jnp.exp(g_last)[..., None]
                 + jnp.einsum('bhck,bhcv->bhkv', k_i * jnp.exp(g_last - g_i)[..., None], v_new))
        return state, o_i

    s0 = jnp.zeros((B, H, dk, dk), F32)
    _, o = lax.scan(step, s0, (qc, kc, u, w, gc, a_qk))
    return o.transpose(1, 0, 3, 2, 4).reshape(B, T, H, dk)


def _gdn_jax(gd, z, ab, conv_w, a_log, dt_bias, gdn_norm, bsz, seq):
    c = lax.conv_general_dilated(
        gd.astype(F32).reshape(bsz, seq, 3 * GDN_WIDTH), conv_w[:, None, :], window_strides=(1,),
        padding=[(CONV_K // 2, CONV_K // 2)], dimension_numbers=('NWC', 'WIO', 'NWC'),
        feature_group_count=3 * GDN_WIDTH)
    c = jax.nn.silu(c)
    l2 = lambda t: t * lax.rsqrt(jnp.sum(t * t, axis=-1, keepdims=True) + EPS)
    qb, kb, vb = [t.reshape(bsz, seq, GDN_HEADS, HEAD_DIM) for t in jnp.split(c, 3, axis=-1)]
    qb = l2(qb) * (HEAD_DIM ** -0.5)
    kb = l2(kb)
    ab4 = ab[:, :4 * GDN_HEADS].reshape(bsz, seq, 4, GDN_HEADS)
    beta = jax.nn.sigmoid(ab4[:, :, 0:2])
    g = -jnp.exp(a_log) * jax.nn.softplus(ab4[:, :, 2:4] + dt_bias)
    fwd = _gdn_scan_jax(qb, kb, vb, g[:, :, 0], beta[:, :, 0])
    flip = lambda t: t[:, ::-1]
    bwd = flip(_gdn_scan_jax(flip(qb), flip(kb), flip(vb), flip(g[:, :, 1]), flip(beta[:, :, 1])))
    ob = fwd + bwd
    ob = (ob * lax.rsqrt(jnp.mean(ob * ob, axis=-1, keepdims=True) + EPS) * gdn_norm
          * jax.nn.silu(z.astype(F32).reshape(bsz, seq, GDN_HEADS, HEAD_DIM)))
    return ob.reshape(bsz * seq, GDN_WIDTH).astype(BF16)


def _encoder(x, mem, norm_mix, w_in, conv_w, a_log, dt_bias, gdn_norm, w_out, norm_mem_q, norm_mem_kv,
             w_mq, w_mkv, w_mo, norm_ffn, w_router_g, w_router_e, w_gate, w_up, w_down, norm_final):
    bsz, seq, _ = x.shape
    x2d = x.reshape(bsz * seq, D_MODEL)
    qkv, gd, z, ab = _inproj(x2d, norm_mix, w_in)
    slopes = jnp.exp2(-8.0 * jnp.arange(1, SWA_HEADS + 1, dtype=F32) / SWA_HEADS)
    branches = [_attn_branch(qkv, slopes, d, bsz, seq) for d in DILATIONS]
    ob = _gdn(gd, z, ab, conv_w, a_log, dt_bias, gdn_norm, bsz, seq)
    mem_k, mem_v = _memkv(mem.reshape(-1, D_MODEL), norm_mem_kv, w_mkv)
    mem_k = mem_k.reshape(bsz, -1, MEM_HEADS * MEM_HEAD_DIM)
    mem_v = mem_v.reshape(bsz, -1, MEM_HEADS * MEM_HEAD_DIM)
    x2, h2, route, counts = _trunk(x2d, branches, ob, mem_k, mem_v, w_out, norm_mem_q, w_mq, w_mo,
                                   norm_ffn, w_router_g, w_router_e, bsz, seq)
    y = _moe(x2, h2, route, counts, w_gate, w_up, w_down, norm_final)
    return y.reshape(bsz, seq, D_MODEL)


def kernel(x_prompt, x_sample, mem_prompt, mem_sample, norm_mix, w_in, conv_w, a_log, dt_bias, gdn_norm,
           w_out, norm_mem_q, norm_mem_kv, w_mq, w_mkv, w_mo, norm_ffn, w_router_g, w_router_e,
           w_gate, w_up, w_down, norm_final):
    p = dict(norm_mix=norm_mix[0], w_in=w_in[0], conv_w=conv_w[0], a_log=a_log[0], dt_bias=dt_bias[0],
             gdn_norm=gdn_norm[0], w_out=w_out[0], norm_mem_q=norm_mem_q[0], norm_mem_kv=norm_mem_kv[0],
             w_mq=w_mq[0], w_mkv=w_mkv[0], w_mo=w_mo[0], norm_ffn=norm_ffn[0], w_router_g=w_router_g[0],
             w_router_e=w_router_e[0], w_gate=w_gate[0].astype(BF16), w_up=w_up[0].astype(BF16),
             w_down=w_down[0].astype(BF16), norm_final=norm_final)
    return (_encoder(x_prompt, mem_prompt, **p), _encoder(x_sample, mem_sample, **p))
```

```python
import functools

import jax
import jax.numpy as jnp
import numpy as np
from jax import lax
from jax.experimental import pallas as pl
from jax.experimental.pallas import tpu as pltpu

F32 = jnp.float32
BF16 = jnp.bfloat16

D_MODEL = 1024
HEAD_DIM = 64
SWA_HEADS = 8
GDN_HEADS = 8
SWA_WIDTH = SWA_HEADS * HEAD_DIM
GDN_WIDTH = GDN_HEADS * HEAD_DIM
DILATIONS = (1, 4, 16)
ATT_W = 64
CONV_K = 5
CHUNK = 64
MEM_HEADS = 4
MEM_HEAD_DIM = 256
N_GROUPS = 4
EXPERTS_PER_GROUP = 8
N_EXPERTS = 32
TOP_K = 2
D_EXPERT = 512
ROUTE_BLOCK = 256
EPS = 1e-6

LANES = 128
VMEM_LIMIT = 56 * 1024 * 1024
NEG_BIG = -1e30


def _dot(a, b):
    return jnp.dot(a, b, preferred_element_type=F32)


def _dot_nt(a, b):
    return lax.dot_general(a, b, (((1,), (1,)), ((), ())), preferred_element_type=F32)


def _dot_tn(a, b):
    return lax.dot_general(a, b, (((0,), (0,)), ((), ())), preferred_element_type=F32)


def _split2(x):
    hi = x.astype(BF16)
    lo = (x - hi.astype(F32)).astype(BF16)
    return hi, lo


def _split3(x):
    hi = x.astype(BF16)
    r = x - hi.astype(F32)
    mid = r.astype(BF16)
    lo = (r - mid.astype(F32)).astype(BF16)
    return hi, mid, lo


def _rms(x, g):
    return x * lax.rsqrt(jnp.mean(x * x, axis=-1, keepdims=True) + EPS) * g


def _params(*sem):
    return pltpu.CompilerParams(dimension_semantics=sem, vmem_limit_bytes=VMEM_LIMIT)


def _inproj_body(x_ref, g_ref, wa_ref, wgd_ref, wz_ref, wab_hi_ref, wab_lo_ref,
                 qkv1_ref, qkv4_ref, qkv16_ref, gd_ref, z_ref, ab_ref, stage_ref, *, tm):
    h = _rms(x_ref[...], g_ref[...])
    h_hi, h_lo = _split2(h)
    qkv = _dot(h_hi, wa_ref[...])
    qkv1_ref[...] = qkv.astype(BF16)
    for c in range(qkv.shape[1] // LANES):
        cols = slice(c * LANES, (c + 1) * LANES)
        stage_ref[c] = qkv[:, cols]
        for dil, out_ref in ((DILATIONS[1], qkv4_ref), (DILATIONS[2], qkv16_ref)):
            for r in range(dil):
                out_ref[r, :, cols] = stage_ref[c, pl.ds(r, tm // dil, stride=dil), :].astype(BF16)
    gd_ref[...] = _dot(h_hi, wgd_ref[...]).astype(BF16)
    z_ref[...] = _dot(h_hi, wz_ref[...]).astype(BF16)
    wab_hi = wab_hi_ref[...]
    ab = _dot(h_hi, wab_hi) + _dot(h_lo, wab_hi) + _dot(h_hi, wab_lo_ref[...])
    ab_ref[...] = ab


def _inproj(x2, norm_mix, w_in, bsz, seq, tm=512):
    n = x2.shape[0]
    nblk = seq // tm
    width = 3 * SWA_WIDTH
    wa = w_in[:, :3 * SWA_WIDTH].astype(BF16)
    wgd = w_in[:, 3 * SWA_WIDTH:3 * SWA_WIDTH + 3 * GDN_WIDTH].astype(BF16)
    wz = w_in[:, 3 * SWA_WIDTH + 3 * GDN_WIDTH:3 * SWA_WIDTH + 4 * GDN_WIDTH].astype(BF16)
    wab = jnp.pad(w_in[:, 3 * SWA_WIDTH + 4 * GDN_WIDTH:], ((0, 0), (0, LANES - 4 * GDN_HEADS)))
    wab_hi, wab_lo = _split2(wab)
    full = lambda shape: pl.BlockSpec(shape, lambda b, i: (0, 0))
    rows = lambda w: pl.BlockSpec((tm, w), lambda b, i: (b * nblk + i, 0))
    strided = lambda d: pl.BlockSpec((None, d, tm // d, width), lambda b, i: (b, 0, i, 0))
    d1, d4, d16 = DILATIONS
    return pl.pallas_call(
        functools.partial(_inproj_body, tm=tm),
        grid=(bsz, nblk),
        in_specs=[rows(D_MODEL), full((1, D_MODEL)), full(wa.shape), full(wgd.shape), full(wz.shape),
                  full(wab_hi.shape), full(wab_lo.shape)],
        out_specs=[pl.BlockSpec((None, None, tm, width), lambda b, i: (b, 0, i, 0)), strided(d4), strided(d16),
                   rows(3 * GDN_WIDTH), rows(GDN_WIDTH), rows(LANES)],
        out_shape=[jax.ShapeDtypeStruct((bsz, d, seq // d, width), BF16) for d in (d1, d4, d16)]
                  + [jax.ShapeDtypeStruct((n, 3 * GDN_WIDTH), BF16),
                     jax.ShapeDtypeStruct((n, GDN_WIDTH), BF16),
                     jax.ShapeDtypeStruct((n, LANES), F32)],
        scratch_shapes=[pltpu.VMEM((width // LANES, tm, LANES), F32)],
        compiler_params=_params("parallel", "parallel"),
        name="inproj",
    )(x2, norm_mix.reshape(1, D_MODEL), wa, wgd, wz, wab_hi, wab_lo)


ATT_QB = 128
ATT_KB = ATT_QB + 2 * ATT_W
ATT_GROUP = 4


def _attn_bias(slopes, dil):
    row = jnp.arange(ATT_QB)[:, None]
    col = jnp.arange(ATT_KB)[None, :]
    rel = jnp.stack([jnp.abs(col - row - var * ATT_W) for var in range(3)])
    dist = (rel * dil).astype(F32)
    bias = jnp.where(rel <= ATT_W, -slopes[:, None, None, None] * dist, NEG_BIG)
    bias = bias.reshape(SWA_HEADS // 2, 2, 3, ATT_QB, ATT_KB).transpose(0, 2, 1, 3, 4)
    return bias.reshape(SWA_HEADS // 2, 6, ATT_QB, ATT_KB)


def _attn_body(bias_ref, q_ref, k_ref, v_ref, o_ref, lse_ref, *, n_qb, seq_l, n_pw):
    lb = pl.program_id(3)
    lane = lax.broadcasted_iota(jnp.int32, (1, LANES), 1)
    left = lane < HEAD_DIM
    group = min(ATT_GROUP, n_qb)

    def qgroup(gi, carry):
        rows, var, kb, vb, q, pw_of, cols = [], [], [], [], [], [], []
        for j in range(group):
            qi = gi * group + j
            n0 = (lb * n_qb + qi) * ATT_QB
            kstart = pl.multiple_of(jnp.clip(n0 - ATT_W, 0, seq_l - ATT_KB), ATT_W)
            for pw in range(n_pw):
                cs = slice(pw * LANES, (pw + 1) * LANES)
                var.append((n0 - kstart) // ATT_W)
                rows.append(pl.ds(pl.multiple_of(qi * ATT_QB, ATT_QB), ATT_QB))
                q.append(q_ref[rows[-1], cs] * jnp.asarray(HEAD_DIM ** -0.5, BF16))
                kb.append(k_ref[pl.ds(kstart, ATT_KB), cs])
                vb.append(v_ref[pl.ds(kstart, ATT_KB), cs])
                pw_of.append(pw)
                cols.append(cs)
        blocks = range(group * n_pw)
        units = [(j, h) for j in blocks for h in range(2)]
        mine = [left, jnp.logical_not(left)]
        s = {(j, h): _dot_nt(jnp.where(mine[h], q[j], jnp.zeros_like(q[j])), kb[j])
                     + bias_ref[pw_of[j], var[j] * 2 + h] for (j, h) in units}
        m = {u: jnp.max(s[u], axis=-1, keepdims=True) for u in units}
        p = {u: jnp.exp(s[u] - m[u]).astype(BF16) for u in units}
        acc = {(j, h): _dot(p[(j, h)], jnp.where(mine[h], vb[j], jnp.ones_like(vb[j]))) for (j, h) in units}
        for j in blocks:
            num = jnp.where(left, acc[(j, 0)], acc[(j, 1)])
            den = pltpu.roll(jnp.where(left, acc[(j, 1)], acc[(j, 0)]), HEAD_DIM, 1)
            mx = jnp.where(left, m[(j, 0)], m[(j, 1)])
            o_ref[rows[j], cols[j]] = (num / den).astype(BF16)
            lse_ref[rows[j], cols[j]] = mx + jnp.log(den)
        return carry

    lax.fori_loop(0, n_qb // group, qgroup, 0)


def _attn_branch(qkv, slopes, dil):
    bsz, _, seq_l, _ = qkv.shape
    lblk = min(2048, seq_l)
    n_qb = lblk // ATT_QB
    n_pw = 1 if n_qb >= ATT_GROUP else 2
    n_steps = SWA_HEADS // 2 // n_pw
    width = n_pw * LANES
    qspec = pl.BlockSpec((None, None, lblk, width), lambda p, b, r, l: (b, r, l, p))
    kspec = pl.BlockSpec((None, None, seq_l, width), lambda p, b, r, l: (b, r, 0, n_steps + p))
    vspec = pl.BlockSpec((None, None, seq_l, width), lambda p, b, r, l: (b, r, 0, 2 * n_steps + p))
    ospec = pl.BlockSpec((None, None, lblk, width), lambda p, b, r, l: (b, r, l, p))
    bspec = pl.BlockSpec((n_pw, 6, ATT_QB, ATT_KB), lambda p, b, r, l: (p, 0, 0, 0))
    return pl.pallas_call(
        functools.partial(_attn_body, n_qb=n_qb, seq_l=seq_l, n_pw=n_pw),
        grid=(n_steps, bsz, dil, seq_l // lblk),
        in_specs=[bspec, qspec, kspec, vspec],
        out_specs=[ospec, ospec],
        out_shape=[jax.ShapeDtypeStruct((bsz, dil, seq_l, SWA_WIDTH), BF16),
                   jax.ShapeDtypeStruct((bsz, dil, seq_l, SWA_WIDTH), F32)],
        compiler_params=_params("parallel", "parallel", "parallel", "arbitrary"),
        name=f"dilated_attn_d{dil}",
    )(_attn_bias(slopes, dil), qkv, qkv, qkv)


def _memkv_body(m_ref, g_ref, w_ref, k_ref, v_ref):
    h = _rms(m_ref[...], g_ref[...]).astype(BF16)
    kv = _dot(h, w_ref[...])
    width = MEM_HEADS * MEM_HEAD_DIM
    k_ref[...] = (kv[:, :width] * (MEM_HEAD_DIM ** -0.5)).astype(BF16)
    v_ref[...] = kv[:, width:].astype(BF16)


def _memkv(mem2, norm_kv, w_mkv, tm=256):
    n = mem2.shape[0]
    width = MEM_HEADS * MEM_HEAD_DIM
    return pl.pallas_call(
        _memkv_body,
        grid=(n // tm,),
        in_specs=[pl.BlockSpec((tm, D_MODEL), lambda i: (i, 0)),
                  pl.BlockSpec((1, D_MODEL), lambda i: (0, 0)),
                  pl.BlockSpec((D_MODEL, 2 * width), lambda i: (0, 0))],
        out_specs=[pl.BlockSpec((tm, width), lambda i: (i, 0))] * 2,
        out_shape=[jax.ShapeDtypeStruct((n, width), BF16)] * 2,
        compiler_params=_params("parallel"),
        name="mem_kv",
    )(mem2, norm_kv.reshape(1, D_MODEL), w_mkv.astype(BF16))


ROUTE_COLS = N_GROUPS + N_EXPERTS


TRUNK_TM = 512
TRUNK_SUB = 256


def _trunk_body(x_ref, o1_ref, o2_ref, o3_ref, l1_ref, l2_ref, l3_ref, ob_ref,
                wo_a_ref, wo_b_ref, gq_ref, wq_ref, k_ref, v_ref, wmo_ref,
                gf_ref, wr_hi_ref, wr_lo_ref,
                x2_ref, h2_ref, route_ref, count_ref, run_ref, obuf_ref, lbuf_ref):
    first = jnp.logical_and(pl.program_id(0) == 0, pl.program_id(1) == 0)

    @pl.when(first)
    def _():
        run_ref[...] = jnp.zeros_like(run_ref)

    n_cb = SWA_WIDTH // LANES
    for bi, (o_ref, l_ref) in enumerate(((o2_ref, l2_ref), (o3_ref, l3_ref))):
        dil = DILATIONS[bi + 1]
        for r in range(dil):
            rows = pl.ds(r, TRUNK_TM // dil, stride=dil)
            o_r = o_ref[r].astype(F32)
            l_r = l_ref[r]
            for c in range(n_cb):
                obuf_ref[bi * n_cb + c, rows, :] = o_r[:, c * LANES:(c + 1) * LANES]
                lbuf_ref[bi * n_cb + c, rows, :] = l_r[:, c * LANES:(c + 1) * LANES]

    slabs = [slice(j * TRUNK_SUB, (j + 1) * TRUNK_SUB) for j in range(TRUNK_TM // TRUNK_SUB)]

    def token_major(buf_ref, bi, sl):
        return jnp.concatenate([buf_ref[bi * n_cb + c, sl, :] for c in range(n_cb)], axis=-1)

    def merge(sl):
        l1, l2, l3 = l1_ref[sl, :], token_major(lbuf_ref, 0, sl), token_major(lbuf_ref, 1, sl)
        mx = jnp.maximum(jnp.maximum(l1, l2), l3)
        e1, e2, e3 = jnp.exp(l1 - mx), jnp.exp(l2 - mx), jnp.exp(l3 - mx)
        ya = (e1 * o1_ref[sl, :].astype(F32) + e2 * token_major(obuf_ref, 0, sl)
              + e3 * token_major(obuf_ref, 1, sl))
        return (ya / (e1 + e2 + e3)).astype(BF16)

    ya = [merge(sl) for sl in slabs]
    x1 = [x_ref[sl, :] + _dot(y, wo_a_ref[...]) + _dot(ob_ref[sl, :], wo_b_ref[...]) for y, sl in zip(ya, slabs)]

    q = [_dot(_rms(x, gq_ref[...]).astype(BF16), wq_ref[...]).astype(BF16) for x in x1]
    cols = [slice(h * MEM_HEAD_DIM, (h + 1) * MEM_HEAD_DIM) for h in range(MEM_HEADS)]
    units = [(j, cs) for j in range(len(slabs)) for cs in cols]
    s = [_dot_nt(q[j][:, cs], k_ref[:, cs]) for j, cs in units]
    p = [jnp.exp(sh - jnp.max(sh, axis=-1, keepdims=True)) for sh in s]
    den = [jnp.sum(ph, axis=-1, keepdims=True) for ph in p]
    pv = [_dot(ph.astype(BF16), v_ref[:, cs]) for ph, (j, cs) in zip(p, units)]
    heads = [(a / d).astype(BF16) for a, d in zip(pv, den)]
    nh = MEM_HEADS
    x2 = [x + _dot(jnp.concatenate(heads[j * nh:(j + 1) * nh], axis=-1), wmo_ref[...]) for j, x in enumerate(x1)]
    h2 = [_rms(x, gf_ref[...]) for x in x2]
    for sl, x, h in zip(slabs, x2, h2):
        x2_ref[sl, :] = x
        h2_ref[sl, :] = h

    wr_hi = wr_hi_ref[...]

    def route_logits(h):
        h_hi, h_lo = _split2(h)
        return _dot(h_hi, wr_hi) + _dot(h_lo, wr_hi) + _dot(h_hi, wr_lo_ref[...])

    logits_all = [route_logits(h) for h in h2]
    lane = lax.broadcasted_iota(jnp.int32, (TRUNK_SUB, LANES), 1)
    big = jnp.int32(LANES)
    is_g = lane < N_GROUPS

    def top2(logits):
        lg = jnp.where(is_g, logits, NEG_BIG)
        mg = jnp.max(lg, axis=-1, keepdims=True)
        g_idx = jnp.min(jnp.where(jnp.logical_and(is_g, lg == mg), lane, big), axis=-1, keepdims=True)
        g_w = 1.0 / jnp.sum(jnp.exp(lg - mg), axis=-1, keepdims=True)
        lo_lane = N_GROUPS + g_idx * EXPERTS_PER_GROUP
        in_grp = jnp.logical_and(lane >= lo_lane, lane < lo_lane + EXPERTS_PER_GROUP)
        le = jnp.where(in_grp, logits, NEG_BIG)
        m1 = jnp.max(le, axis=-1, keepdims=True)
        i1 = jnp.min(jnp.where(jnp.logical_and(in_grp, le == m1), lane, big), axis=-1, keepdims=True)
        le2 = jnp.where(lane == i1, NEG_BIG, le)
        m2 = jnp.max(le2, axis=-1, keepdims=True)
        i2 = jnp.min(jnp.where(jnp.logical_and(in_grp, le2 == m2), lane, big), axis=-1, keepdims=True)
        r2 = jnp.exp(m2 - m1)
        return i1, i2, g_w / (1.0 + r2), g_w * r2 / (1.0 + r2)

    picks = [top2(lg) for lg in logits_all]

    r_i = lax.broadcasted_iota(jnp.int32, (TRUNK_SUB, TRUNK_SUB), 0)
    c_i = lax.broadcasted_iota(jnp.int32, (TRUNK_SUB, TRUNK_SUB), 1)
    tri = jnp.where(c_i < r_i, 1.0, 0.0).astype(BF16)
    oh = [jnp.where(jnp.logical_or(lane == i1, lane == i2), 1.0, 0.0) for i1, i2, _, _ in picks]
    prefix = [_dot(tri, o.astype(BF16)) for o in oh]
    base = run_ref[...]
    for sl, (i1, i2, gate1, gate2), o, pre in zip(slabs, picks, oh, prefix):
        before = pre + base
        rank1 = jnp.sum(jnp.where(lane == i1, before, 0.0), axis=-1, keepdims=True)
        rank2 = jnp.sum(jnp.where(lane == i2, before, 0.0), axis=-1, keepdims=True)
        base = base + jnp.sum(o, axis=0, keepdims=True)
        route = jnp.where(lane == 0, (i1 - N_GROUPS).astype(F32), 0.0)
        for j, val in enumerate(((i2 - N_GROUPS).astype(F32), gate1, gate2, rank1, rank2), start=1):
            route = jnp.where(lane == j, val, route)
        route_ref[sl, :] = route
    run_ref[...] = base
    count_ref[...] = base


def _trunk(x2d, branches, ob, mem_k, mem_v, w_out, norm_mem_q, w_mq, w_mo, norm_ffn,
           w_router_g, w_router_e, bsz, seq):
    tm = TRUNK_TM
    n = bsz * seq
    nblk = seq // tm
    (o1, l1), (o2, l2), (o3, l3) = branches
    wo = w_out.astype(BF16)
    wr = jnp.concatenate([w_router_g, jnp.moveaxis(w_router_e, 0, 1).reshape(D_MODEL, N_EXPERTS)], axis=1)
    wr = jnp.pad(wr, ((0, 0), (0, LANES - ROUTE_COLS)))
    wr_hi, wr_lo = _split2(wr)
    rows = lambda w: pl.BlockSpec((tm, w), lambda b, i: (b * nblk + i, 0))
    full = lambda shape: pl.BlockSpec(shape, lambda b, i: (0, 0))
    memspec = pl.BlockSpec((None, mem_k.shape[1], mem_k.shape[2]), lambda b, i: (b, 0, 0))
    d1, d4, d16 = DILATIONS
    natural = pl.BlockSpec((None, None, tm, SWA_WIDTH), lambda b, i: (b, 0, i, 0))
    strided = lambda d: pl.BlockSpec((None, d, tm // d, SWA_WIDTH), lambda b, i: (b, 0, i, 0))
    branch_specs = [natural, strided(d4), strided(d16)]
    return pl.pallas_call(
        _trunk_body,
        grid=(bsz, nblk),
        in_specs=[rows(D_MODEL)] + branch_specs + branch_specs + [rows(GDN_WIDTH)]
                 + [full((SWA_WIDTH, D_MODEL)), full((GDN_WIDTH, D_MODEL)), full((1, D_MODEL)),
                    full((D_MODEL, D_MODEL)), memspec, memspec, full((D_MODEL, D_MODEL)),
                    full((1, D_MODEL)), full((D_MODEL, LANES)), full((D_MODEL, LANES))],
        out_specs=[rows(D_MODEL), rows(D_MODEL), rows(LANES), full((1, LANES))],
        out_shape=[jax.ShapeDtypeStruct((n, D_MODEL), F32), jax.ShapeDtypeStruct((n, D_MODEL), F32),
                   jax.ShapeDtypeStruct((n, LANES), F32), jax.ShapeDtypeStruct((1, LANES), F32)],
        scratch_shapes=[pltpu.VMEM((1, LANES), F32), pltpu.VMEM((2 * SWA_WIDTH // LANES, tm, LANES), F32),
                        pltpu.VMEM((2 * SWA_WIDTH // LANES, tm, LANES), F32)],
        compiler_params=_params("arbitrary", "arbitrary"),
        name="trunk",
    )(x2d, o1, o2, o3, l1, l2, l3, ob, wo[:SWA_WIDTH], wo[SWA_WIDTH:], norm_mem_q.reshape(1, D_MODEL),
      w_mq.astype(BF16), mem_k, mem_v, w_mo.astype(BF16), norm_ffn.reshape(1, D_MODEL), wr_hi, wr_lo)


DMA_UNROLL = 8


def _dispatch_body(dest_ref, h_ref, xs_in_ref, xs_ref, sem, *, tm):
    del xs_in_ref

    def issue(t, c):
        for k in range(TOP_K):
            dst = xs_ref.at[pl.ds(dest_ref[0, 0, t * TOP_K + k], 1)]
            pltpu.make_async_copy(h_ref.at[pl.ds(t, 1)], dst, sem).start()
        return c

    lax.fori_loop(0, tm, issue, 0, unroll=DMA_UNROLL)
    for _ in range(TOP_K):
        pltpu.make_async_copy(h_ref, xs_ref.at[pl.ds(0, tm)], sem).wait()


def _dispatch(h2, dest, n_slots, tm=512):
    n = h2.shape[0]
    dest3 = dest.reshape(n // tm, 1, tm * TOP_K)
    xs0 = jnp.zeros((n_slots, D_MODEL), F32)
    return pl.pallas_call(
        functools.partial(_dispatch_body, tm=tm),
        grid=(n // tm,),
        in_specs=[pl.BlockSpec((1, 1, tm * TOP_K), lambda i: (i, 0, 0), memory_space=pltpu.SMEM),
                  pl.BlockSpec((tm, D_MODEL), lambda i: (i, 0)),
                  pl.BlockSpec(memory_space=pl.ANY)],
        out_specs=pl.BlockSpec(memory_space=pl.ANY),
        out_shape=jax.ShapeDtypeStruct((n_slots, D_MODEL), F32),
        scratch_shapes=[pltpu.SemaphoreType.DMA(())],
        input_output_aliases={2: 0},
        compiler_params=_params("arbitrary"),
        name="moe_dispatch",
    )(dest3, h2, xs0)


def _expert_body(be_ref, nused_ref, x_ref, wg_ref, wu_ref, wd_ref, y_ref):
    i = pl.program_id(0)

    @pl.when(i < nused_ref[0])
    def _():
        x = x_ref[...].astype(BF16)
        a = _dot(x, wg_ref[...])
        b = _dot(x, wu_ref[...])
        hid = (a * jax.nn.sigmoid(a) * b).astype(BF16)
        y_ref[...] = _dot(hid, wd_ref[...])

    @pl.when(i >= nused_ref[0])
    def _():
        y_ref[...] = jnp.zeros_like(y_ref)


def _experts(xs, block_expert, n_used, w_gate, w_up, w_down):
    n_slots = xs.shape[0]
    n_blocks = n_slots // ROUTE_BLOCK
    grid_spec = pltpu.PrefetchScalarGridSpec(
        num_scalar_prefetch=2,
        grid=(n_blocks,),
        in_specs=[pl.BlockSpec((ROUTE_BLOCK, D_MODEL), lambda i, be, nu: (i, 0)),
                  pl.BlockSpec((None, D_MODEL, D_EXPERT), lambda i, be, nu: (be[i], 0, 0)),
                  pl.BlockSpec((None, D_MODEL, D_EXPERT), lambda i, be, nu: (be[i], 0, 0)),
                  pl.BlockSpec((None, D_EXPERT, D_MODEL), lambda i, be, nu: (be[i], 0, 0))],
        out_specs=pl.BlockSpec((ROUTE_BLOCK, D_MODEL), lambda i, be, nu: (i, 0)),
    )
    return pl.pallas_call(
        _expert_body,
        grid_spec=grid_spec,
        out_shape=jax.ShapeDtypeStruct((n_slots, D_MODEL), F32),
        compiler_params=_params("arbitrary"),
        name="moe_experts",
    )(block_expert, n_used, xs, w_gate, w_up, w_down)


def _combine_body(dest_ref, x_ref, route_ref, g_ref, yb_ref, o_ref, buf_ref, sem, *, tm):
    def issue(t, c):
        for k in range(TOP_K):
            src = yb_ref.at[pl.ds(dest_ref[0, 0, t * TOP_K + k], 1)]
            pltpu.make_async_copy(src, buf_ref.at[k, pl.ds(t, 1)], sem).start()
        return c

    lax.fori_loop(0, tm, issue, 0, unroll=DMA_UNROLL)
    for k in range(TOP_K):
        pltpu.make_async_copy(yb_ref.at[pl.ds(0, tm)], buf_ref.at[k], sem).wait()
    route = route_ref[...]
    y = route[:, 2:3] * buf_ref[0] + route[:, 3:4] * buf_ref[1]
    o_ref[...] = _rms(x_ref[...] + y, g_ref[...])


def _combine(x2, route, dest, yb, norm_final, tm=256):
    n = x2.shape[0]
    dest3 = dest.reshape(n // tm, 1, tm * TOP_K)
    return pl.pallas_call(
        functools.partial(_combine_body, tm=tm),
        grid=(n // tm,),
        in_specs=[pl.BlockSpec((1, 1, tm * TOP_K), lambda i: (i, 0, 0), memory_space=pltpu.SMEM),
                  pl.BlockSpec((tm, D_MODEL), lambda i: (i, 0)),
                  pl.BlockSpec((tm, LANES), lambda i: (i, 0)),
                  pl.BlockSpec((1, D_MODEL), lambda i: (0, 0)),
                  pl.BlockSpec(memory_space=pl.ANY)],
        out_specs=pl.BlockSpec((tm, D_MODEL), lambda i: (i, 0)),
        out_shape=jax.ShapeDtypeStruct((n, D_MODEL), F32),
        scratch_shapes=[pltpu.VMEM((TOP_K, tm, D_MODEL), F32), pltpu.SemaphoreType.DMA(())],
        compiler_params=_params("arbitrary"),
        name="moe_combine",
    )(dest3, x2, route, norm_final.reshape(1, D_MODEL), yb)


def _moe(x2, h2, route, counts, w_gate, w_up, w_down, norm_final):
    n = x2.shape[0]
    m_slots = n * TOP_K
    n_blocks = -(-(m_slots + N_EXPERTS * (ROUTE_BLOCK - 1)) // ROUTE_BLOCK)
    cnt = counts[0, N_GROUPS:N_GROUPS + N_EXPERTS].astype(jnp.int32)
    padded = (cnt + ROUTE_BLOCK - 1) // ROUTE_BLOCK * ROUTE_BLOCK
    pad_end = jnp.cumsum(padded)
    seg_start = pad_end - padded
    eid = route[:, 0:TOP_K].astype(jnp.int32)
    rank = route[:, 4:4 + TOP_K].astype(jnp.int32)
    dest = (seg_start[eid] + rank).reshape(-1)
    block_start = jnp.arange(n_blocks, dtype=jnp.int32) * ROUTE_BLOCK
    block_expert = jnp.minimum(jnp.sum(pad_end[None, :] <= block_start[:, None], axis=1), N_EXPERTS - 1)
    block_expert = block_expert.astype(jnp.int32)
    n_used = (pad_end[-1:] // ROUTE_BLOCK).astype(jnp.int32)
    xs = _dispatch(h2, dest, n_blocks * ROUTE_BLOCK)
    yb = _experts(xs, block_expert, n_used, w_gate, w_up, w_down)
    return _combine(x2, route, dest, yb, norm_final)


GDN_TB = 512
GDN_HALO = 16
GDN_GROUP = 4


def _gdn_body(gd_ref, prev_ref, next_ref, abr_ref, cw_ref, alog_ref, dt_ref, *rest, rev, final):
    if final:
        z_ref, oprev_ref, gn_ref, o_ref, xpad_ref, state_ref = rest
    else:
        o_ref, xpad_ref, state_ref = rest
    i = pl.program_id(1)
    nblk = pl.num_programs(1)
    n_chunks = GDN_TB // CHUNK
    n_pairs = GDN_HEADS // 2
    width3 = 3 * GDN_WIDTH

    @pl.when(i == 0)
    def _():
        state_ref[...] = jnp.zeros_like(state_ref)

    blk = (nblk - 1 - i) if rev else i
    zero_halo = jnp.zeros((GDN_HALO, width3), BF16)
    xpad_ref[:GDN_HALO, :] = jnp.where(blk == 0, zero_halo, prev_ref[...])
    xpad_ref[GDN_HALO:GDN_HALO + GDN_TB, :] = gd_ref[...]
    xpad_ref[GDN_HALO + GDN_TB:, :] = jnp.where(blk == nblk - 1, zero_halo, next_ref[...])
    win = CHUNK + 2 * GDN_HALO
    sr = lax.broadcasted_iota(jnp.int32, (CONV_K * CHUNK, 1), 0)
    sc = lax.broadcasted_iota(jnp.int32, (1, win), 1)
    shift_sel = jnp.where(sc == sr % CHUNK + sr // CHUNK + (GDN_HALO - CONV_K // 2), 1.0, 0.0).astype(BF16)

    lane = lax.broadcasted_iota(jnp.int32, (1, LANES), 1)
    left = lane < CHUNK
    tok = lane % CHUNK
    row = lax.broadcasted_iota(jnp.int32, (CHUNK, 1), 0)
    causal = (tok >= row) if rev else (tok <= row)
    strict = (tok > row) if rev else (tok < row)
    eye = jnp.where(tok == row, 1.0, 0.0)
    r2 = lax.broadcasted_iota(jnp.int32, (LANES, 1), 0)
    same_head = (r2 // CHUNK) == (lane // CHUNK)
    bd_ones = jnp.where(same_head, 1.0, 0.0).astype(BF16)
    tri_in = (r2 % CHUNK >= tok) if rev else (r2 % CHUNK <= tok)
    tri_bd = jnp.where(jnp.logical_and(same_head, tri_in), 1.0, 0.0).astype(BF16)
    row8 = lax.broadcasted_iota(jnp.int32, (8, 1), 0)

    def blockdiag(x):
        xb = x.astype(BF16)
        zero = jnp.zeros_like(xb)
        return jnp.concatenate([jnp.where(left, xb, zero), jnp.where(left, zero, xb)], axis=0)

    def wide_mm(x, y):
        return _dot(x.astype(BF16), blockdiag(y))

    def head_sums(x):
        hi, lo = _split2(x)
        return _dot(hi, bd_ones) + _dot(lo, bd_ones)

    def group(gi, carry):
        g0 = (n_chunks // GDN_GROUP - 1 - gi) if rev else gi
        chunks = [g0 * GDN_GROUP + ((GDN_GROUP - 1 - j) if rev else j) for j in range(GDN_GROUP)]
        units = [(ci, p) for ci in range(GDN_GROUP) for p in range(n_pairs)]
        r0 = [pl.multiple_of(c * CHUNK, CHUNK) for c in chunks]

        gates, gc_rows, gl_rows = [], [], []
        for c in chunks:
            graw = abr_ref[c]
            xg = graw + dt_ref[...]
            softplus = jnp.maximum(xg, 0.0) + jnp.log(1.0 + jnp.exp(-jnp.abs(xg)))
            g = jnp.where(row8 < n_pairs, jax.nn.sigmoid(graw), -jnp.exp(alog_ref[...]) * softplus)
            g3 = _split3(g)
            gates.append(g)
            gc_rows.append(sum(_dot(t, tri_bd) for t in g3))
            gl_rows.append(sum(_dot(t, bd_ones) for t in g3))

        shifted = [_dot(shift_sel, xpad_ref[pl.ds(r0[ci], win), :]) for ci in range(GDN_GROUP)]

        def conv(ci, p, col0):
            cols = slice(col0 + p * LANES, col0 + (p + 1) * LANES)
            acc = None
            for j in range(CONV_K):
                term = shifted[ci][j * CHUNK:(j + 1) * CHUNK, cols] * cw_ref[j:j + 1, cols]
                acc = term if acc is None else acc + term
            return acc * jax.nn.sigmoid(acc)

        qkv = {u: [conv(*u, col0) for col0 in (0, GDN_WIDTH, 2 * GDN_WIDTH)] for u in units}
        ss = {u: head_sums(jnp.concatenate([qkv[u][0] * qkv[u][0], qkv[u][1] * qkv[u][1]], axis=0))
              for u in units}
        prep = {}
        for (ci, p) in units:
            beta_r = gates[ci][p:p + 1]
            g_r = gates[ci][n_pairs + p:n_pairs + p + 1]
            lhs = jnp.concatenate([jnp.where(causal, g_r, 0.0), eye * beta_r], axis=0)
            prep[(ci, p)] = sum(_dot(t, bd_ones) for t in _split2(lhs))

        kq, kn_bd, e_gc, k_upd, decay_b = {}, {}, {}, {}, {}
        for u in units:
            ci, p = u
            qn = qkv[u][0] * lax.rsqrt(ss[u][:CHUNK] + EPS) * (HEAD_DIM ** -0.5)
            kn = qkv[u][1] * lax.rsqrt(ss[u][CHUNK:] + EPS)
            kq[u] = jnp.concatenate([kn, qn], axis=0).astype(BF16)
            kn_bd[u] = blockdiag(kn)
            beta_r = gates[ci][p:p + 1]
            gc_r = gc_rows[ci][n_pairs + p:n_pairs + p + 1]
            gl_r = gl_rows[ci][n_pairs + p:n_pairs + p + 1]
            gc_b, beta_b = prep[u][:CHUNK], prep[u][CHUNK:]
            decay_b[u] = jnp.exp(jnp.where(causal, gc_b - gc_r, NEG_BIG)) * beta_r
            e_gc[u] = jnp.exp(gc_b)
            k_upd[u] = (kn * (jnp.exp(gl_r - gc_b) * beta_b)).astype(BF16)

        kk_qk = {u: _dot_nt(kq[u], kn_bd[u]) for u in units}
        lpow = {u: jnp.where(strict, kk_qk[u][:CHUNK] * decay_b[u], 0.0) for u in units}
        amat = {u: kk_qk[u][CHUNK:] * decay_b[u] for u in units}
        tinv = {u: eye - lpow[u] for u in units}
        for _ in range(5):
            lpow = {u: wide_mm(lpow[u], lpow[u]) for u in units}
            tinv = {u: tinv[u] + wide_mm(tinv[u], lpow[u]) for u in units}

        for ci in range(GDN_GROUP):
            us = [(ci, p) for p in range(n_pairs)]
            state = {u: state_ref[u[1]] for u in us}
            pq = {u: _dot(kq[u], state[u].astype(BF16)) for u in us}
            vhat = {u: wide_mm(tinv[u], qkv[u][2] - e_gc[u] * pq[u][:CHUNK]) for u in us}
            upd = {u: _dot_tn(k_upd[u], vhat[u].astype(BF16)) for u in us}
            for u in us:
                gl_r = gl_rows[ci][n_pairs + u[1]:n_pairs + u[1] + 1]
                state_ref[u[1]] = state[u] * jnp.exp(gl_r) + jnp.where(same_head, upd[u], 0.0)
            o = {u: e_gc[u] * pq[u][CHUNK:] + wide_mm(amat[u], vhat[u]) for u in us}
            for u in us:
                cs = slice(u[1] * LANES, (u[1] + 1) * LANES)
                rows = pl.ds(r0[ci], CHUNK)
                if final:
                    ob = o[u] + oprev_ref[rows, cs]
                    ms = head_sums(ob * ob) * (1.0 / HEAD_DIM)
                    zz = z_ref[rows, cs].astype(F32)
                    out = ob * lax.rsqrt(ms + EPS) * gn_ref[:, cs] * (zz * jax.nn.sigmoid(zz))
                    o_ref[rows, cs] = out.astype(o_ref.dtype)
                else:
                    o_ref[rows, cs] = o[u]
        return carry

    lax.fori_loop(0, n_chunks // GDN_GROUP, group, 0)


def _gdn_pass(gd3, abr, conv_w, alog_row, dt_row, rev, final_args=None):
    bsz, seq, width3 = gd3.shape
    nblk = seq // GDN_TB
    hpb = GDN_TB // GDN_HALO
    blk = (lambda i: nblk - 1 - i) if rev else (lambda i: i)
    main = lambda w: pl.BlockSpec((None, GDN_TB, w), lambda b, i: (b, blk(i), 0))
    full = lambda shape: pl.BlockSpec(shape, lambda b, i: (0,) * len(shape))
    in_specs = [main(width3),
                pl.BlockSpec((None, GDN_HALO, width3), lambda b, i: (b, jnp.maximum(blk(i) * hpb - 1, 0), 0)),
                pl.BlockSpec((None, GDN_HALO, width3),
                             lambda b, i: (b, jnp.minimum((blk(i) + 1) * hpb, seq // GDN_HALO - 1), 0)),
                pl.BlockSpec((None, GDN_TB // CHUNK, 8, LANES), lambda b, i: (b, blk(i), 0, 0)),
                full((CONV_K, width3)), full((8, LANES)), full((8, LANES))]
    args = [gd3, gd3, gd3, abr, conv_w, alog_row, dt_row]
    final = final_args is not None
    if final:
        z3, o_prev, gn = final_args
        in_specs += [main(GDN_WIDTH), main(GDN_WIDTH), full((1, GDN_WIDTH))]
        args += [z3, o_prev, gn]
    return pl.pallas_call(
        functools.partial(_gdn_body, rev=rev, final=final),
        grid=(bsz, nblk),
        in_specs=in_specs,
        out_specs=main(GDN_WIDTH),
        out_shape=jax.ShapeDtypeStruct((bsz, seq, GDN_WIDTH), BF16 if final else F32),
        scratch_shapes=[pltpu.VMEM((GDN_TB + 2 * GDN_HALO, width3), BF16),
                        pltpu.VMEM((GDN_HEADS // 2, LANES, LANES), F32)],
        compiler_params=_params("parallel", "arbitrary"),
        name="gdn_bwd" if rev else "gdn_fwd",
    )(*args)


def _pair_rows(x, bsz, seq):
    x = x.reshape(bsz, seq // CHUNK, CHUNK, GDN_HEADS // 2, 2)
    return x.transpose(0, 1, 3, 4, 2).reshape(bsz, seq // CHUNK, GDN_HEADS // 2, LANES)


def _gdn(gd, z, ab, conv_w, a_log, dt_bias, gdn_norm, bsz, seq):
    gd3 = gd.reshape(bsz, seq, 3 * GDN_WIDTH)
    z3 = z.reshape(bsz, seq, GDN_WIDTH)
    gn = jnp.tile(gdn_norm, GDN_HEADS).reshape(1, GDN_WIDTH)
    out = None
    for d in range(2):
        beta = _pair_rows(ab[:, d * GDN_HEADS:(d + 1) * GDN_HEADS], bsz, seq)
        araw = _pair_rows(ab[:, (2 + d) * GDN_HEADS:(3 + d) * GDN_HEADS], bsz, seq)
        abr = jnp.concatenate([beta, araw], axis=2)
        per_head = lambda v: jnp.concatenate(
            [jnp.zeros((GDN_HEADS // 2, LANES), F32),
             jnp.repeat(v[d], CHUNK).reshape(GDN_HEADS // 2, LANES)], axis=0)
        final_args = None if d == 0 else (z3, out, gn)
        out = _gdn_pass(gd3, abr, conv_w, per_head(a_log), per_head(dt_bias), rev=(d == 1), final_args=final_args)
    return out.reshape(bsz * seq, GDN_WIDTH)


def _gdn_scan_jax(q, k, v, g, beta):
    B, T, H, dk = q.shape
    C = CHUNK
    N = T // C
    chunks = lambda t: t.reshape(B, N, C, H, -1).transpose(1, 0, 3, 2, 4)
    qc, kc, vc = chunks(q), chunks(k), chunks(v)
    gc = jnp.cumsum(g.reshape(B, N, C, H).transpose(1, 0, 3, 2), axis=-1)
    bc = beta.reshape(B, N, C, H).transpose(1, 0, 3, 2)[..., None]
    lower = jnp.tril(jnp.ones((C, C), bool))
    strict = jnp.tril(jnp.ones((C, C), bool), -1)
    decay = jnp.exp(jnp.where(lower, gc[..., :, None] - gc[..., None, :], -jnp.inf))
    kbeta = kc * bc
    lmat = jnp.where(strict, jnp.einsum('nbhid,nbhjd->nbhij', kbeta, kc) * decay, 0.0)
    solve = lambda rhs: lax.linalg.triangular_solve(lmat, rhs, left_side=True, lower=True, unit_diagonal=True)
    u = solve(vc * bc)
    w = solve(kbeta * jnp.exp(gc)[..., None])
    a_qk = jnp.einsum('nbhid,nbhjd->nbhij', qc, kc) * decay

    def step(state, inp):
        q_i, k_i, u_i, w_i, g_i, a_i = inp
        v_new = u_i - jnp.einsum('bhck,bhkv->bhcv', w_i, state)
        o_i = (jnp.einsum('bhck,bhkv->bhcv', q_i * jnp.exp(g_i)[..., None], state)
               + jnp.einsum('bhij,bhjv->bhiv', a_i, v_new))
        g_last = g_i[..., -1:]
        state = (state * jnp.exp(g_last)[..., None]
                 + jnp.einsum('bhck,bhcv->bhkv', k_i * jnp.exp(g_last - g_i)[..., None], v_new))
        return state, o_i

    s0 = jnp.zeros((B, H, dk, dk), F32)
    _, o = lax.scan(step, s0, (qc, kc, u, w, gc, a_qk))
    return o.transpose(1, 0, 3, 2, 4).reshape(B, T, H, dk)


def _gdn_jax(gd, z, ab, conv_w, a_log, dt_bias, gdn_norm, bsz, seq):
    c = lax.conv_general_dilated(
        gd.astype(F32).reshape(bsz, seq, 3 * GDN_WIDTH), conv_w[:, None, :], window_strides=(1,),
        padding=[(CONV_K // 2, CONV_K // 2)], dimension_numbers=('NWC', 'WIO', 'NWC'),
        feature_group_count=3 * GDN_WIDTH)
    c = jax.nn.silu(c)
    l2 = lambda t: t * lax.rsqrt(jnp.sum(t * t, axis=-1, keepdims=True) + EPS)
    qb, kb, vb = [t.reshape(bsz, seq, GDN_HEADS, HEAD_DIM) for t in jnp.split(c, 3, axis=-1)]
    qb = l2(qb) * (HEAD_DIM ** -0.5)
    kb = l2(kb)
    ab4 = ab[:, :4 * GDN_HEADS].reshape(bsz, seq, 4, GDN_HEADS)
    beta = jax.nn.sigmoid(ab4[:, :, 0:2])
    g = -jnp.exp(a_log) * jax.nn.softplus(ab4[:, :, 2:4] + dt_bias)
    fwd = _gdn_scan_jax(qb, kb, vb, g[:, :, 0], beta[:, :, 0])
    flip = lambda t: t[:, ::-1]
    bwd = flip(_gdn_scan_jax(flip(qb), flip(kb), flip(vb), flip(g[:, :, 1]), flip(beta[:, :, 1])))
    ob = fwd + bwd
    ob = (ob * lax.rsqrt(jnp.mean(ob * ob, axis=-1, keepdims=True) + EPS) * gdn_norm
          * jax.nn.silu(z.astype(F32).reshape(bsz, seq, GDN_HEADS, HEAD_DIM)))
    return ob.reshape(bsz * seq, GDN_WIDTH).astype(BF16)


def _encoder(x, mem, norm_mix, w_in, conv_w, a_log, dt_bias, gdn_norm, w_out, norm_mem_q, norm_mem_kv,
             w_mq, w_mkv, w_mo, norm_ffn, w_router_g, w_router_e, w_gate, w_up, w_down, norm_final):
    bsz, seq, _ = x.shape
    x2d = x.reshape(bsz * seq, D_MODEL)
    qkv1, qkv4, qkv16, gd, z, ab = _inproj(x2d, norm_mix, w_in, bsz, seq)
    slopes = jnp.exp2(-8.0 * jnp.arange(1, SWA_HEADS + 1, dtype=F32) / SWA_HEADS)
    branches = [_attn_branch(qkv, slopes, d) for qkv, d in zip((qkv1, qkv4, qkv16), DILATIONS)]
    ob = _gdn(gd, z, ab, conv_w, a_log, dt_bias, gdn_norm, bsz, seq)
    mem_k, mem_v = _memkv(mem.reshape(-1, D_MODEL), norm_mem_kv, w_mkv)
    mem_k = mem_k.reshape(bsz, -1, MEM_HEADS * MEM_HEAD_DIM)
    mem_v = mem_v.reshape(bsz, -1, MEM_HEADS * MEM_HEAD_DIM)
    x2, h2, route, counts = _trunk(x2d, branches, ob, mem_k, mem_v, w_out, norm_mem_q, w_mq, w_mo,
                                   norm_ffn, w_router_g, w_router_e, bsz, seq)
    y = _moe(x2, h2, route, counts, w_gate, w_up, w_down, norm_final)
    return y.reshape(bsz, seq, D_MODEL)


def kernel(x_prompt, x_sample, mem_prompt, mem_sample, norm_mix, w_in, conv_w, a_log, dt_bias, gdn_norm,
           w_out, norm_mem_q, norm_mem_kv, w_mq, w_mkv, w_mo, norm_ffn, w_router_g, w_router_e,
           w_gate, w_up, w_down, norm_final):
    p = dict(norm_mix=norm_mix[0], w_in=w_in[0], conv_w=conv_w[0], a_log=a_log[0], dt_bias=dt_bias[0],
             gdn_norm=gdn_norm[0], w_out=w_out[0], norm_mem_q=norm_mem_q[0], norm_mem_kv=norm_mem_kv[0],
             w_mq=w_mq[0], w_mkv=w_mkv[0], w_mo=w_mo[0], norm_ffn=norm_ffn[0], w_router_g=w_router_g[0],
             w_router_e=w_router_e[0], w_gate=w_gate[0].astype(BF16), w_up=w_up[0].astype(BF16),
             w_down=w_down[0].astype(BF16), norm_final=norm_final)
    return (_encoder(x_prompt, mem_prompt, **p), _encoder(x_sample, mem_sample, **p))
```

```python
import functools

import jax
import jax.numpy as jnp
import numpy as np
from jax import lax
from jax.experimental import pallas as pl
from jax.experimental.pallas import tpu as pltpu

F32 = jnp.float32
BF16 = jnp.bfloat16

D_MODEL = 1024
HEAD_DIM = 64
SWA_HEADS = 8
GDN_HEADS = 8
SWA_WIDTH = SWA_HEADS * HEAD_DIM
GDN_WIDTH = GDN_HEADS * HEAD_DIM
DILATIONS = (1, 4, 16)
ATT_W = 64
CONV_K = 5
CHUNK = 64
MEM_HEADS = 4
MEM_HEAD_DIM = 256
N_GROUPS = 4
EXPERTS_PER_GROUP = 8
N_EXPERTS = 32
TOP_K = 2
D_EXPERT = 512
ROUTE_BLOCK = 256
EPS = 1e-6

LANES = 128
VMEM_LIMIT = 56 * 1024 * 1024
NEG_BIG = -1e30


def _dot(a, b):
    return jnp.dot(a, b, preferred_element_type=F32)


def _dot_nt(a, b):
    return lax.dot_general(a, b, (((1,), (1,)), ((), ())), preferred_element_type=F32)


def _dot_tn(a, b):
    return lax.dot_general(a, b, (((0,), (0,)), ((), ())), preferred_element_type=F32)


def _split2(x):
    hi = x.astype(BF16)
    lo = (x - hi.astype(F32)).astype(BF16)
    return hi, lo


def _split3(x):
    hi = x.astype(BF16)
    r = x - hi.astype(F32)
    mid = r.astype(BF16)
    lo = (r - mid.astype(F32)).astype(BF16)
    return hi, mid, lo


def _rms(x, g):
    return x * lax.rsqrt(jnp.mean(x * x, axis=-1, keepdims=True) + EPS) * g


def _params(*sem):
    return pltpu.CompilerParams(dimension_semantics=sem, vmem_limit_bytes=VMEM_LIMIT)


def _inproj_body(x_ref, g_ref, wa_ref, wgd_ref, wz_ref, wab_hi_ref, wab_lo_ref,
                 qkv1_ref, qkv4_ref, qkv16_ref, gd_ref, z_ref, ab_ref, stage_ref, *, tm):
    h = _rms(x_ref[...], g_ref[...])
    h_hi, h_lo = _split2(h)
    qkv = _dot(h_hi, wa_ref[...])
    qkv1_ref[...] = qkv.astype(BF16)
    for c in range(qkv.shape[1] // LANES):
        cols = slice(c * LANES, (c + 1) * LANES)
        stage_ref[c] = qkv[:, cols]
        for dil, out_ref in ((DILATIONS[1], qkv4_ref), (DILATIONS[2], qkv16_ref)):
            for r in range(dil):
                out_ref[r, :, cols] = stage_ref[c, pl.ds(r, tm // dil, stride=dil), :].astype(BF16)
    gd_ref[...] = _dot(h_hi, wgd_ref[...]).astype(BF16)
    z_ref[...] = _dot(h_hi, wz_ref[...]).astype(BF16)
    wab_hi = wab_hi_ref[...]
    ab = _dot(h_hi, wab_hi) + _dot(h_lo, wab_hi) + _dot(h_hi, wab_lo_ref[...])
    ab_ref[...] = ab


def _inproj(x2, norm_mix, w_in, bsz, seq, tm=512):
    n = x2.shape[0]
    nblk = seq // tm
    width = 3 * SWA_WIDTH
    wa = w_in[:, :3 * SWA_WIDTH].astype(BF16)
    wgd = w_in[:, 3 * SWA_WIDTH:3 * SWA_WIDTH + 3 * GDN_WIDTH].astype(BF16)
    wz = w_in[:, 3 * SWA_WIDTH + 3 * GDN_WIDTH:3 * SWA_WIDTH + 4 * GDN_WIDTH].astype(BF16)
    wab = jnp.pad(w_in[:, 3 * SWA_WIDTH + 4 * GDN_WIDTH:], ((0, 0), (0, LANES - 4 * GDN_HEADS)))
    wab_hi, wab_lo = _split2(wab)
    full = lambda shape: pl.BlockSpec(shape, lambda b, i: (0, 0))
    rows = lambda w: pl.BlockSpec((tm, w), lambda b, i: (b * nblk + i, 0))
    strided = lambda d: pl.BlockSpec((None, d, tm // d, width), lambda b, i: (b, 0, i, 0))
    d1, d4, d16 = DILATIONS
    return pl.pallas_call(
        functools.partial(_inproj_body, tm=tm),
        grid=(bsz, nblk),
        in_specs=[rows(D_MODEL), full((1, D_MODEL)), full(wa.shape), full(wgd.shape), full(wz.shape),
                  full(wab_hi.shape), full(wab_lo.shape)],
        out_specs=[pl.BlockSpec((None, None, tm, width), lambda b, i: (b, 0, i, 0)), strided(d4), strided(d16),
                   rows(3 * GDN_WIDTH), rows(GDN_WIDTH), rows(LANES)],
        out_shape=[jax.ShapeDtypeStruct((bsz, d, seq // d, width), BF16) for d in (d1, d4, d16)]
                  + [jax.ShapeDtypeStruct((n, 3 * GDN_WIDTH), BF16),
                     jax.ShapeDtypeStruct((n, GDN_WIDTH), BF16),
                     jax.ShapeDtypeStruct((n, LANES), F32)],
        scratch_shapes=[pltpu.VMEM((width // LANES, tm, LANES), F32)],
        compiler_params=_params("parallel", "parallel"),
        name="inproj",
    )(x2, norm_mix.reshape(1, D_MODEL), wa, wgd, wz, wab_hi, wab_lo)


ATT_QB = 128
ATT_KB = ATT_QB + 2 * ATT_W
ATT_GROUP = 4


def _attn_bias(slopes, dil):
    row = jnp.arange(ATT_QB)[:, None]
    col = jnp.arange(ATT_KB)[None, :]
    rel = jnp.stack([jnp.abs(col - row - var * ATT_W) for var in range(3)])
    dist = (rel * dil).astype(F32)
    bias = jnp.where(rel <= ATT_W, -slopes[:, None, None, None] * dist, NEG_BIG)
    bias = bias.reshape(SWA_HEADS // 2, 2, 3, ATT_QB, ATT_KB).transpose(0, 2, 1, 3, 4)
    return bias.reshape(SWA_HEADS // 2, 6, ATT_QB, ATT_KB)


def _attn_body(bias_ref, q_ref, k_ref, v_ref, o_ref, lse_ref, *, n_qb, seq_l, n_pw):
    lb = pl.program_id(3)
    lane = lax.broadcasted_iota(jnp.int32, (1, LANES), 1)
    left = lane < HEAD_DIM
    group = min(ATT_GROUP, n_qb)

    def qgroup(gi, carry):
        rows, var, kb, vb, q, pw_of, cols = [], [], [], [], [], [], []
        for j in range(group):
            qi = gi * group + j
            n0 = (lb * n_qb + qi) * ATT_QB
            kstart = pl.multiple_of(jnp.clip(n0 - ATT_W, 0, seq_l - ATT_KB), ATT_W)
            for pw in range(n_pw):
                cs = slice(pw * LANES, (pw + 1) * LANES)
                var.append((n0 - kstart) // ATT_W)
                rows.append(pl.ds(pl.multiple_of(qi * ATT_QB, ATT_QB), ATT_QB))
                q.append(q_ref[rows[-1], cs] * jnp.asarray(HEAD_DIM ** -0.5, BF16))
                kb.append(k_ref[pl.ds(kstart, ATT_KB), cs])
                vb.append(v_ref[pl.ds(kstart, ATT_KB), cs])
                pw_of.append(pw)
                cols.append(cs)
        blocks = range(group * n_pw)
        units = [(j, h) for j in blocks for h in range(2)]
        mine = [left, jnp.logical_not(left)]
        s = {(j, h): _dot_nt(jnp.where(mine[h], q[j], jnp.zeros_like(q[j])), kb[j])
                     + bias_ref[pw_of[j], var[j] * 2 + h] for (j, h) in units}
        m = {u: jnp.max(s[u], axis=-1, keepdims=True) for u in units}
        p = {u: jnp.exp(s[u] - m[u]).astype(BF16) for u in units}
        acc = {(j, h): _dot(p[(j, h)], jnp.where(mine[h], vb[j], jnp.ones_like(vb[j]))) for (j, h) in units}
        for j in blocks:
            num = jnp.where(left, acc[(j, 0)], acc[(j, 1)])
            den = pltpu.roll(jnp.where(left, acc[(j, 1)], acc[(j, 0)]), HEAD_DIM, 1)
            mx = jnp.where(left, m[(j, 0)], m[(j, 1)])
            o_ref[rows[j], cols[j]] = (num / den).astype(BF16)
            lse_ref[rows[j], cols[j]] = mx + jnp.log(den)
        return carry

    lax.fori_loop(0, n_qb // group, qgroup, 0)


def _attn_branch(qkv, slopes, dil):
    bsz, _, seq_l, _ = qkv.shape
    lblk = min(2048, seq_l)
    n_qb = lblk // ATT_QB
    n_pw = 1 if n_qb >= ATT_GROUP else 2
    n_steps = SWA_HEADS // 2 // n_pw
    width = n_pw * LANES
    qspec = pl.BlockSpec((None, None, lblk, width), lambda p, b, r, l: (b, r, l, p))
    kspec = pl.BlockSpec((None, None, seq_l, width), lambda p, b, r, l: (b, r, 0, n_steps + p))
    vspec = pl.BlockSpec((None, None, seq_l, width), lambda p, b, r, l: (b, r, 0, 2 * n_steps + p))
    ospec = pl.BlockSpec((None, None, lblk, width), lambda p, b, r, l: (b, r, l, p))
    bspec = pl.BlockSpec((n_pw, 6, ATT_QB, ATT_KB), lambda p, b, r, l: (p, 0, 0, 0))
    return pl.pallas_call(
        functools.partial(_attn_body, n_qb=n_qb, seq_l=seq_l, n_pw=n_pw),
        grid=(n_steps, bsz, dil, seq_l // lblk),
        in_specs=[bspec, qspec, kspec, vspec],
        out_specs=[ospec, ospec],
        out_shape=[jax.ShapeDtypeStruct((bsz, dil, seq_l, SWA_WIDTH), BF16),
                   jax.ShapeDtypeStruct((bsz, dil, seq_l, SWA_WIDTH), F32)],
        compiler_params=_params("parallel", "parallel", "parallel", "arbitrary"),
        name=f"dilated_attn_d{dil}",
    )(_attn_bias(slopes, dil), qkv, qkv, qkv)


def _memkv_body(m_ref, g_ref, w_ref, k_ref, v_ref):
    h = _rms(m_ref[...], g_ref[...]).astype(BF16)
    kv = _dot(h, w_ref[...])
    width = MEM_HEADS * MEM_HEAD_DIM
    k_ref[...] = (kv[:, :width] * (MEM_HEAD_DIM ** -0.5)).astype(BF16)
    v_ref[...] = kv[:, width:].astype(BF16)


def _memkv(mem2, norm_kv, w_mkv, tm=256):
    n = mem2.shape[0]
    width = MEM_HEADS * MEM_HEAD_DIM
    return pl.pallas_call(
        _memkv_body,
        grid=(n // tm,),
        in_specs=[pl.BlockSpec((tm, D_MODEL), lambda i: (i, 0)),
                  pl.BlockSpec((1, D_MODEL), lambda i: (0, 0)),
                  pl.BlockSpec((D_MODEL, 2 * width), lambda i: (0, 0))],
        out_specs=[pl.BlockSpec((tm, width), lambda i: (i, 0))] * 2,
        out_shape=[jax.ShapeDtypeStruct((n, width), BF16)] * 2,
        compiler_params=_params("parallel"),
        name="mem_kv",
    )(mem2, norm_kv.reshape(1, D_MODEL), w_mkv.astype(BF16))


ROUTE_COLS = N_GROUPS + N_EXPERTS


TRUNK_TM = 512
TRUNK_SUB = 256


def _trunk_body(x_ref, o1_ref, o2_ref, o3_ref, l1_ref, l2_ref, l3_ref, ob_ref,
                wo_a_ref, wo_b_ref, gq_ref, wq_ref, k_ref, v_ref, wmo_ref,
                gf_ref, wr_hi_ref, wr_lo_ref,
                x2_ref, h2_ref, route_ref, count_ref, run_ref, obuf_ref, lbuf_ref):
    first = jnp.logical_and(pl.program_id(0) == 0, pl.program_id(1) == 0)

    @pl.when(first)
    def _():
        run_ref[...] = jnp.zeros_like(run_ref)

    n_cb = SWA_WIDTH // LANES
    for bi, (o_ref, l_ref) in enumerate(((o2_ref, l2_ref), (o3_ref, l3_ref))):
        dil = DILATIONS[bi + 1]
        for r in range(dil):
            rows = pl.ds(r, TRUNK_TM // dil, stride=dil)
            o_r = o_ref[r].astype(F32)
            l_r = l_ref[r]
            for c in range(n_cb):
                obuf_ref[bi * n_cb + c, rows, :] = o_r[:, c * LANES:(c + 1) * LANES]
                lbuf_ref[bi * n_cb + c, rows, :] = l_r[:, c * LANES:(c + 1) * LANES]

    slabs = [slice(j * TRUNK_SUB, (j + 1) * TRUNK_SUB) for j in range(TRUNK_TM // TRUNK_SUB)]

    def token_major(buf_ref, bi, sl):
        return jnp.concatenate([buf_ref[bi * n_cb + c, sl, :] for c in range(n_cb)], axis=-1)

    def merge(sl):
        l1, l2, l3 = l1_ref[sl, :], token_major(lbuf_ref, 0, sl), token_major(lbuf_ref, 1, sl)
        mx = jnp.maximum(jnp.maximum(l1, l2), l3)
        e1, e2, e3 = jnp.exp(l1 - mx), jnp.exp(l2 - mx), jnp.exp(l3 - mx)
        ya = (e1 * o1_ref[sl, :].astype(F32) + e2 * token_major(obuf_ref, 0, sl)
              + e3 * token_major(obuf_ref, 1, sl))
        return (ya / (e1 + e2 + e3)).astype(BF16)

    ya = [merge(sl) for sl in slabs]
    x1 = [x_ref[sl, :] + _dot(y, wo_a_ref[...]) + _dot(ob_ref[sl, :], wo_b_ref[...]) for y, sl in zip(ya, slabs)]

    q = [_dot(_rms(x, gq_ref[...]).astype(BF16), wq_ref[...]).astype(BF16) for x in x1]
    cols = [slice(h * MEM_HEAD_DIM, (h + 1) * MEM_HEAD_DIM) for h in range(MEM_HEADS)]
    units = [(j, cs) for j in range(len(slabs)) for cs in cols]
    s = [_dot_nt(q[j][:, cs], k_ref[:, cs]) for j, cs in units]
    p = [jnp.exp(sh - jnp.max(sh, axis=-1, keepdims=True)) for sh in s]
    den = [jnp.sum(ph, axis=-1, keepdims=True) for ph in p]
    pv = [_dot(ph.astype(BF16), v_ref[:, cs]) for ph, (j, cs) in zip(p, units)]
    heads = [(a / d).astype(BF16) for a, d in zip(pv, den)]
    nh = MEM_HEADS
    x2 = [x + _dot(jnp.concatenate(heads[j * nh:(j + 1) * nh], axis=-1), wmo_ref[...]) for j, x in enumerate(x1)]
    h2 = [_rms(x, gf_ref[...]) for x in x2]
    for sl, x, h in zip(slabs, x2, h2):
        x2_ref[sl, :] = x
        h2_ref[sl, :] = h

    wr_hi = wr_hi_ref[...]

    def route_logits(h):
        h_hi, h_lo = _split2(h)
        return _dot(h_hi, wr_hi) + _dot(h_lo, wr_hi) + _dot(h_hi, wr_lo_ref[...])

    logits_all = [route_logits(h) for h in h2]
    lane = lax.broadcasted_iota(jnp.int32, (TRUNK_SUB, LANES), 1)
    big = jnp.int32(LANES)
    is_g = lane < N_GROUPS

    def top2(logits):
        lg = jnp.where(is_g, logits, NEG_BIG)
        mg = jnp.max(lg, axis=-1, keepdims=True)
        g_idx = jnp.min(jnp.where(jnp.logical_and(is_g, lg == mg), lane, big), axis=-1, keepdims=True)
        g_w = 1.0 / jnp.sum(jnp.exp(lg - mg), axis=-1, keepdims=True)
        lo_lane = N_GROUPS + g_idx * EXPERTS_PER_GROUP
        in_grp = jnp.logical_and(lane >= lo_lane, lane < lo_lane + EXPERTS_PER_GROUP)
        le = jnp.where(in_grp, logits, NEG_BIG)
        m1 = jnp.max(le, axis=-1, keepdims=True)
        i1 = jnp.min(jnp.where(jnp.logical_and(in_grp, le == m1), lane, big), axis=-1, keepdims=True)
        le2 = jnp.where(lane == i1, NEG_BIG, le)
        m2 = jnp.max(le2, axis=-1, keepdims=True)
        i2 = jnp.min(jnp.where(jnp.logical_and(in_grp, le2 == m2), lane, big), axis=-1, keepdims=True)
        r2 = jnp.exp(m2 - m1)
        return i1, i2, g_w / (1.0 + r2), g_w * r2 / (1.0 + r2)

    picks = [top2(lg) for lg in logits_all]

    r_i = lax.broadcasted_iota(jnp.int32, (TRUNK_SUB, TRUNK_SUB), 0)
    c_i = lax.broadcasted_iota(jnp.int32, (TRUNK_SUB, TRUNK_SUB), 1)
    tri = jnp.where(c_i < r_i, 1.0, 0.0).astype(BF16)
    oh = [jnp.where(jnp.logical_or(lane == i1, lane == i2), 1.0, 0.0) for i1, i2, _, _ in picks]
    prefix = [_dot(tri, o.astype(BF16)) for o in oh]
    base = run_ref[...]
    for sl, (i1, i2, gate1, gate2), o, pre in zip(slabs, picks, oh, prefix):
        before = pre + base
        rank1 = jnp.sum(jnp.where(lane == i1, before, 0.0), axis=-1, keepdims=True)
        rank2 = jnp.sum(jnp.where(lane == i2, before, 0.0), axis=-1, keepdims=True)
        base = base + jnp.sum(o, axis=0, keepdims=True)
        route = jnp.where(lane == 0, (i1 - N_GROUPS).astype(F32), 0.0)
        for j, val in enumerate(((i2 - N_GROUPS).astype(F32), gate1, gate2, rank1, rank2), start=1):
            route = jnp.where(lane == j, val, route)
        route_ref[sl, :] = route
    run_ref[...] = base
    count_ref[...] = base


def _trunk(x2d, branches, ob, mem_k, mem_v, w_out, norm_mem_q, w_mq, w_mo, norm_ffn,
           w_router_g, w_router_e, bsz, seq):
    tm = TRUNK_TM
    n = bsz * seq
    nblk = seq // tm
    (o1, l1), (o2, l2), (o3, l3) = branches
    wo = w_out.astype(BF16)
    wr = jnp.concatenate([w_router_g, jnp.moveaxis(w_router_e, 0, 1).reshape(D_MODEL, N_EXPERTS)], axis=1)
    wr = jnp.pad(wr, ((0, 0), (0, LANES - ROUTE_COLS)))
    wr_hi, wr_lo = _split2(wr)
    rows = lambda w: pl.BlockSpec((tm, w), lambda b, i: (b * nblk + i, 0))
    full = lambda shape: pl.BlockSpec(shape, lambda b, i: (0, 0))
    memspec = pl.BlockSpec((None, mem_k.shape[1], mem_k.shape[2]), lambda b, i: (b, 0, 0))
    d1, d4, d16 = DILATIONS
    natural = pl.BlockSpec((None, None, tm, SWA_WIDTH), lambda b, i: (b, 0, i, 0))
    strided = lambda d: pl.BlockSpec((None, d, tm // d, SWA_WIDTH), lambda b, i: (b, 0, i, 0))
    branch_specs = [natural, strided(d4), strided(d16)]
    return pl.pallas_call(
        _trunk_body,
        grid=(bsz, nblk),
        in_specs=[rows(D_MODEL)] + branch_specs + branch_specs + [rows(GDN_WIDTH)]
                 + [full((SWA_WIDTH, D_MODEL)), full((GDN_WIDTH, D_MODEL)), full((1, D_MODEL)),
                    full((D_MODEL, D_MODEL)), memspec, memspec, full((D_MODEL, D_MODEL)),
                    full((1, D_MODEL)), full((D_MODEL, LANES)), full((D_MODEL, LANES))],
        out_specs=[rows(D_MODEL), rows(D_MODEL), rows(LANES), full((1, LANES))],
        out_shape=[jax.ShapeDtypeStruct((n, D_MODEL), F32), jax.ShapeDtypeStruct((n, D_MODEL), F32),
                   jax.ShapeDtypeStruct((n, LANES), F32), jax.ShapeDtypeStruct((1, LANES), F32)],
        scratch_shapes=[pltpu.VMEM((1, LANES), F32), pltpu.VMEM((2 * SWA_WIDTH // LANES, tm, LANES), F32),
                        pltpu.VMEM((2 * SWA_WIDTH // LANES, tm, LANES), F32)],
        compiler_params=_params("arbitrary", "arbitrary"),
        name="trunk",
    )(x2d, o1, o2, o3, l1, l2, l3, ob, wo[:SWA_WIDTH], wo[SWA_WIDTH:], norm_mem_q.reshape(1, D_MODEL),
      w_mq.astype(BF16), mem_k, mem_v, w_mo.astype(BF16), norm_ffn.reshape(1, D_MODEL), wr_hi, wr_lo)


DMA_UNROLL = 8


def _dispatch_body(dest_ref, h_ref, xs_in_ref, xs_ref, sem, *, tm):
    del xs_in_ref

    def issue(t, c):
        for k in range(TOP_K):
            dst = xs_ref.at[pl.ds(dest_ref[0, 0, t * TOP_K + k], 1)]
            pltpu.make_async_copy(h_ref.at[pl.ds(t, 1)], dst, sem).start(priority=k % 2)
        return c

    lax.fori_loop(0, tm, issue, 0, unroll=DMA_UNROLL)
    for _ in range(TOP_K):
        pltpu.make_async_copy(h_ref, xs_ref.at[pl.ds(0, tm)], sem).wait()


def _dispatch(h2, dest, n_slots, tm=512):
    n = h2.shape[0]
    dest3 = dest.reshape(n // tm, 1, tm * TOP_K)
    xs0 = jnp.zeros((n_slots, D_MODEL), F32)
    return pl.pallas_call(
        functools.partial(_dispatch_body, tm=tm),
        grid=(n // tm,),
        in_specs=[pl.BlockSpec((1, 1, tm * TOP_K), lambda i: (i, 0, 0), memory_space=pltpu.SMEM),
                  pl.BlockSpec((tm, D_MODEL), lambda i: (i, 0)),
                  pl.BlockSpec(memory_space=pl.ANY)],
        out_specs=pl.BlockSpec(memory_space=pl.ANY),
        out_shape=jax.ShapeDtypeStruct((n_slots, D_MODEL), F32),
        scratch_shapes=[pltpu.SemaphoreType.DMA(())],
        input_output_aliases={2: 0},
        compiler_params=_params("arbitrary"),
        name="moe_dispatch",
    )(dest3, h2, xs0)


def _expert_body(be_ref, nused_ref, x_ref, wg_ref, wu_ref, wd_ref, y_ref):
    i = pl.program_id(0)

    @pl.when(i < nused_ref[0])
    def _():
        x = x_ref[...].astype(BF16)
        a = _dot(x, wg_ref[...])
        b = _dot(x, wu_ref[...])
        hid = (a * jax.nn.sigmoid(a) * b).astype(BF16)
        y_ref[...] = _dot(hid, wd_ref[...])

    @pl.when(i >= nused_ref[0])
    def _():
        y_ref[...] = jnp.zeros_like(y_ref)


def _experts(xs, block_expert, n_used, w_gate, w_up, w_down):
    n_slots = xs.shape[0]
    n_blocks = n_slots // ROUTE_BLOCK
    grid_spec = pltpu.PrefetchScalarGridSpec(
        num_scalar_prefetch=2,
        grid=(n_blocks,),
        in_specs=[pl.BlockSpec((ROUTE_BLOCK, D_MODEL), lambda i, be, nu: (i, 0)),
                  pl.BlockSpec((None, D_MODEL, D_EXPERT), lambda i, be, nu: (be[i], 0, 0)),
                  pl.BlockSpec((None, D_MODEL, D_EXPERT), lambda i, be, nu: (be[i], 0, 0)),
                  pl.BlockSpec((None, D_EXPERT, D_MODEL), lambda i, be, nu: (be[i], 0, 0))],
        out_specs=pl.BlockSpec((ROUTE_BLOCK, D_MODEL), lambda i, be, nu: (i, 0)),
    )
    return pl.pallas_call(
        _expert_body,
        grid_spec=grid_spec,
        out_shape=jax.ShapeDtypeStruct((n_slots, D_MODEL), F32),
        compiler_params=_params("arbitrary"),
        name="moe_experts",
    )(block_expert, n_used, xs, w_gate, w_up, w_down)


def _combine_body(dest_ref, x_ref, route_ref, g_ref, yb_ref, o_ref, buf_ref, sem, *, tm):
    def issue(t, c):
        for k in range(TOP_K):
            src = yb_ref.at[pl.ds(dest_ref[0, 0, t * TOP_K + k], 1)]
            pltpu.make_async_copy(src, buf_ref.at[k, pl.ds(t, 1)], sem).start(priority=k % 2)
        return c

    lax.fori_loop(0, tm, issue, 0, unroll=DMA_UNROLL)
    for k in range(TOP_K):
        pltpu.make_async_copy(yb_ref.at[pl.ds(0, tm)], buf_ref.at[k], sem).wait()
    route = route_ref[...]
    y = route[:, 2:3] * buf_ref[0] + route[:, 3:4] * buf_ref[1]
    o_ref[...] = _rms(x_ref[...] + y, g_ref[...])


def _combine(x2, route, dest, yb, norm_final, tm=256):
    n = x2.shape[0]
    dest3 = dest.reshape(n // tm, 1, tm * TOP_K)
    return pl.pallas_call(
        functools.partial(_combine_body, tm=tm),
        grid=(n // tm,),
        in_specs=[pl.BlockSpec((1, 1, tm * TOP_K), lambda i: (i, 0, 0), memory_space=pltpu.SMEM),
                  pl.BlockSpec((tm, D_MODEL), lambda i: (i, 0)),
                  pl.BlockSpec((tm, LANES), lambda i: (i, 0)),
                  pl.BlockSpec((1, D_MODEL), lambda i: (0, 0)),
                  pl.BlockSpec(memory_space=pl.ANY)],
        out_specs=pl.BlockSpec((tm, D_MODEL), lambda i: (i, 0)),
        out_shape=jax.ShapeDtypeStruct((n, D_MODEL), F32),
        scratch_shapes=[pltpu.VMEM((TOP_K, tm, D_MODEL), F32), pltpu.SemaphoreType.DMA(())],
        compiler_params=_params("arbitrary"),
        name="moe_combine",
    )(dest3, x2, route, norm_final.reshape(1, D_MODEL), yb)


def _moe(x2, h2, route, counts, w_gate, w_up, w_down, norm_final):
    n = x2.shape[0]
    m_slots = n * TOP_K
    n_blocks = -(-(m_slots + N_EXPERTS * (ROUTE_BLOCK - 1)) // ROUTE_BLOCK)
    cnt = counts[0, N_GROUPS:N_GROUPS + N_EXPERTS].astype(jnp.int32)
    padded = (cnt + ROUTE_BLOCK - 1) // ROUTE_BLOCK * ROUTE_BLOCK
    pad_end = jnp.cumsum(padded)
    seg_start = pad_end - padded
    eid = route[:, 0:TOP_K].astype(jnp.int32)
    rank = route[:, 4:4 + TOP_K].astype(jnp.int32)
    dest = (seg_start[eid] + rank).reshape(-1)
    block_start = jnp.arange(n_blocks, dtype=jnp.int32) * ROUTE_BLOCK
    block_expert = jnp.minimum(jnp.sum(pad_end[None, :] <= block_start[:, None], axis=1), N_EXPERTS - 1)
    block_expert = block_expert.astype(jnp.int32)
    n_used = (pad_end[-1:] // ROUTE_BLOCK).astype(jnp.int32)
    xs = _dispatch(h2, dest, n_blocks * ROUTE_BLOCK)
    yb = _experts(xs, block_expert, n_used, w_gate, w_up, w_down)
    return _combine(x2, route, dest, yb, norm_final)


GDN_TB = 512
GDN_HALO = 16
GDN_GROUP = 4


def _gdn_body(gd_ref, prev_ref, next_ref, abr_ref, cw_ref, alog_ref, dt_ref, *rest, rev, final):
    if final:
        z_ref, oprev_ref, gn_ref, o_ref, xpad_ref, state_ref = rest
    else:
        o_ref, xpad_ref, state_ref = rest
    i = pl.program_id(1)
    nblk = pl.num_programs(1)
    n_chunks = GDN_TB // CHUNK
    n_pairs = GDN_HEADS // 2
    width3 = 3 * GDN_WIDTH

    @pl.when(i == 0)
    def _():
        state_ref[...] = jnp.zeros_like(state_ref)

    blk = (nblk - 1 - i) if rev else i
    zero_halo = jnp.zeros((GDN_HALO, width3), BF16)
    xpad_ref[:GDN_HALO, :] = jnp.where(blk == 0, zero_halo, prev_ref[...])
    xpad_ref[GDN_HALO:GDN_HALO + GDN_TB, :] = gd_ref[...]
    xpad_ref[GDN_HALO + GDN_TB:, :] = jnp.where(blk == nblk - 1, zero_halo, next_ref[...])
    win = CHUNK + 2 * GDN_HALO
    side_taps = [j for j in range(CONV_K) if j != CONV_K // 2]
    sr = lax.broadcasted_iota(jnp.int32, (len(side_taps) * CHUNK, 1), 0)
    sc = lax.broadcasted_iota(jnp.int32, (1, win), 1)
    tap = sr // CHUNK
    tap = tap + jnp.where(tap >= CONV_K // 2, 1, 0)
    shift_sel = jnp.where(sc == sr % CHUNK + tap + (GDN_HALO - CONV_K // 2), 1.0, 0.0).astype(BF16)

    lane = lax.broadcasted_iota(jnp.int32, (1, LANES), 1)
    left = lane < CHUNK
    tok = lane % CHUNK
    row = lax.broadcasted_iota(jnp.int32, (CHUNK, 1), 0)
    causal = (tok >= row) if rev else (tok <= row)
    strict = (tok > row) if rev else (tok < row)
    eye = jnp.where(tok == row, 1.0, 0.0)
    r2 = lax.broadcasted_iota(jnp.int32, (LANES, 1), 0)
    same_head = (r2 // CHUNK) == (lane // CHUNK)
    bd_ones = jnp.where(same_head, 1.0, 0.0).astype(BF16)
    tri_in = (r2 % CHUNK >= tok) if rev else (r2 % CHUNK <= tok)
    tri_bd = jnp.where(jnp.logical_and(same_head, tri_in), 1.0, 0.0).astype(BF16)
    row8 = lax.broadcasted_iota(jnp.int32, (8, 1), 0)

    def blockdiag(x):
        xb = x.astype(BF16)
        zero = jnp.zeros_like(xb)
        return jnp.concatenate([jnp.where(left, xb, zero), jnp.where(left, zero, xb)], axis=0)

    def wide_mm(x, y):
        return _dot(x.astype(BF16), blockdiag(y))

    def head_sums(x):
        return _dot(x.astype(BF16), bd_ones)

    def group(gi, carry):
        g0 = (n_chunks // GDN_GROUP - 1 - gi) if rev else gi
        chunks = [g0 * GDN_GROUP + ((GDN_GROUP - 1 - j) if rev else j) for j in range(GDN_GROUP)]
        units = [(ci, p) for ci in range(GDN_GROUP) for p in range(n_pairs)]
        r0 = [pl.multiple_of(c * CHUNK, CHUNK) for c in chunks]

        gates, gc_rows, gl_rows = [], [], []
        for c in chunks:
            graw = abr_ref[c]
            xg = graw + dt_ref[...]
            softplus = jnp.maximum(xg, 0.0) + jnp.log(1.0 + jnp.exp(-jnp.abs(xg)))
            g = jnp.where(row8 < n_pairs, jax.nn.sigmoid(graw), -jnp.exp(alog_ref[...]) * softplus)
            g3 = _split3(g)
            gates.append(g)
            gc_rows.append(sum(_dot(t, tri_bd) for t in g3))
            gl_rows.append(sum(_dot(t, bd_ones) for t in g3))

        shifted = [_dot(shift_sel, xpad_ref[pl.ds(r0[ci], win), :]) for ci in range(GDN_GROUP)]

        def conv(ci, p, col0):
            cols = slice(col0 + p * LANES, col0 + (p + 1) * LANES)
            centre = xpad_ref[pl.ds(r0[ci] + GDN_HALO, CHUNK), cols].astype(F32)
            acc = centre * cw_ref[CONV_K // 2:CONV_K // 2 + 1, cols]
            for i, j in enumerate(side_taps):
                acc = acc + shifted[ci][i * CHUNK:(i + 1) * CHUNK, cols] * cw_ref[j:j + 1, cols]
            return acc * jax.nn.sigmoid(acc)

        qkv = {u: [conv(*u, col0) for col0 in (0, GDN_WIDTH, 2 * GDN_WIDTH)] for u in units}
        ss = {u: head_sums(jnp.concatenate([qkv[u][0] * qkv[u][0], qkv[u][1] * qkv[u][1]], axis=0))
              for u in units}
        prep = {}
        for (ci, p) in units:
            beta_r = gates[ci][p:p + 1]
            g_r = gates[ci][n_pairs + p:n_pairs + p + 1]
            lhs = jnp.concatenate([jnp.where(causal, g_r, 0.0), eye * beta_r], axis=0)
            prep[(ci, p)] = sum(_dot(t, bd_ones) for t in _split2(lhs))

        kq, kn_bd, e_gc, k_upd, decay_b = {}, {}, {}, {}, {}
        for u in units:
            ci, p = u
            qn = qkv[u][0] * lax.rsqrt(ss[u][:CHUNK] + EPS) * (HEAD_DIM ** -0.5)
            kn = qkv[u][1] * lax.rsqrt(ss[u][CHUNK:] + EPS)
            kq[u] = jnp.concatenate([kn, qn], axis=0).astype(BF16)
            kn_bd[u] = blockdiag(kn)
            beta_r = gates[ci][p:p + 1]
            gc_r = gc_rows[ci][n_pairs + p:n_pairs + p + 1]
            gl_r = gl_rows[ci][n_pairs + p:n_pairs + p + 1]
            gc_b, beta_b = prep[u][:CHUNK], prep[u][CHUNK:]
            decay_b[u] = jnp.exp(jnp.where(causal, gc_b - gc_r, NEG_BIG)) * beta_r
            e_gc[u] = jnp.exp(gc_b)
            k_upd[u] = (kn * (jnp.exp(gl_r - gc_b) * beta_b)).astype(BF16)

        kk_qk = {u: _dot_nt(kq[u], kn_bd[u]) for u in units}
        lpow = {u: jnp.where(strict, kk_qk[u][:CHUNK] * decay_b[u], 0.0) for u in units}
        amat = {u: kk_qk[u][CHUNK:] * decay_b[u] for u in units}
        tinv = {u: eye - lpow[u] for u in units}
        for _ in range(5):
            lpow = {u: wide_mm(lpow[u], lpow[u]) for u in units}
            tinv = {u: tinv[u] + wide_mm(tinv[u], lpow[u]) for u in units}

        for ci in range(GDN_GROUP):
            us = [(ci, p) for p in range(n_pairs)]
            state = {u: state_ref[u[1]] for u in us}
            pq = {u: _dot(kq[u], state[u].astype(BF16)) for u in us}
            vhat = {u: wide_mm(tinv[u], qkv[u][2] - e_gc[u] * pq[u][:CHUNK]) for u in us}
            upd = {u: _dot_tn(k_upd[u], vhat[u].astype(BF16)) for u in us}
            for u in us:
                gl_r = gl_rows[ci][n_pairs + u[1]:n_pairs + u[1] + 1]
                state_ref[u[1]] = state[u] * jnp.exp(gl_r) + jnp.where(same_head, upd[u], 0.0)
            o = {u: e_gc[u] * pq[u][CHUNK:] + wide_mm(amat[u], vhat[u]) for u in us}
            for u in us:
                cs = slice(u[1] * LANES, (u[1] + 1) * LANES)
                rows = pl.ds(r0[ci], CHUNK)
                if final:
                    ob = o[u] + oprev_ref[rows, cs]
                    ms = head_sums(ob * ob) * (1.0 / HEAD_DIM)
                    zz = z_ref[rows, cs].astype(F32)
                    out = ob * lax.rsqrt(ms + EPS) * gn_ref[:, cs] * (zz * jax.nn.sigmoid(zz))
                    o_ref[rows, cs] = out.astype(o_ref.dtype)
                else:
                    o_ref[rows, cs] = o[u]
        return carry

    lax.fori_loop(0, n_chunks // GDN_GROUP, group, 0)


def _gdn_pass(gd3, abr, conv_w, alog_row, dt_row, rev, final_args=None):
    bsz, seq, width3 = gd3.shape
    nblk = seq // GDN_TB
    hpb = GDN_TB // GDN_HALO
    blk = (lambda i: nblk - 1 - i) if rev else (lambda i: i)
    main = lambda w: pl.BlockSpec((None, GDN_TB, w), lambda b, i: (b, blk(i), 0))
    full = lambda shape: pl.BlockSpec(shape, lambda b, i: (0,) * len(shape))
    in_specs = [main(width3),
                pl.BlockSpec((None, GDN_HALO, width3), lambda b, i: (b, jnp.maximum(blk(i) * hpb - 1, 0), 0)),
                pl.BlockSpec((None, GDN_HALO, width3),
                             lambda b, i: (b, jnp.minimum((blk(i) + 1) * hpb, seq // GDN_HALO - 1), 0)),
                pl.BlockSpec((None, GDN_TB // CHUNK, 8, LANES), lambda b, i: (b, blk(i), 0, 0)),
                full((CONV_K, width3)), full((8, LANES)), full((8, LANES))]
    args = [gd3, gd3, gd3, abr, conv_w, alog_row, dt_row]
    final = final_args is not None
    if final:
        z3, o_prev, gn = final_args
        in_specs += [main(GDN_WIDTH), main(GDN_WIDTH), full((1, GDN_WIDTH))]
        args += [z3, o_prev, gn]
    return pl.pallas_call(
        functools.partial(_gdn_body, rev=rev, final=final),
        grid=(bsz, nblk),
        in_specs=in_specs,
        out_specs=main(GDN_WIDTH),
        out_shape=jax.ShapeDtypeStruct((bsz, seq, GDN_WIDTH), BF16 if final else F32),
        scratch_shapes=[pltpu.VMEM((GDN_TB + 2 * GDN_HALO, width3), BF16),
                        pltpu.VMEM((GDN_HEADS // 2, LANES, LANES), F32)],
        compiler_params=_params("parallel", "arbitrary"),
        name="gdn_bwd" if rev else "gdn_fwd",
    )(*args)


def _pair_rows(x, bsz, seq):
    x = x.reshape(bsz, seq // CHUNK, CHUNK, GDN_HEADS // 2, 2)
    return x.transpose(0, 1, 3, 4, 2).reshape(bsz, seq // CHUNK, GDN_HEADS // 2, LANES)


def _gdn(gd, z, ab, conv_w, a_log, dt_bias, gdn_norm, bsz, seq):
    gd3 = gd.reshape(bsz, seq, 3 * GDN_WIDTH)
    z3 = z.reshape(bsz, seq, GDN_WIDTH)
    gn = jnp.tile(gdn_norm, GDN_HEADS).reshape(1, GDN_WIDTH)
    out = None
    for d in range(2):
        beta = _pair_rows(ab[:, d * GDN_HEADS:(d + 1) * GDN_HEADS], bsz, seq)
        araw = _pair_rows(ab[:, (2 + d) * GDN_HEADS:(3 + d) * GDN_HEADS], bsz, seq)
        abr = jnp.concatenate([beta, araw], axis=2)
        per_head = lambda v: jnp.concatenate(
            [jnp.zeros((GDN_HEADS // 2, LANES), F32),
             jnp.repeat(v[d], CHUNK).reshape(GDN_HEADS // 2, LANES)], axis=0)
        final_args = None if d == 0 else (z3, out, gn)
        out = _gdn_pass(gd3, abr, conv_w, per_head(a_log), per_head(dt_bias), rev=(d == 1), final_args=final_args)
    return out.reshape(bsz * seq, GDN_WIDTH)


def _gdn_scan_jax(q, k, v, g, beta):
    B, T, H, dk = q.shape
    C = CHUNK
    N = T // C
    chunks = lambda t: t.reshape(B, N, C, H, -1).transpose(1, 0, 3, 2, 4)
    qc, kc, vc = chunks(q), chunks(k), chunks(v)
    gc = jnp.cumsum(g.reshape(B, N, C, H).transpose(1, 0, 3, 2), axis=-1)
    bc = beta.reshape(B, N, C, H).transpose(1, 0, 3, 2)[..., None]
    lower = jnp.tril(jnp.ones((C, C), bool))
    strict = jnp.tril(jnp.ones((C, C), bool), -1)
    decay = jnp.exp(jnp.where(lower, gc[..., :, None] - gc[..., None, :], -jnp.inf))
    kbeta = kc * bc
    lmat = jnp.where(strict, jnp.einsum('nbhid,nbhjd->nbhij', kbeta, kc) * decay, 0.0)
    solve = lambda rhs: lax.linalg.triangular_solve(lmat, rhs, left_side=True, lower=True, unit_diagonal=True)
    u = solve(vc * bc)
    w = solve(kbeta * jnp.exp(gc)[..., None])
    a_qk = jnp.einsum('nbhid,nbhjd->nbhij', qc, kc) * decay

    def step(state, inp):
        q_i, k_i, u_i, w_i, g_i, a_i = inp
        v_new = u_i - jnp.einsum('bhck,bhkv->bhcv', w_i, state)
        o_i = (jnp.einsum('bhck,bhkv->bhcv', q_i * jnp.exp(g_i)[..., None], state)
               + jnp.einsum('bhij,bhjv->bhiv', a_i, v_new))
        g_last = g_i[..., -1:]
        state = (state * jnp.exp(g_last)[..., None]
                 + jnp.einsum('bhck,bhcv->bhkv', k_i * jnp.exp(g_last - g_i)[..., None], v_new))
        return state, o_i

    s0 = jnp.zeros((B, H, dk, dk), F32)
    _, o = lax.scan(step, s0, (qc, kc, u, w, gc, a_qk))
    return o.transpose(1, 0, 3, 2, 4).reshape(B, T, H, dk)


def _gdn_jax(gd, z, ab, conv_w, a_log, dt_bias, gdn_norm, bsz, seq):
    c = lax.conv_general_dilated(
        gd.astype(F32).reshape(bsz, seq, 3 * GDN_WIDTH), conv_w[:, None, :], window_strides=(1,),
        padding=[(CONV_K // 2, CONV_K // 2)], dimension_numbers=('NWC', 'WIO', 'NWC'),
        feature_group_count=3 * GDN_WIDTH)
    c = jax.nn.silu(c)
    l2 = lambda t: t * lax.rsqrt(jnp.sum(t * t, axis=-1, keepdims=True) + EPS)
    qb, kb, vb = [t.reshape(bsz, seq, GDN_HEADS, HEAD_DIM) for t in jnp.split(c, 3, axis=-1)]
    qb = l2(qb) * (HEAD_DIM ** -0.5)
    kb = l2(kb)
    ab4 = ab[:, :4 * GDN_HEADS].reshape(bsz, seq, 4, GDN_HEADS)
    beta = jax.nn.sigmoid(ab4[:, :, 0:2])
    g = -jnp.exp(a_log) * jax.nn.softplus(ab4[:, :, 2:4] + dt_bias)
    fwd = _gdn_scan_jax(qb, kb, vb, g[:, :, 0], beta[:, :, 0])
    flip = lambda t: t[:, ::-1]
    bwd = flip(_gdn_scan_jax(flip(qb), flip(kb), flip(vb), flip(g[:, :, 1]), flip(beta[:, :, 1])))
    ob = fwd + bwd
    ob = (ob * lax.rsqrt(jnp.mean(ob * ob, axis=-1, keepdims=True) + EPS) * gdn_norm
          * jax.nn.silu(z.astype(F32).reshape(bsz, seq, GDN_HEADS, HEAD_DIM)))
    return ob.reshape(bsz * seq, GDN_WIDTH).astype(BF16)


def _encoder(x, mem, norm_mix, w_in, conv_w, a_log, dt_bias, gdn_norm, w_out, norm_mem_q, norm_mem_kv,
             w_mq, w_mkv, w_mo, norm_ffn, w_router_g, w_router_e, w_gate, w_up, w_down, norm_final):
    bsz, seq, _ = x.shape
    x2d = x.reshape(bsz * seq, D_MODEL)
    qkv1, qkv4, qkv16, gd, z, ab = _inproj(x2d, norm_mix, w_in, bsz, seq)
    slopes = jnp.exp2(-8.0 * jnp.arange(1, SWA_HEADS + 1, dtype=F32) / SWA_HEADS)
    branches = [_attn_branch(qkv, slopes, d) for qkv, d in zip((qkv1, qkv4, qkv16), DILATIONS)]
    ob = _gdn(gd, z, ab, conv_w, a_log, dt_bias, gdn_norm, bsz, seq)
    mem_k, mem_v = _memkv(mem.reshape(-1, D_MODEL), norm_mem_kv, w_mkv)
    mem_k = mem_k.reshape(bsz, -1, MEM_HEADS * MEM_HEAD_DIM)
    mem_v = mem_v.reshape(bsz, -1, MEM_HEADS * MEM_HEAD_DIM)
    x2, h2, route, counts = _trunk(x2d, branches, ob, mem_k, mem_v, w_out, norm_mem_q, w_mq, w_mo,
                                   norm_ffn, w_router_g, w_router_e, bsz, seq)
    y = _moe(x2, h2, route, counts, w_gate, w_up, w_down, norm_final)
    return y.reshape(bsz, seq, D_MODEL)


def kernel(x_prompt, x_sample, mem_prompt, mem_sample, norm_mix, w_in, conv_w, a_log, dt_bias, gdn_norm,
           w_out, norm_mem_q, norm_mem_kv, w_mq, w_mkv, w_mo, norm_ffn, w_router_g, w_router_e,
           w_gate, w_up, w_down, norm_final):
    p = dict(norm_mix=norm_mix[0], w_in=w_in[0], conv_w=conv_w[0], a_log=a_log[0], dt_bias=dt_bias[0],
             gdn_norm=gdn_norm[0], w_out=w_out[0], norm_mem_q=norm_mem_q[0], norm_mem_kv=norm_mem_kv[0],
             w_mq=w_mq[0], w_mkv=w_mkv[0], w_mo=w_mo[0], norm_ffn=norm_ffn[0], w_router_g=w_router_g[0],
             w_router_e=w_router_e[0], w_gate=w_gate[0].astype(BF16), w_up=w_up[0].astype(BF16),
             w_down=w_down[0].astype(BF16), norm_final=norm_final)
    return (_encoder(x_prompt, mem_prompt, **p), _encoder(x_sample, mem_sample, **p))
```

```python
import functools

import jax
import jax.numpy as jnp
import numpy as np
from jax import lax
from jax.experimental import pallas as pl
from jax.experimental.pallas import tpu as pltpu

F32 = jnp.float32
BF16 = jnp.bfloat16

D_MODEL = 1024
HEAD_DIM = 64
SWA_HEADS = 8
GDN_HEADS = 8
SWA_WIDTH = SWA_HEADS * HEAD_DIM
GDN_WIDTH = GDN_HEADS * HEAD_DIM
DILATIONS = (1, 4, 16)
ATT_W = 64
CONV_K = 5
CHUNK = 64
MEM_HEADS = 4
MEM_HEAD_DIM = 256
N_GROUPS = 4
EXPERTS_PER_GROUP = 8
N_EXPERTS = 32
TOP_K = 2
D_EXPERT = 512
ROUTE_BLOCK = 256
EPS = 1e-6

LANES = 128
VMEM_LIMIT = 56 * 1024 * 1024
NEG_BIG = -1e30


def _dot(a, b):
    return jnp.dot(a, b, preferred_element_type=F32)


def _dot_nt(a, b):
    return lax.dot_general(a, b, (((1,), (1,)), ((), ())), preferred_element_type=F32)


def _dot_tn(a, b):
    return lax.dot_general(a, b, (((0,), (0,)), ((), ())), preferred_element_type=F32)


def _split2(x):
    hi = x.astype(BF16)
    lo = (x - hi.astype(F32)).astype(BF16)
    return hi, lo


def _split3(x):
    hi = x.astype(BF16)
    r = x - hi.astype(F32)
    mid = r.astype(BF16)
    lo = (r - mid.astype(F32)).astype(BF16)
    return hi, mid, lo


def _rms(x, g):
    return x * lax.rsqrt(jnp.mean(x * x, axis=-1, keepdims=True) + EPS) * g


def _params(*sem):
    return pltpu.CompilerParams(dimension_semantics=sem, vmem_limit_bytes=VMEM_LIMIT)


def _inproj_body(x_ref, g_ref, wa_ref, wgd_ref, wz_ref, wab_hi_ref, wab_lo_ref,
                 qkv1_ref, qkv4_ref, qkv16_ref, gd_ref, z_ref, ab_ref, stage_ref, *, tm):
    h = _rms(x_ref[...], g_ref[...])
    h_hi, h_lo = _split2(h)
    qkv = _dot(h_hi, wa_ref[...])
    qkv1_ref[...] = qkv.astype(BF16)
    for c in range(qkv.shape[1] // LANES):
        cols = slice(c * LANES, (c + 1) * LANES)
        stage_ref[c] = qkv[:, cols]
        for dil, out_ref in ((DILATIONS[1], qkv4_ref), (DILATIONS[2], qkv16_ref)):
            for r in range(dil):
                out_ref[r, :, cols] = stage_ref[c, pl.ds(r, tm // dil, stride=dil), :].astype(BF16)
    gd_ref[...] = _dot(h_hi, wgd_ref[...]).astype(BF16)
    z_ref[...] = _dot(h_hi, wz_ref[...]).astype(BF16)
    wab_hi = wab_hi_ref[...]
    ab = _dot(h_hi, wab_hi) + _dot(h_lo, wab_hi) + _dot(h_hi, wab_lo_ref[...])
    ab_ref[...] = ab


def _inproj(x2, norm_mix, w_in, bsz, seq, tm=512):
    n = x2.shape[0]
    nblk = seq // tm
    width = 3 * SWA_WIDTH
    wa = w_in[:, :3 * SWA_WIDTH].astype(BF16)
    wgd = w_in[:, 3 * SWA_WIDTH:3 * SWA_WIDTH + 3 * GDN_WIDTH].astype(BF16)
    wz = w_in[:, 3 * SWA_WIDTH + 3 * GDN_WIDTH:3 * SWA_WIDTH + 4 * GDN_WIDTH].astype(BF16)
    wab = jnp.pad(w_in[:, 3 * SWA_WIDTH + 4 * GDN_WIDTH:], ((0, 0), (0, LANES - 4 * GDN_HEADS)))
    wab_hi, wab_lo = _split2(wab)
    full = lambda shape: pl.BlockSpec(shape, lambda b, i: (0, 0))
    rows = lambda w: pl.BlockSpec((tm, w), lambda b, i: (b * nblk + i, 0))
    strided = lambda d: pl.BlockSpec((None, d, tm // d, width), lambda b, i: (b, 0, i, 0))
    d1, d4, d16 = DILATIONS
    return pl.pallas_call(
        functools.partial(_inproj_body, tm=tm),
        grid=(bsz, nblk),
        in_specs=[rows(D_MODEL), full((1, D_MODEL)), full(wa.shape), full(wgd.shape), full(wz.shape),
                  full(wab_hi.shape), full(wab_lo.shape)],
        out_specs=[pl.BlockSpec((None, None, tm, width), lambda b, i: (b, 0, i, 0)), strided(d4), strided(d16),
                   rows(3 * GDN_WIDTH), rows(GDN_WIDTH), rows(LANES)],
        out_shape=[jax.ShapeDtypeStruct((bsz, d, seq // d, width), BF16) for d in (d1, d4, d16)]
                  + [jax.ShapeDtypeStruct((n, 3 * GDN_WIDTH), BF16),
                     jax.ShapeDtypeStruct((n, GDN_WIDTH), BF16),
                     jax.ShapeDtypeStruct((n, LANES), F32)],
        scratch_shapes=[pltpu.VMEM((width // LANES, tm, LANES), F32)],
        compiler_params=_params("parallel", "parallel"),
        name="inproj",
    )(x2, norm_mix.reshape(1, D_MODEL), wa, wgd, wz, wab_hi, wab_lo)


ATT_QB = 128
ATT_KB = ATT_QB + 2 * ATT_W
ATT_GROUP = 4


def _attn_bias(slopes, dil):
    row = jnp.arange(ATT_QB)[:, None]
    col = jnp.arange(ATT_KB)[None, :]
    rel = jnp.stack([jnp.abs(col - row - var * ATT_W) for var in range(3)])
    dist = (rel * dil).astype(F32)
    bias = jnp.where(rel <= ATT_W, -slopes[:, None, None, None] * dist, NEG_BIG)
    bias = bias.reshape(SWA_HEADS // 2, 2, 3, ATT_QB, ATT_KB).transpose(0, 2, 1, 3, 4)
    return bias.reshape(SWA_HEADS // 2, 6, ATT_QB, ATT_KB)


def _attn_body(bias_ref, q_ref, k_ref, v_ref, o_ref, lse_ref, *, n_qb, seq_l, n_pw):
    lb = pl.program_id(3)
    lane = lax.broadcasted_iota(jnp.int32, (1, LANES), 1)
    left = lane < HEAD_DIM
    group = min(ATT_GROUP, n_qb)

    def qgroup(gi, carry):
        rows, var, kb, vb, q, pw_of, cols = [], [], [], [], [], [], []
        for j in range(group):
            qi = gi * group + j
            n0 = (lb * n_qb + qi) * ATT_QB
            kstart = pl.multiple_of(jnp.clip(n0 - ATT_W, 0, seq_l - ATT_KB), ATT_W)
            for pw in range(n_pw):
                cs = slice(pw * LANES, (pw + 1) * LANES)
                var.append((n0 - kstart) // ATT_W)
                rows.append(pl.ds(pl.multiple_of(qi * ATT_QB, ATT_QB), ATT_QB))
                q.append(q_ref[rows[-1], cs] * jnp.asarray(HEAD_DIM ** -0.5, BF16))
                kb.append(k_ref[pl.ds(kstart, ATT_KB), cs])
                vb.append(v_ref[pl.ds(kstart, ATT_KB), cs])
                pw_of.append(pw)
                cols.append(cs)
        blocks = range(group * n_pw)
        units = [(j, h) for j in blocks for h in range(2)]
        mine = [left, jnp.logical_not(left)]
        s = {(j, h): _dot_nt(jnp.where(mine[h], q[j], jnp.zeros_like(q[j])), kb[j])
                     + bias_ref[pw_of[j], var[j] * 2 + h] for (j, h) in units}
        m = {u: jnp.max(s[u], axis=-1, keepdims=True) for u in units}
        p = {u: jnp.exp(s[u] - m[u]).astype(BF16) for u in units}
        acc = {(j, h): _dot(p[(j, h)], jnp.where(mine[h], vb[j], jnp.ones_like(vb[j]))) for (j, h) in units}
        for j in blocks:
            num = jnp.where(left, acc[(j, 0)], acc[(j, 1)])
            den = pltpu.roll(jnp.where(left, acc[(j, 1)], acc[(j, 0)]), HEAD_DIM, 1)
            mx = jnp.where(left, m[(j, 0)], m[(j, 1)])
            o_ref[rows[j], cols[j]] = (num / den).astype(BF16)
            lse_ref[rows[j], cols[j]] = mx + jnp.log(den)
        return carry

    lax.fori_loop(0, n_qb // group, qgroup, 0)


def _attn_branch(qkv, slopes, dil):
    bsz, _, seq_l, _ = qkv.shape
    lblk = min(2048, seq_l)
    n_qb = lblk // ATT_QB
    n_pw = 1 if n_qb >= ATT_GROUP else 2
    n_steps = SWA_HEADS // 2 // n_pw
    width = n_pw * LANES
    qspec = pl.BlockSpec((None, None, lblk, width), lambda p, b, r, l: (b, r, l, p))
    kspec = pl.BlockSpec((None, None, seq_l, width), lambda p, b, r, l: (b, r, 0, n_steps + p))
    vspec = pl.BlockSpec((None, None, seq_l, width), lambda p, b, r, l: (b, r, 0, 2 * n_steps + p))
    ospec = pl.BlockSpec((None, None, lblk, width), lambda p, b, r, l: (b, r, l, p))
    bspec = pl.BlockSpec((n_pw, 6, ATT_QB, ATT_KB), lambda p, b, r, l: (p, 0, 0, 0))
    return pl.pallas_call(
        functools.partial(_attn_body, n_qb=n_qb, seq_l=seq_l, n_pw=n_pw),
        grid=(n_steps, bsz, dil, seq_l // lblk),
        in_specs=[bspec, qspec, kspec, vspec],
        out_specs=[ospec, ospec],
        out_shape=[jax.ShapeDtypeStruct((bsz, dil, seq_l, SWA_WIDTH), BF16),
                   jax.ShapeDtypeStruct((bsz, dil, seq_l, SWA_WIDTH), F32)],
        compiler_params=_params("parallel", "parallel", "parallel", "arbitrary"),
        name=f"dilated_attn_d{dil}",
    )(_attn_bias(slopes, dil), qkv, qkv, qkv)


def _memkv_body(m_ref, g_ref, w_ref, k_ref, v_ref):
    h = _rms(m_ref[...], g_ref[...]).astype(BF16)
    kv = _dot(h, w_ref[...])
    width = MEM_HEADS * MEM_HEAD_DIM
    k_ref[...] = (kv[:, :width] * (MEM_HEAD_DIM ** -0.5)).astype(BF16)
    v_ref[...] = kv[:, width:].astype(BF16)


def _memkv(mem2, norm_kv, w_mkv, tm=256):
    n = mem2.shape[0]
    width = MEM_HEADS * MEM_HEAD_DIM
    return pl.pallas_call(
        _memkv_body,
        grid=(n // tm,),
        in_specs=[pl.BlockSpec((tm, D_MODEL), lambda i: (i, 0)),
                  pl.BlockSpec((1, D_MODEL), lambda i: (0, 0)),
                  pl.BlockSpec((D_MODEL, 2 * width), lambda i: (0, 0))],
        out_specs=[pl.BlockSpec((tm, width), lambda i: (i, 0))] * 2,
        out_shape=[jax.ShapeDtypeStruct((n, width), BF16)] * 2,
        compiler_params=_params("parallel"),
        name="mem_kv",
    )(mem2, norm_kv.reshape(1, D_MODEL), w_mkv.astype(BF16))


ROUTE_COLS = N_GROUPS + N_EXPERTS


TRUNK_TM = 512
TRUNK_SUB = 256


def _trunk_body(x_ref, o1_ref, o2_ref, o3_ref, l1_ref, l2_ref, l3_ref, ob_ref,
                wo_a_ref, wo_b_ref, gq_ref, wq_ref, k_ref, v_ref, wmo_ref,
                gf_ref, wr_hi_ref, wr_lo_ref,
                x2_ref, h2_ref, route_ref, count_ref, run_ref, obuf_ref, lbuf_ref):
    first = jnp.logical_and(pl.program_id(0) == 0, pl.program_id(1) == 0)

    @pl.when(first)
    def _():
        run_ref[...] = jnp.zeros_like(run_ref)

    n_cb = SWA_WIDTH // LANES
    for bi, (o_ref, l_ref) in enumerate(((o2_ref, l2_ref), (o3_ref, l3_ref))):
        dil = DILATIONS[bi + 1]
        for r in range(dil):
            rows = pl.ds(r, TRUNK_TM // dil, stride=dil)
            o_r = o_ref[r].astype(F32)
            l_r = l_ref[r]
            for c in range(n_cb):
                obuf_ref[bi * n_cb + c, rows, :] = o_r[:, c * LANES:(c + 1) * LANES]
                lbuf_ref[bi * n_cb + c, rows, :] = l_r[:, c * LANES:(c + 1) * LANES]

    slabs = [slice(j * TRUNK_SUB, (j + 1) * TRUNK_SUB) for j in range(TRUNK_TM // TRUNK_SUB)]

    def token_major(buf_ref, bi, sl):
        return jnp.concatenate([buf_ref[bi * n_cb + c, sl, :] for c in range(n_cb)], axis=-1)

    def merge(sl):
        l1, l2, l3 = l1_ref[sl, :], token_major(lbuf_ref, 0, sl), token_major(lbuf_ref, 1, sl)
        mx = jnp.maximum(jnp.maximum(l1, l2), l3)
        e1, e2, e3 = jnp.exp(l1 - mx), jnp.exp(l2 - mx), jnp.exp(l3 - mx)
        ya = (e1 * o1_ref[sl, :].astype(F32) + e2 * token_major(obuf_ref, 0, sl)
              + e3 * token_major(obuf_ref, 1, sl))
        return (ya / (e1 + e2 + e3)).astype(BF16)

    ya = [merge(sl) for sl in slabs]
    x1 = [x_ref[sl, :] + _dot(y, wo_a_ref[...]) + _dot(ob_ref[sl, :], wo_b_ref[...]) for y, sl in zip(ya, slabs)]

    q = [_dot(_rms(x, gq_ref[...]).astype(BF16), wq_ref[...]).astype(BF16) for x in x1]
    cols = [slice(h * MEM_HEAD_DIM, (h + 1) * MEM_HEAD_DIM) for h in range(MEM_HEADS)]
    units = [(j, cs) for j in range(len(slabs)) for cs in cols]
    s = [_dot_nt(q[j][:, cs], k_ref[:, cs]) for j, cs in units]
    p = [jnp.exp(sh - jnp.max(sh, axis=-1, keepdims=True)) for sh in s]
    den = [jnp.sum(ph, axis=-1, keepdims=True) for ph in p]
    pv = [_dot(ph.astype(BF16), v_ref[:, cs]) for ph, (j, cs) in zip(p, units)]
    heads = [(a / d).astype(BF16) for a, d in zip(pv, den)]
    nh = MEM_HEADS
    x2 = [x + _dot(jnp.concatenate(heads[j * nh:(j + 1) * nh], axis=-1), wmo_ref[...]) for j, x in enumerate(x1)]
    h2 = [_rms(x, gf_ref[...]) for x in x2]
    for sl, x, h in zip(slabs, x2, h2):
        x2_ref[sl, :] = x
        h2_ref[sl, :] = _pack_rows(h)

    wr_hi = wr_hi_ref[...]

    def route_logits(h):
        h_hi, h_lo = _split2(h)
        return _dot(h_hi, wr_hi) + _dot(h_lo, wr_hi) + _dot(h_hi, wr_lo_ref[...])

    logits_all = [route_logits(h) for h in h2]
    lane = lax.broadcasted_iota(jnp.int32, (TRUNK_SUB, LANES), 1)
    big = jnp.int32(LANES)
    is_g = lane < N_GROUPS

    def top2(logits):
        lg = jnp.where(is_g, logits, NEG_BIG)
        mg = jnp.max(lg, axis=-1, keepdims=True)
        g_idx = jnp.min(jnp.where(jnp.logical_and(is_g, lg == mg), lane, big), axis=-1, keepdims=True)
        g_w = 1.0 / jnp.sum(jnp.exp(lg - mg), axis=-1, keepdims=True)
        lo_lane = N_GROUPS + g_idx * EXPERTS_PER_GROUP
        in_grp = jnp.logical_and(lane >= lo_lane, lane < lo_lane + EXPERTS_PER_GROUP)
        le = jnp.where(in_grp, logits, NEG_BIG)
        m1 = jnp.max(le, axis=-1, keepdims=True)
        i1 = jnp.min(jnp.where(jnp.logical_and(in_grp, le == m1), lane, big), axis=-1, keepdims=True)
        le2 = jnp.where(lane == i1, NEG_BIG, le)
        m2 = jnp.max(le2, axis=-1, keepdims=True)
        i2 = jnp.min(jnp.where(jnp.logical_and(in_grp, le2 == m2), lane, big), axis=-1, keepdims=True)
        r2 = jnp.exp(m2 - m1)
        return i1, i2, g_w / (1.0 + r2), g_w * r2 / (1.0 + r2)

    picks = [top2(lg) for lg in logits_all]

    r_i = lax.broadcasted_iota(jnp.int32, (TRUNK_SUB, TRUNK_SUB), 0)
    c_i = lax.broadcasted_iota(jnp.int32, (TRUNK_SUB, TRUNK_SUB), 1)
    tri = jnp.where(c_i < r_i, 1.0, 0.0).astype(BF16)
    oh = [jnp.where(jnp.logical_or(lane == i1, lane == i2), 1.0, 0.0) for i1, i2, _, _ in picks]
    prefix = [_dot(tri, o.astype(BF16)) for o in oh]
    base = run_ref[...]
    for sl, (i1, i2, gate1, gate2), o, pre in zip(slabs, picks, oh, prefix):
        before = pre + base
        rank1 = jnp.sum(jnp.where(lane == i1, before, 0.0), axis=-1, keepdims=True)
        rank2 = jnp.sum(jnp.where(lane == i2, before, 0.0), axis=-1, keepdims=True)
        base = base + jnp.sum(o, axis=0, keepdims=True)
        route = jnp.where(lane == 0, (i1 - N_GROUPS).astype(F32), 0.0)
        for j, val in enumerate(((i2 - N_GROUPS).astype(F32), gate1, gate2, rank1, rank2), start=1):
            route = jnp.where(lane == j, val, route)
        route_ref[sl, :] = route
    run_ref[...] = base
    count_ref[...] = base


def _trunk(x2d, branches, ob, mem_k, mem_v, w_out, norm_mem_q, w_mq, w_mo, norm_ffn,
           w_router_g, w_router_e, bsz, seq):
    tm = TRUNK_TM
    n = bsz * seq
    nblk = seq // tm
    (o1, l1), (o2, l2), (o3, l3) = branches
    wo = w_out.astype(BF16)
    wr = jnp.concatenate([w_router_g, jnp.moveaxis(w_router_e, 0, 1).reshape(D_MODEL, N_EXPERTS)], axis=1)
    wr = jnp.pad(wr, ((0, 0), (0, LANES - ROUTE_COLS)))
    wr_hi, wr_lo = _split2(wr)
    rows = lambda w: pl.BlockSpec((tm, w), lambda b, i: (b * nblk + i, 0))
    full = lambda shape: pl.BlockSpec(shape, lambda b, i: (0, 0))
    memspec = pl.BlockSpec((None, mem_k.shape[1], mem_k.shape[2]), lambda b, i: (b, 0, 0))
    d1, d4, d16 = DILATIONS
    natural = pl.BlockSpec((None, None, tm, SWA_WIDTH), lambda b, i: (b, 0, i, 0))
    strided = lambda d: pl.BlockSpec((None, d, tm // d, SWA_WIDTH), lambda b, i: (b, 0, i, 0))
    branch_specs = [natural, strided(d4), strided(d16)]
    return pl.pallas_call(
        _trunk_body,
        grid=(bsz, nblk),
        in_specs=[rows(D_MODEL)] + branch_specs + branch_specs + [rows(GDN_WIDTH)]
                 + [full((SWA_WIDTH, D_MODEL)), full((GDN_WIDTH, D_MODEL)), full((1, D_MODEL)),
                    full((D_MODEL, D_MODEL)), memspec, memspec, full((D_MODEL, D_MODEL)),
                    full((1, D_MODEL)), full((D_MODEL, LANES)), full((D_MODEL, LANES))],
        out_specs=[rows(D_MODEL), rows(PACKED), rows(LANES), full((1, LANES))],
        out_shape=[jax.ShapeDtypeStruct((n, D_MODEL), F32), jax.ShapeDtypeStruct((n, PACKED), jnp.uint32),
                   jax.ShapeDtypeStruct((n, LANES), F32), jax.ShapeDtypeStruct((1, LANES), F32)],
        scratch_shapes=[pltpu.VMEM((1, LANES), F32), pltpu.VMEM((2 * SWA_WIDTH // LANES, tm, LANES), F32),
                        pltpu.VMEM((2 * SWA_WIDTH // LANES, tm, LANES), F32)],
        compiler_params=_params("arbitrary", "arbitrary"),
        name="trunk",
    )(x2d, o1, o2, o3, l1, l2, l3, ob, wo[:SWA_WIDTH], wo[SWA_WIDTH:], norm_mem_q.reshape(1, D_MODEL),
      w_mq.astype(BF16), mem_k, mem_v, w_mo.astype(BF16), norm_ffn.reshape(1, D_MODEL), wr_hi, wr_lo)


DMA_UNROLL = 8
PACKED = D_MODEL // 2
HI_MASK = 0xFFFF0000


def _pack_rows(x):
    bits = lambda v: lax.bitcast_convert_type(v.astype(BF16).astype(F32), jnp.uint32)
    return (bits(x[:, :PACKED]) & jnp.uint32(HI_MASK)) | (bits(x[:, PACKED:]) >> 16)


def _unpack_rows(u):
    hi = lax.bitcast_convert_type(u & jnp.uint32(HI_MASK), F32)
    lo = lax.bitcast_convert_type(u << 16, F32)
    return hi, lo


def _dispatch_body(dest_ref, h_ref, xs_in_ref, xs_ref, sem, *, tm):
    del xs_in_ref

    def issue(t, c):
        for k in range(TOP_K):
            dst = xs_ref.at[pl.ds(dest_ref[0, 0, t * TOP_K + k], 1)]
            pltpu.make_async_copy(h_ref.at[pl.ds(t, 1)], dst, sem).start(priority=k % 2)
        return c

    lax.fori_loop(0, tm, issue, 0, unroll=DMA_UNROLL)
    for _ in range(TOP_K):
        pltpu.make_async_copy(h_ref, xs_ref.at[pl.ds(0, tm)], sem).wait()


def _dispatch(h2, dest, n_slots, tm=2048):
    n = h2.shape[0]
    dest3 = dest.reshape(n // tm, 1, tm * TOP_K)
    xs0 = jnp.zeros((n_slots, PACKED), jnp.uint32)
    return pl.pallas_call(
        functools.partial(_dispatch_body, tm=tm),
        grid=(n // tm,),
        in_specs=[pl.BlockSpec((1, 1, tm * TOP_K), lambda i: (i, 0, 0), memory_space=pltpu.SMEM),
                  pl.BlockSpec((tm, PACKED), lambda i: (i, 0)),
                  pl.BlockSpec(memory_space=pl.ANY)],
        out_specs=pl.BlockSpec(memory_space=pl.ANY),
        out_shape=jax.ShapeDtypeStruct((n_slots, PACKED), jnp.uint32),
        scratch_shapes=[pltpu.SemaphoreType.DMA(())],
        input_output_aliases={2: 0},
        compiler_params=_params("arbitrary"),
        name="moe_dispatch",
    )(dest3, h2, xs0)


def _expert_body(be_ref, nused_ref, x_ref, wg_ref, wu_ref, wd_ref, y_ref):
    i = pl.program_id(0)

    @pl.when(i < nused_ref[0])
    def _():
        x = jnp.concatenate([half.astype(BF16) for half in _unpack_rows(x_ref[...])], axis=-1)
        a = _dot(x, wg_ref[...])
        b = _dot(x, wu_ref[...])
        hid = (a * jax.nn.sigmoid(a) * b).astype(BF16)
        y_ref[...] = _pack_rows(_dot(hid, wd_ref[...]))

    @pl.when(i >= nused_ref[0])
    def _():
        y_ref[...] = jnp.zeros_like(y_ref)


def _experts(xs, block_expert, n_used, w_gate, w_up, w_down):
    n_slots = xs.shape[0]
    n_blocks = n_slots // ROUTE_BLOCK
    grid_spec = pltpu.PrefetchScalarGridSpec(
        num_scalar_prefetch=2,
        grid=(n_blocks,),
        in_specs=[pl.BlockSpec((ROUTE_BLOCK, PACKED), lambda i, be, nu: (i, 0)),
                  pl.BlockSpec((None, D_MODEL, D_EXPERT), lambda i, be, nu: (be[i], 0, 0)),
                  pl.BlockSpec((None, D_MODEL, D_EXPERT), lambda i, be, nu: (be[i], 0, 0)),
                  pl.BlockSpec((None, D_EXPERT, D_MODEL), lambda i, be, nu: (be[i], 0, 0))],
        out_specs=pl.BlockSpec((ROUTE_BLOCK, PACKED), lambda i, be, nu: (i, 0)),
    )
    return pl.pallas_call(
        _expert_body,
        grid_spec=grid_spec,
        out_shape=jax.ShapeDtypeStruct((n_slots, PACKED), jnp.uint32),
        compiler_params=_params("arbitrary"),
        name="moe_experts",
    )(block_expert, n_used, xs, w_gate, w_up, w_down)


def _combine_body(dest_ref, x_ref, route_ref, g_ref, yb_ref, o_ref, buf_ref, sem, *, tm):
    def issue(t, c):
        for k in range(TOP_K):
            src = yb_ref.at[pl.ds(dest_ref[0, 0, t * TOP_K + k], 1)]
            pltpu.make_async_copy(src, buf_ref.at[k, pl.ds(t, 1)], sem).start(priority=k % 2)
        return c

    lax.fori_loop(0, tm, issue, 0, unroll=DMA_UNROLL)
    for k in range(TOP_K):
        pltpu.make_async_copy(yb_ref.at[pl.ds(0, tm)], buf_ref.at[k], sem).wait()
    route = route_ref[...]
    hi0, lo0 = _unpack_rows(buf_ref[0])
    hi1, lo1 = _unpack_rows(buf_ref[1])
    g0, g1 = route[:, 2:3], route[:, 3:4]
    y = jnp.concatenate([g0 * hi0 + g1 * hi1, g0 * lo0 + g1 * lo1], axis=-1)
    o_ref[...] = _rms(x_ref[...] + y, g_ref[...])


def _combine(x2, route, dest, yb, norm_final, tm=1024):
    n = x2.shape[0]
    dest3 = dest.reshape(n // tm, 1, tm * TOP_K)
    return pl.pallas_call(
        functools.partial(_combine_body, tm=tm),
        grid=(n // tm,),
        in_specs=[pl.BlockSpec((1, 1, tm * TOP_K), lambda i: (i, 0, 0), memory_space=pltpu.SMEM),
                  pl.BlockSpec((tm, D_MODEL), lambda i: (i, 0)),
                  pl.BlockSpec((tm, LANES), lambda i: (i, 0)),
                  pl.BlockSpec((1, D_MODEL), lambda i: (0, 0)),
                  pl.BlockSpec(memory_space=pl.ANY)],
        out_specs=pl.BlockSpec((tm, D_MODEL), lambda i: (i, 0)),
        out_shape=jax.ShapeDtypeStruct((n, D_MODEL), F32),
        scratch_shapes=[pltpu.VMEM((TOP_K, tm, PACKED), jnp.uint32), pltpu.SemaphoreType.DMA(())],
        compiler_params=_params("arbitrary"),
        name="moe_combine",
    )(dest3, x2, route, norm_final.reshape(1, D_MODEL), yb)


def _moe(x2, h2, route, counts, w_gate, w_up, w_down, norm_final):
    n = x2.shape[0]
    m_slots = n * TOP_K
    n_blocks = -(-(m_slots + N_EXPERTS * (ROUTE_BLOCK - 1)) // ROUTE_BLOCK)
    cnt = counts[0, N_GROUPS:N_GROUPS + N_EXPERTS].astype(jnp.int32)
    padded = (cnt + ROUTE_BLOCK - 1) // ROUTE_BLOCK * ROUTE_BLOCK
    pad_end = jnp.cumsum(padded)
    seg_start = pad_end - padded
    eid = route[:, 0:TOP_K].astype(jnp.int32)
    rank = route[:, 4:4 + TOP_K].astype(jnp.int32)
    dest = (seg_start[eid] + rank).reshape(-1)
    block_start = jnp.arange(n_blocks, dtype=jnp.int32) * ROUTE_BLOCK
    block_expert = jnp.minimum(jnp.sum(pad_end[None, :] <= block_start[:, None], axis=1), N_EXPERTS - 1)
    block_expert = block_expert.astype(jnp.int32)
    n_used = (pad_end[-1:] // ROUTE_BLOCK).astype(jnp.int32)
    xs = _dispatch(h2, dest, n_blocks * ROUTE_BLOCK)
    yb = _experts(xs, block_expert, n_used, w_gate, w_up, w_down)
    return _combine(x2, route, dest, yb, norm_final)


GDN_TB = 512
GDN_HALO = 16
GDN_GROUP = 4


def _gdn_body(gd_ref, prev_ref, next_ref, abr_ref, cw_ref, alog_ref, dt_ref, *rest, rev, final):
    if final:
        z_ref, oprev_ref, gn_ref, o_ref, xpad_ref, state_ref = rest
    else:
        o_ref, xpad_ref, state_ref = rest
    i = pl.program_id(1)
    nblk = pl.num_programs(1)
    n_chunks = GDN_TB // CHUNK
    n_pairs = GDN_HEADS // 2
    width3 = 3 * GDN_WIDTH

    @pl.when(i == 0)
    def _():
        state_ref[...] = jnp.zeros_like(state_ref)

    blk = (nblk - 1 - i) if rev else i
    zero_halo = jnp.zeros((GDN_HALO, width3), BF16)
    xpad_ref[:GDN_HALO, :] = jnp.where(blk == 0, zero_halo, prev_ref[...])
    xpad_ref[GDN_HALO:GDN_HALO + GDN_TB, :] = gd_ref[...]
    xpad_ref[GDN_HALO + GDN_TB:, :] = jnp.where(blk == nblk - 1, zero_halo, next_ref[...])
    win = CHUNK + 2 * GDN_HALO
    side_taps = [j for j in range(CONV_K) if j != CONV_K // 2]
    sr = lax.broadcasted_iota(jnp.int32, (len(side_taps) * CHUNK, 1), 0)
    sc = lax.broadcasted_iota(jnp.int32, (1, win), 1)
    tap = sr // CHUNK
    tap = tap + jnp.where(tap >= CONV_K // 2, 1, 0)
    shift_sel = jnp.where(sc == sr % CHUNK + tap + (GDN_HALO - CONV_K // 2), 1.0, 0.0).astype(BF16)

    lane = lax.broadcasted_iota(jnp.int32, (1, LANES), 1)
    left = lane < CHUNK
    tok = lane % CHUNK
    row = lax.broadcasted_iota(jnp.int32, (CHUNK, 1), 0)
    causal = (tok >= row) if rev else (tok <= row)
    strict = (tok > row) if rev else (tok < row)
    eye = jnp.where(tok == row, 1.0, 0.0)
    r2 = lax.broadcasted_iota(jnp.int32, (LANES, 1), 0)
    same_head = (r2 // CHUNK) == (lane // CHUNK)
    bd_ones = jnp.where(same_head, 1.0, 0.0).astype(BF16)
    tri_in = (r2 % CHUNK >= tok) if rev else (r2 % CHUNK <= tok)
    tri_bd = jnp.where(jnp.logical_and(same_head, tri_in), 1.0, 0.0).astype(BF16)
    row8 = lax.broadcasted_iota(jnp.int32, (8, 1), 0)

    def blockdiag(x):
        xb = x.astype(BF16)
        zero = jnp.zeros_like(xb)
        return jnp.concatenate([jnp.where(left, xb, zero), jnp.where(left, zero, xb)], axis=0)

    def wide_mm(x, y):
        return _dot(x.astype(BF16), blockdiag(y))

    def head_sums(x):
        return _dot(x.astype(BF16), bd_ones)

    def group(gi, carry):
        g0 = (n_chunks // GDN_GROUP - 1 - gi) if rev else gi
        chunks = [g0 * GDN_GROUP + ((GDN_GROUP - 1 - j) if rev else j) for j in range(GDN_GROUP)]
        units = [(ci, p) for ci in range(GDN_GROUP) for p in range(n_pairs)]
        r0 = [pl.multiple_of(c * CHUNK, CHUNK) for c in chunks]

        gates, gc_rows, gl_rows = [], [], []
        for c in chunks:
            graw = abr_ref[c]
            xg = graw + dt_ref[...]
            softplus = jnp.maximum(xg, 0.0) + jnp.log(1.0 + jnp.exp(-jnp.abs(xg)))
            g = jnp.where(row8 < n_pairs, jax.nn.sigmoid(graw), -jnp.exp(alog_ref[...]) * softplus)
            g3 = _split3(g)
            gates.append(g)
            gc_rows.append(sum(_dot(t, tri_bd) for t in g3))
            gl_rows.append(sum(_dot(t, bd_ones) for t in g3))

        shifted = [_dot(shift_sel, xpad_ref[pl.ds(r0[ci], win), :]) for ci in range(GDN_GROUP)]

        def conv(ci, p, col0):
            cols = slice(col0 + p * LANES, col0 + (p + 1) * LANES)
            centre = xpad_ref[pl.ds(r0[ci] + GDN_HALO, CHUNK), cols].astype(F32)
            acc = centre * cw_ref[CONV_K // 2:CONV_K // 2 + 1, cols]
            for i, j in enumerate(side_taps):
                acc = acc + shifted[ci][i * CHUNK:(i + 1) * CHUNK, cols] * cw_ref[j:j + 1, cols]
            return acc * jax.nn.sigmoid(acc)

        qkv = {u: [conv(*u, col0) for col0 in (0, GDN_WIDTH, 2 * GDN_WIDTH)] for u in units}
        ss = {u: head_sums(jnp.concatenate([qkv[u][0] * qkv[u][0], qkv[u][1] * qkv[u][1]], axis=0))
              for u in units}
        prep = {}
        for (ci, p) in units:
            beta_r = gates[ci][p:p + 1]
            g_r = gates[ci][n_pairs + p:n_pairs + p + 1]
            lhs = jnp.concatenate([jnp.where(causal, g_r, 0.0), eye * beta_r], axis=0)
            prep[(ci, p)] = sum(_dot(t, bd_ones) for t in _split2(lhs))

        kq, kn_bd, e_gc, k_upd, decay_b = {}, {}, {}, {}, {}
        for u in units:
            ci, p = u
            qn = qkv[u][0] * lax.rsqrt(ss[u][:CHUNK] + EPS) * (HEAD_DIM ** -0.5)
            kn = qkv[u][1] * lax.rsqrt(ss[u][CHUNK:] + EPS)
            kq[u] = jnp.concatenate([kn, qn], axis=0).astype(BF16)
            kn_bd[u] = blockdiag(kn)
            beta_r = gates[ci][p:p + 1]
            gc_r = gc_rows[ci][n_pairs + p:n_pairs + p + 1]
            gl_r = gl_rows[ci][n_pairs + p:n_pairs + p + 1]
            gc_b, beta_b = prep[u][:CHUNK], prep[u][CHUNK:]
            decay_b[u] = jnp.exp(jnp.where(causal, gc_b - gc_r, NEG_BIG)) * beta_r
            e_gc[u] = jnp.exp(gc_b)
            k_upd[u] = (kn * (jnp.exp(gl_r - gc_b) * beta_b)).astype(BF16)

        kk_qk = {u: _dot_nt(kq[u], kn_bd[u]) for u in units}
        lpow = {u: jnp.where(strict, kk_qk[u][:CHUNK] * decay_b[u], 0.0) for u in units}
        amat = {u: kk_qk[u][CHUNK:] * decay_b[u] for u in units}
        tinv = {u: eye - lpow[u] for u in units}
        for _ in range(5):
            lpow = {u: wide_mm(lpow[u], lpow[u]) for u in units}
            tinv = {u: tinv[u] + wide_mm(tinv[u], lpow[u]) for u in units}

        for ci in range(GDN_GROUP):
            us = [(ci, p) for p in range(n_pairs)]
            state = {u: state_ref[u[1]] for u in us}
            pq = {u: _dot(kq[u], state[u].astype(BF16)) for u in us}
            vhat = {u: wide_mm(tinv[u], qkv[u][2] - e_gc[u] * pq[u][:CHUNK]) for u in us}
            upd = {u: _dot_tn(k_upd[u], vhat[u].astype(BF16)) for u in us}
            for u in us:
                gl_r = gl_rows[ci][n_pairs + u[1]:n_pairs + u[1] + 1]
                state_ref[u[1]] = state[u] * jnp.exp(gl_r) + jnp.where(same_head, upd[u], 0.0)
            o = {u: e_gc[u] * pq[u][CHUNK:] + wide_mm(amat[u], vhat[u]) for u in us}
            for u in us:
                cs = slice(u[1] * LANES, (u[1] + 1) * LANES)
                rows = pl.ds(r0[ci], CHUNK)
                if final:
                    ob = o[u] + oprev_ref[rows, cs]
                    ms = head_sums(ob * ob) * (1.0 / HEAD_DIM)
                    zz = z_ref[rows, cs].astype(F32)
                    out = ob * lax.rsqrt(ms + EPS) * gn_ref[:, cs] * (zz * jax.nn.sigmoid(zz))
                    o_ref[rows, cs] = out.astype(o_ref.dtype)
                else:
                    o_ref[rows, cs] = o[u]
        return carry

    lax.fori_loop(0, n_chunks // GDN_GROUP, group, 0)


def _gdn_pass(gd3, abr, conv_w, alog_row, dt_row, rev, final_args=None):
    bsz, seq, width3 = gd3.shape
    nblk = seq // GDN_TB
    hpb = GDN_TB // GDN_HALO
    blk = (lambda i: nblk - 1 - i) if rev else (lambda i: i)
    main = lambda w: pl.BlockSpec((None, GDN_TB, w), lambda b, i: (b, blk(i), 0))
    full = lambda shape: pl.BlockSpec(shape, lambda b, i: (0,) * len(shape))
    in_specs = [main(width3),
                pl.BlockSpec((None, GDN_HALO, width3), lambda b, i: (b, jnp.maximum(blk(i) * hpb - 1, 0), 0)),
                pl.BlockSpec((None, GDN_HALO, width3),
                             lambda b, i: (b, jnp.minimum((blk(i) + 1) * hpb, seq // GDN_HALO - 1), 0)),
                pl.BlockSpec((None, GDN_TB // CHUNK, 8, LANES), lambda b, i: (b, blk(i), 0, 0)),
                full((CONV_K, width3)), full((8, LANES)), full((8, LANES))]
    args = [gd3, gd3, gd3, abr, conv_w, alog_row, dt_row]
    final = final_args is not None
    if final:
        z3, o_prev, gn = final_args
        in_specs += [main(GDN_WIDTH), main(GDN_WIDTH), full((1, GDN_WIDTH))]
        args += [z3, o_prev, gn]
    return pl.pallas_call(
        functools.partial(_gdn_body, rev=rev, final=final),
        grid=(bsz, nblk),
        in_specs=in_specs,
        out_specs=main(GDN_WIDTH),
        out_shape=jax.ShapeDtypeStruct((bsz, seq, GDN_WIDTH), BF16 if final else F32),
        scratch_shapes=[pltpu.VMEM((GDN_TB + 2 * GDN_HALO, width3), BF16),
                        pltpu.VMEM((GDN_HEADS // 2, LANES, LANES), F32)],
        compiler_params=_params("parallel", "arbitrary"),
        name="gdn_bwd" if rev else "gdn_fwd",
    )(*args)


def _pair_rows(x, bsz, seq):
    x = x.reshape(bsz, seq // CHUNK, CHUNK, GDN_HEADS // 2, 2)
    return x.transpose(0, 1, 3, 4, 2).reshape(bsz, seq // CHUNK, GDN_HEADS // 2, LANES)


def _gdn(gd, z, ab, conv_w, a_log, dt_bias, gdn_norm, bsz, seq):
    gd3 = gd.reshape(bsz, seq, 3 * GDN_WIDTH)
    z3 = z.reshape(bsz, seq, GDN_WIDTH)
    gn = jnp.tile(gdn_norm, GDN_HEADS).reshape(1, GDN_WIDTH)
    out = None
    for d in range(2):
        beta = _pair_rows(ab[:, d * GDN_HEADS:(d + 1) * GDN_HEADS], bsz, seq)
        araw = _pair_rows(ab[:, (2 + d) * GDN_HEADS:(3 + d) * GDN_HEADS], bsz, seq)
        abr = jnp.concatenate([beta, araw], axis=2)
        per_head = lambda v: jnp.concatenate(
            [jnp.zeros((GDN_HEADS // 2, LANES), F32),
             jnp.repeat(v[d], CHUNK).reshape(GDN_HEADS // 2, LANES)], axis=0)
        final_args = None if d == 0 else (z3, out, gn)
        out = _gdn_pass(gd3, abr, conv_w, per_head(a_log), per_head(dt_bias), rev=(d == 1), final_args=final_args)
    return out.reshape(bsz * seq, GDN_WIDTH)


def _gdn_scan_jax(q, k, v, g, beta):
    B, T, H, dk = q.shape
    C = CHUNK
    N = T // C
    chunks = lambda t: t.reshape(B, N, C, H, -1).transpose(1, 0, 3, 2, 4)
    qc, kc, vc = chunks(q), chunks(k), chunks(v)
    gc = jnp.cumsum(g.reshape(B, N, C, H).transpose(1, 0, 3, 2), axis=-1)
    bc = beta.reshape(B, N, C, H).transpose(1, 0, 3, 2)[..., None]
    lower = jnp.tril(jnp.ones((C, C), bool))
    strict = jnp.tril(jnp.ones((C, C), bool), -1)
    decay = jnp.exp(jnp.where(lower, gc[..., :, None] - gc[..., None, :], -jnp.inf))
    kbeta = kc * bc
    lmat = jnp.where(strict, jnp.einsum('nbhid,nbhjd->nbhij', kbeta, kc) * decay, 0.0)
    solve = lambda rhs: lax.linalg.triangular_solve(lmat, rhs, left_side=True, lower=True, unit_diagonal=True)
    u = solve(vc * bc)
    w = solve(kbeta * jnp.exp(gc)[..., None])
    a_qk = jnp.einsum('nbhid,nbhjd->nbhij', qc, kc) * decay

    def step(state, inp):
        q_i, k_i, u_i, w_i, g_i, a_i = inp
        v_new = u_i - jnp.einsum('bhck,bhkv->bhcv', w_i, state)
        o_i = (jnp.einsum('bhck,bhkv->bhcv', q_i * jnp.exp(g_i)[..., None], state)
               + jnp.einsum('bhij,bhjv->bhiv', a_i, v_new))
        g_last = g_i[..., -1:]
        state = (state * jnp.exp(g_last)[..., None]
                 + jnp.einsum('bhck,bhcv->bhkv', k_i * jnp.exp(g_last - g_i)[..., None], v_new))
        return state, o_i

    s0 = jnp.zeros((B, H, dk, dk), F32)
    _, o = lax.scan(step, s0, (qc, kc, u, w, gc, a_qk))
    return o.transpose(1, 0, 3, 2, 4).reshape(B, T, H, dk)


def _gdn_jax(gd, z, ab, conv_w, a_log, dt_bias, gdn_norm, bsz, seq):
    c = lax.conv_general_dilated(
        gd.astype(F32).reshape(bsz, seq, 3 * GDN_WIDTH), conv_w[:, None, :], window_strides=(1,),
        padding=[(CONV_K // 2, CONV_K // 2)], dimension_numbers=('NWC', 'WIO', 'NWC'),
        feature_group_count=3 * GDN_WIDTH)
    c = jax.nn.silu(c)
    l2 = lambda t: t * lax.rsqrt(jnp.sum(t * t, axis=-1, keepdims=True) + EPS)
    qb, kb, vb = [t.reshape(bsz, seq, GDN_HEADS, HEAD_DIM) for t in jnp.split(c, 3, axis=-1)]
    qb = l2(qb) * (HEAD_DIM ** -0.5)
    kb = l2(kb)
    ab4 = ab[:, :4 * GDN_HEADS].reshape(bsz, seq, 4, GDN_HEADS)
    beta = jax.nn.sigmoid(ab4[:, :, 0:2])
    g = -jnp.exp(a_log) * jax.nn.softplus(ab4[:, :, 2:4] + dt_bias)
    fwd = _gdn_scan_jax(qb, kb, vb, g[:, :, 0], beta[:, :, 0])
    flip = lambda t: t[:, ::-1]
    bwd = flip(_gdn_scan_jax(flip(qb), flip(kb), flip(vb), flip(g[:, :, 1]), flip(beta[:, :, 1])))
    ob = fwd + bwd
    ob = (ob * lax.rsqrt(jnp.mean(ob * ob, axis=-1, keepdims=True) + EPS) * gdn_norm
          * jax.nn.silu(z.astype(F32).reshape(bsz, seq, GDN_HEADS, HEAD_DIM)))
    return ob.reshape(bsz * seq, GDN_WIDTH).astype(BF16)


def _encoder(x, mem, norm_mix, w_in, conv_w, a_log, dt_bias, gdn_norm, w_out, norm_mem_q, norm_mem_kv,
             w_mq, w_mkv, w_mo, norm_ffn, w_router_g, w_router_e, w_gate, w_up, w_down, norm_final):
    bsz, seq, _ = x.shape
    x2d = x.reshape(bsz * seq, D_MODEL)
    qkv1, qkv4, qkv16, gd, z, ab = _inproj(x2d, norm_mix, w_in, bsz, seq)
    slopes = jnp.exp2(-8.0 * jnp.arange(1, SWA_HEADS + 1, dtype=F32) / SWA_HEADS)
    branches = [_attn_branch(qkv, slopes, d) for qkv, d in zip((qkv1, qkv4, qkv16), DILATIONS)]
    ob = _gdn(gd, z, ab, conv_w, a_log, dt_bias, gdn_norm, bsz, seq)
    mem_k, mem_v = _memkv(mem.reshape(-1, D_MODEL), norm_mem_kv, w_mkv)
    mem_k = mem_k.reshape(bsz, -1, MEM_HEADS * MEM_HEAD_DIM)
    mem_v = mem_v.reshape(bsz, -1, MEM_HEADS * MEM_HEAD_DIM)
    x2, h2, route, counts = _trunk(x2d, branches, ob, mem_k, mem_v, w_out, norm_mem_q, w_mq, w_mo,
                                   norm_ffn, w_router_g, w_router_e, bsz, seq)
    y = _moe(x2, h2, route, counts, w_gate, w_up, w_down, norm_final)
    return y.reshape(bsz, seq, D_MODEL)


def kernel(x_prompt, x_sample, mem_prompt, mem_sample, norm_mix, w_in, conv_w, a_log, dt_bias, gdn_norm,
           w_out, norm_mem_q, norm_mem_kv, w_mq, w_mkv, w_mo, norm_ffn, w_router_g, w_router_e,
           w_gate, w_up, w_down, norm_final):
    p = dict(norm_mix=norm_mix[0], w_in=w_in[0], conv_w=conv_w[0], a_log=a_log[0], dt_bias=dt_bias[0],
             gdn_norm=gdn_norm[0], w_out=w_out[0], norm_mem_q=norm_mem_q[0], norm_mem_kv=norm_mem_kv[0],
             w_mq=w_mq[0], w_mkv=w_mkv[0], w_mo=w_mo[0], norm_ffn=norm_ffn[0], w_router_g=w_router_g[0],
             w_router_e=w_router_e[0], w_gate=w_gate[0].astype(BF16), w_up=w_up[0].astype(BF16),
             w_down=w_down[0].astype(BF16), norm_final=norm_final)
    return (_encoder(x_prompt, mem_prompt, **p), _encoder(x_sample, mem_sample, **p))
```

```python
import functools

import jax
import jax.numpy as jnp
import numpy as np
from jax import lax
from jax.experimental import pallas as pl
from jax.experimental.pallas import tpu as pltpu

F32 = jnp.float32
BF16 = jnp.bfloat16

D_MODEL = 1024
HEAD_DIM = 64
SWA_HEADS = 8
GDN_HEADS = 8
SWA_WIDTH = SWA_HEADS * HEAD_DIM
GDN_WIDTH = GDN_HEADS * HEAD_DIM
DILATIONS = (1, 4, 16)
ATT_W = 64
CONV_K = 5
CHUNK = 64
MEM_HEADS = 4
MEM_HEAD_DIM = 256
N_GROUPS = 4
EXPERTS_PER_GROUP = 8
N_EXPERTS = 32
TOP_K = 2
D_EXPERT = 512
ROUTE_BLOCK = 256
EPS = 1e-6

LANES = 128
VMEM_LIMIT = 56 * 1024 * 1024
NEG_BIG = -1e30


def _dot(a, b):
    return jnp.dot(a, b, preferred_element_type=F32)


def _dot_nt(a, b):
    return lax.dot_general(a, b, (((1,), (1,)), ((), ())), preferred_element_type=F32)


def _dot_tn(a, b):
    return lax.dot_general(a, b, (((0,), (0,)), ((), ())), preferred_element_type=F32)


def _split2(x):
    hi = x.astype(BF16)
    lo = (x - hi.astype(F32)).astype(BF16)
    return hi, lo


def _split3(x):
    hi = x.astype(BF16)
    r = x - hi.astype(F32)
    mid = r.astype(BF16)
    lo = (r - mid.astype(F32)).astype(BF16)
    return hi, mid, lo


def _rms(x, g):
    return x * lax.rsqrt(jnp.mean(x * x, axis=-1, keepdims=True) + EPS) * g


def _params(*sem):
    return pltpu.CompilerParams(dimension_semantics=sem, vmem_limit_bytes=VMEM_LIMIT)


def _inproj_body(x_ref, g_ref, wa_ref, wgd_ref, wz_ref, wab_hi_ref, wab_lo_ref,
                 qkv1_ref, qkv4_ref, qkv16_ref, gd_ref, z_ref, ab_ref, stage_ref, *, tm):
    h = _rms(x_ref[...], g_ref[...])
    h_hi, h_lo = _split2(h)
    qkv = _dot(h_hi, wa_ref[...])
    qkv1_ref[...] = qkv.astype(BF16)
    for c in range(qkv.shape[1] // LANES):
        cols = slice(c * LANES, (c + 1) * LANES)
        stage_ref[c] = qkv[:, cols]
        for dil, out_ref in ((DILATIONS[1], qkv4_ref), (DILATIONS[2], qkv16_ref)):
            for r in range(dil):
                out_ref[r, :, cols] = stage_ref[c, pl.ds(r, tm // dil, stride=dil), :].astype(BF16)
    gd_ref[...] = _dot(h_hi, wgd_ref[...]).astype(BF16)
    z_ref[...] = _dot(h_hi, wz_ref[...]).astype(BF16)
    wab_hi = wab_hi_ref[...]
    ab = _dot(h_hi, wab_hi) + _dot(h_lo, wab_hi) + _dot(h_hi, wab_lo_ref[...])
    ab_ref[...] = ab


def _inproj(x2, norm_mix, w_in, bsz, seq, tm=512):
    n = x2.shape[0]
    nblk = seq // tm
    width = 3 * SWA_WIDTH
    wa = w_in[:, :3 * SWA_WIDTH].astype(BF16)
    wgd = w_in[:, 3 * SWA_WIDTH:3 * SWA_WIDTH + 3 * GDN_WIDTH].astype(BF16)
    wz = w_in[:, 3 * SWA_WIDTH + 3 * GDN_WIDTH:3 * SWA_WIDTH + 4 * GDN_WIDTH].astype(BF16)
    wab = jnp.pad(w_in[:, 3 * SWA_WIDTH + 4 * GDN_WIDTH:], ((0, 0), (0, LANES - 4 * GDN_HEADS)))
    wab_hi, wab_lo = _split2(wab)
    full = lambda shape: pl.BlockSpec(shape, lambda b, i: (0, 0))
    rows = lambda w: pl.BlockSpec((tm, w), lambda b, i: (b * nblk + i, 0))
    strided = lambda d: pl.BlockSpec((None, d, tm // d, width), lambda b, i: (b, 0, i, 0))
    d1, d4, d16 = DILATIONS
    return pl.pallas_call(
        functools.partial(_inproj_body, tm=tm),
        grid=(bsz, nblk),
        in_specs=[rows(D_MODEL), full((1, D_MODEL)), full(wa.shape), full(wgd.shape), full(wz.shape),
                  full(wab_hi.shape), full(wab_lo.shape)],
        out_specs=[pl.BlockSpec((None, None, tm, width), lambda b, i: (b, 0, i, 0)), strided(d4), strided(d16),
                   rows(3 * GDN_WIDTH), rows(GDN_WIDTH), rows(LANES)],
        out_shape=[jax.ShapeDtypeStruct((bsz, d, seq // d, width), BF16) for d in (d1, d4, d16)]
                  + [jax.ShapeDtypeStruct((n, 3 * GDN_WIDTH), BF16),
                     jax.ShapeDtypeStruct((n, GDN_WIDTH), BF16),
                     jax.ShapeDtypeStruct((n, LANES), F32)],
        scratch_shapes=[pltpu.VMEM((width // LANES, tm, LANES), F32)],
        compiler_params=_params("parallel", "parallel"),
        name="inproj",
    )(x2, norm_mix.reshape(1, D_MODEL), wa, wgd, wz, wab_hi, wab_lo)


ATT_QB = 128
ATT_KB = ATT_QB + 2 * ATT_W
ATT_GROUP = 4


def _attn_bias(slopes, dil):
    row = jnp.arange(ATT_QB)[:, None]
    col = jnp.arange(ATT_KB)[None, :]
    rel = jnp.stack([jnp.abs(col - row - var * ATT_W) for var in range(3)])
    dist = (rel * dil).astype(F32)
    bias = jnp.where(rel <= ATT_W, -slopes[:, None, None, None] * dist, NEG_BIG)
    bias = bias.reshape(SWA_HEADS // 2, 2, 3, ATT_QB, ATT_KB).transpose(0, 2, 1, 3, 4)
    return bias.reshape(SWA_HEADS // 2, 6, ATT_QB, ATT_KB)


def _attn_body(bias_ref, q_ref, k_ref, v_ref, o_ref, lse_ref, *, n_qb, seq_l, n_pw):
    lb = pl.program_id(3)
    lane = lax.broadcasted_iota(jnp.int32, (1, LANES), 1)
    left = lane < HEAD_DIM
    group = min(ATT_GROUP, n_qb)

    def qgroup(gi, carry):
        rows, var, kb, vb, q, pw_of, cols = [], [], [], [], [], [], []
        for j in range(group):
            qi = gi * group + j
            n0 = (lb * n_qb + qi) * ATT_QB
            kstart = pl.multiple_of(jnp.clip(n0 - ATT_W, 0, seq_l - ATT_KB), ATT_W)
            for pw in range(n_pw):
                cs = slice(pw * LANES, (pw + 1) * LANES)
                var.append((n0 - kstart) // ATT_W)
                rows.append(pl.ds(pl.multiple_of(qi * ATT_QB, ATT_QB), ATT_QB))
                q.append(q_ref[rows[-1], cs] * jnp.asarray(HEAD_DIM ** -0.5, BF16))
                kb.append(k_ref[pl.ds(kstart, ATT_KB), cs])
                vb.append(v_ref[pl.ds(kstart, ATT_KB), cs])
                pw_of.append(pw)
                cols.append(cs)
        blocks = range(group * n_pw)
        units = [(j, h) for j in blocks for h in range(2)]
        mine = [left, jnp.logical_not(left)]
        s = {(j, h): _dot_nt(jnp.where(mine[h], q[j], jnp.zeros_like(q[j])), kb[j])
                     + bias_ref[pw_of[j], var[j] * 2 + h] for (j, h) in units}
        m = {u: jnp.max(s[u], axis=-1, keepdims=True) for u in units}
        p = {u: jnp.exp(s[u] - m[u]).astype(BF16) for u in units}
        acc = {(j, h): _dot(p[(j, h)], jnp.where(mine[h], vb[j], jnp.ones_like(vb[j]))) for (j, h) in units}
        for j in blocks:
            num = jnp.where(left, acc[(j, 0)], acc[(j, 1)])
            den = pltpu.roll(jnp.where(left, acc[(j, 1)], acc[(j, 0)]), HEAD_DIM, 1)
            mx = jnp.where(left, m[(j, 0)], m[(j, 1)])
            o_ref[rows[j], cols[j]] = (num / den).astype(BF16)
            lse_ref[rows[j], cols[j]] = mx + jnp.log(den)
        return carry

    lax.fori_loop(0, n_qb // group, qgroup, 0)


def _attn_branch(qkv, slopes, dil):
    bsz, _, seq_l, _ = qkv.shape
    lblk = min(2048, seq_l)
    n_qb = lblk // ATT_QB
    n_pw = 1 if n_qb >= ATT_GROUP else 2
    n_steps = SWA_HEADS // 2 // n_pw
    width = n_pw * LANES
    qspec = pl.BlockSpec((None, None, lblk, width), lambda p, b, r, l: (b, r, l, p))
    kspec = pl.BlockSpec((None, None, seq_l, width), lambda p, b, r, l: (b, r, 0, n_steps + p))
    vspec = pl.BlockSpec((None, None, seq_l, width), lambda p, b, r, l: (b, r, 0, 2 * n_steps + p))
    ospec = pl.BlockSpec((None, None, lblk, width), lambda p, b, r, l: (b, r, l, p))
    bspec = pl.BlockSpec((n_pw, 6, ATT_QB, ATT_KB), lambda p, b, r, l: (p, 0, 0, 0))
    return pl.pallas_call(
        functools.partial(_attn_body, n_qb=n_qb, seq_l=seq_l, n_pw=n_pw),
        grid=(n_steps, bsz, dil, seq_l // lblk),
        in_specs=[bspec, qspec, kspec, vspec],
        out_specs=[ospec, ospec],
        out_shape=[jax.ShapeDtypeStruct((bsz, dil, seq_l, SWA_WIDTH), BF16),
                   jax.ShapeDtypeStruct((bsz, dil, seq_l, SWA_WIDTH), F32)],
        compiler_params=_params("parallel", "parallel", "parallel", "arbitrary"),
        name=f"dilated_attn_d{dil}",
    )(_attn_bias(slopes, dil), qkv, qkv, qkv)


def _memkv_body(m_ref, g_ref, w_ref, k_ref, v_ref):
    h = _rms(m_ref[...], g_ref[...]).astype(BF16)
    kv = _dot(h, w_ref[...])
    width = MEM_HEADS * MEM_HEAD_DIM
    k_ref[...] = (kv[:, :width] * (MEM_HEAD_DIM ** -0.5)).astype(BF16)
    v_ref[...] = kv[:, width:].astype(BF16)


def _memkv(mem2, norm_kv, w_mkv, tm=256):
    n = mem2.shape[0]
    width = MEM_HEADS * MEM_HEAD_DIM
    return pl.pallas_call(
        _memkv_body,
        grid=(n // tm,),
        in_specs=[pl.BlockSpec((tm, D_MODEL), lambda i: (i, 0)),
                  pl.BlockSpec((1, D_MODEL), lambda i: (0, 0)),
                  pl.BlockSpec((D_MODEL, 2 * width), lambda i: (0, 0))],
        out_specs=[pl.BlockSpec((tm, width), lambda i: (i, 0))] * 2,
        out_shape=[jax.ShapeDtypeStruct((n, width), BF16)] * 2,
        compiler_params=_params("parallel"),
        name="mem_kv",
    )(mem2, norm_kv.reshape(1, D_MODEL), w_mkv.astype(BF16))


ROUTE_COLS = N_GROUPS + N_EXPERTS


TRUNK_TM = 512
TRUNK_SUB = 256


def _trunk_body(x_ref, o1_ref, o2_ref, o3_ref, l1_ref, l2_ref, l3_ref, ob_ref,
                wo_a_ref, wo_b_ref, gq_ref, wq_ref, k_ref, v_ref, wmo_ref,
                gf_ref, wr_hi_ref, wr_lo_ref,
                x2_ref, h2_ref, route_ref, count_ref, run_ref, obuf_ref, lbuf_ref):
    first = jnp.logical_and(pl.program_id(0) == 0, pl.program_id(1) == 0)

    @pl.when(first)
    def _():
        run_ref[...] = jnp.zeros_like(run_ref)

    n_cb = SWA_WIDTH // LANES
    for bi, (o_ref, l_ref) in enumerate(((o2_ref, l2_ref), (o3_ref, l3_ref))):
        dil = DILATIONS[bi + 1]
        for r in range(dil):
            rows = pl.ds(r, TRUNK_TM // dil, stride=dil)
            o_r = o_ref[r].astype(F32)
            l_r = l_ref[r]
            for c in range(n_cb):
                obuf_ref[bi * n_cb + c, rows, :] = o_r[:, c * LANES:(c + 1) * LANES]
                lbuf_ref[bi * n_cb + c, rows, :] = l_r[:, c * LANES:(c + 1) * LANES]

    slabs = [slice(j * TRUNK_SUB, (j + 1) * TRUNK_SUB) for j in range(TRUNK_TM // TRUNK_SUB)]

    def token_major(buf_ref, bi, sl):
        return jnp.concatenate([buf_ref[bi * n_cb + c, sl, :] for c in range(n_cb)], axis=-1)

    def merge(sl):
        l1, l2, l3 = l1_ref[sl, :], token_major(lbuf_ref, 0, sl), token_major(lbuf_ref, 1, sl)
        mx = jnp.maximum(jnp.maximum(l1, l2), l3)
        e1, e2, e3 = jnp.exp(l1 - mx), jnp.exp(l2 - mx), jnp.exp(l3 - mx)
        ya = (e1 * o1_ref[sl, :].astype(F32) + e2 * token_major(obuf_ref, 0, sl)
              + e3 * token_major(obuf_ref, 1, sl))
        return (ya / (e1 + e2 + e3)).astype(BF16)

    ya = [merge(sl) for sl in slabs]
    x1 = [x_ref[sl, :] + _dot(y, wo_a_ref[...]) + _dot(ob_ref[sl, :], wo_b_ref[...]) for y, sl in zip(ya, slabs)]

    q = [_dot(_rms(x, gq_ref[...]).astype(BF16), wq_ref[...]).astype(BF16) for x in x1]
    cols = [slice(h * MEM_HEAD_DIM, (h + 1) * MEM_HEAD_DIM) for h in range(MEM_HEADS)]
    units = [(j, cs) for j in range(len(slabs)) for cs in cols]
    s = [_dot_nt(q[j][:, cs], k_ref[:, cs]) for j, cs in units]
    p = [jnp.exp(sh - jnp.max(sh, axis=-1, keepdims=True)) for sh in s]
    den = [jnp.sum(ph, axis=-1, keepdims=True) for ph in p]
    pv = [_dot(ph.astype(BF16), v_ref[:, cs]) for ph, (j, cs) in zip(p, units)]
    heads = [(a / d).astype(BF16) for a, d in zip(pv, den)]
    nh = MEM_HEADS
    x2 = [x + _dot(jnp.concatenate(heads[j * nh:(j + 1) * nh], axis=-1), wmo_ref[...]) for j, x in enumerate(x1)]
    h2 = [_rms(x, gf_ref[...]) for x in x2]
    for sl, x, h in zip(slabs, x2, h2):
        x2_ref[sl, :] = x
        h2_ref[sl, :] = _pack_rows(h)

    wr_hi = wr_hi_ref[...]

    def route_logits(h):
        h_hi, h_lo = _split2(h)
        return _dot(h_hi, wr_hi) + _dot(h_lo, wr_hi) + _dot(h_hi, wr_lo_ref[...])

    logits_all = [route_logits(h) for h in h2]
    lane = lax.broadcasted_iota(jnp.int32, (TRUNK_SUB, LANES), 1)
    big = jnp.int32(LANES)
    is_g = lane < N_GROUPS

    def top2(logits):
        lg = jnp.where(is_g, logits, NEG_BIG)
        mg = jnp.max(lg, axis=-1, keepdims=True)
        g_idx = jnp.min(jnp.where(jnp.logical_and(is_g, lg == mg), lane, big), axis=-1, keepdims=True)
        g_w = 1.0 / jnp.sum(jnp.exp(lg - mg), axis=-1, keepdims=True)
        lo_lane = N_GROUPS + g_idx * EXPERTS_PER_GROUP
        in_grp = jnp.logical_and(lane >= lo_lane, lane < lo_lane + EXPERTS_PER_GROUP)
        le = jnp.where(in_grp, logits, NEG_BIG)
        m1 = jnp.max(le, axis=-1, keepdims=True)
        i1 = jnp.min(jnp.where(jnp.logical_and(in_grp, le == m1), lane, big), axis=-1, keepdims=True)
        le2 = jnp.where(lane == i1, NEG_BIG, le)
        m2 = jnp.max(le2, axis=-1, keepdims=True)
        i2 = jnp.min(jnp.where(jnp.logical_and(in_grp, le2 == m2), lane, big), axis=-1, keepdims=True)
        r2 = jnp.exp(m2 - m1)
        return i1, i2, g_w / (1.0 + r2), g_w * r2 / (1.0 + r2)

    picks = [top2(lg) for lg in logits_all]

    r_i = lax.broadcasted_iota(jnp.int32, (TRUNK_SUB, TRUNK_SUB), 0)
    c_i = lax.broadcasted_iota(jnp.int32, (TRUNK_SUB, TRUNK_SUB), 1)
    tri = jnp.where(c_i < r_i, 1.0, 0.0).astype(BF16)
    oh = [jnp.where(jnp.logical_or(lane == i1, lane == i2), 1.0, 0.0) for i1, i2, _, _ in picks]
    prefix = [_dot(tri, o.astype(BF16)) for o in oh]
    base = run_ref[...]
    for sl, (i1, i2, gate1, gate2), o, pre in zip(slabs, picks, oh, prefix):
        before = pre + base
        rank1 = jnp.sum(jnp.where(lane == i1, before, 0.0), axis=-1, keepdims=True)
        rank2 = jnp.sum(jnp.where(lane == i2, before, 0.0), axis=-1, keepdims=True)
        base = base + jnp.sum(o, axis=0, keepdims=True)
        route = jnp.where(lane == 0, (i1 - N_GROUPS).astype(F32), 0.0)
        for j, val in enumerate(((i2 - N_GROUPS).astype(F32), gate1, gate2, rank1, rank2), start=1):
            route = jnp.where(lane == j, val, route)
        route_ref[sl, :] = route
    run_ref[...] = base
    count_ref[...] = base


def _trunk(x2d, branches, ob, mem_k, mem_v, w_out, norm_mem_q, w_mq, w_mo, norm_ffn,
           w_router_g, w_router_e, bsz, seq):
    tm = TRUNK_TM
    n = bsz * seq
    nblk = seq // tm
    (o1, l1), (o2, l2), (o3, l3) = branches
    wo = w_out.astype(BF16)
    wr = jnp.concatenate([w_router_g, jnp.moveaxis(w_router_e, 0, 1).reshape(D_MODEL, N_EXPERTS)], axis=1)
    wr = jnp.pad(wr, ((0, 0), (0, LANES - ROUTE_COLS)))
    wr_hi, wr_lo = _split2(wr)
    rows = lambda w: pl.BlockSpec((tm, w), lambda b, i: (b * nblk + i, 0))
    full = lambda shape: pl.BlockSpec(shape, lambda b, i: (0, 0))
    memspec = pl.BlockSpec((None, mem_k.shape[1], mem_k.shape[2]), lambda b, i: (b, 0, 0))
    d1, d4, d16 = DILATIONS
    natural = pl.BlockSpec((None, None, tm, SWA_WIDTH), lambda b, i: (b, 0, i, 0))
    strided = lambda d: pl.BlockSpec((None, d, tm // d, SWA_WIDTH), lambda b, i: (b, 0, i, 0))
    branch_specs = [natural, strided(d4), strided(d16)]
    return pl.pallas_call(
        _trunk_body,
        grid=(bsz, nblk),
        in_specs=[rows(D_MODEL)] + branch_specs + branch_specs + [rows(GDN_WIDTH)]
                 + [full((SWA_WIDTH, D_MODEL)), full((GDN_WIDTH, D_MODEL)), full((1, D_MODEL)),
                    full((D_MODEL, D_MODEL)), memspec, memspec, full((D_MODEL, D_MODEL)),
                    full((1, D_MODEL)), full((D_MODEL, LANES)), full((D_MODEL, LANES))],
        out_specs=[rows(D_MODEL), rows(PACKED), rows(LANES), full((1, LANES))],
        out_shape=[jax.ShapeDtypeStruct((n, D_MODEL), F32), jax.ShapeDtypeStruct((n, PACKED), jnp.uint32),
                   jax.ShapeDtypeStruct((n, LANES), F32), jax.ShapeDtypeStruct((1, LANES), F32)],
        scratch_shapes=[pltpu.VMEM((1, LANES), F32), pltpu.VMEM((2 * SWA_WIDTH // LANES, tm, LANES), F32),
                        pltpu.VMEM((2 * SWA_WIDTH // LANES, tm, LANES), F32)],
        compiler_params=_params("arbitrary", "arbitrary"),
        name="trunk",
    )(x2d, o1, o2, o3, l1, l2, l3, ob, wo[:SWA_WIDTH], wo[SWA_WIDTH:], norm_mem_q.reshape(1, D_MODEL),
      w_mq.astype(BF16), mem_k, mem_v, w_mo.astype(BF16), norm_ffn.reshape(1, D_MODEL), wr_hi, wr_lo)


DMA_UNROLL = 8
PACKED = D_MODEL // 2
HI_MASK = 0xFFFF0000


def _pack_rows(x):
    bits = lambda v: lax.bitcast_convert_type(v.astype(BF16).astype(F32), jnp.uint32)
    return (bits(x[:, :PACKED]) & jnp.uint32(HI_MASK)) | (bits(x[:, PACKED:]) >> 16)


def _unpack_rows(u):
    hi = lax.bitcast_convert_type(u & jnp.uint32(HI_MASK), F32)
    lo = lax.bitcast_convert_type(u << 16, F32)
    return hi, lo


def _dispatch_body(dest_ref, h_ref, xs_in_ref, xs_ref, sem, *, tm):
    del xs_in_ref

    def issue(t, c):
        for k in range(TOP_K):
            dst = xs_ref.at[pl.ds(dest_ref[0, 0, t * TOP_K + k], 1)]
            pltpu.make_async_copy(h_ref.at[pl.ds(t, 1)], dst, sem).start(priority=k % 2)
        return c

    lax.fori_loop(0, tm, issue, 0, unroll=DMA_UNROLL)
    for _ in range(TOP_K):
        pltpu.make_async_copy(h_ref, xs_ref.at[pl.ds(0, tm)], sem).wait()


def _dispatch(h2, dest, n_slots, tm=2048):
    n = h2.shape[0]
    dest3 = dest.reshape(n // tm, 1, tm * TOP_K)
    xs0 = jnp.zeros((n_slots, PACKED), jnp.uint32)
    return pl.pallas_call(
        functools.partial(_dispatch_body, tm=tm),
        grid=(n // tm,),
        in_specs=[pl.BlockSpec((1, 1, tm * TOP_K), lambda i: (i, 0, 0), memory_space=pltpu.SMEM),
                  pl.BlockSpec((tm, PACKED), lambda i: (i, 0)),
                  pl.BlockSpec(memory_space=pl.ANY)],
        out_specs=pl.BlockSpec(memory_space=pl.ANY),
        out_shape=jax.ShapeDtypeStruct((n_slots, PACKED), jnp.uint32),
        scratch_shapes=[pltpu.SemaphoreType.DMA(())],
        input_output_aliases={2: 0},
        compiler_params=_params("arbitrary"),
        name="moe_dispatch",
    )(dest3, h2, xs0)


def _expert_body(be_ref, nused_ref, x_ref, wg_ref, wu_ref, wd_ref, y_ref):
    i = pl.program_id(0)

    @pl.when(i < nused_ref[0])
    def _():
        x = jnp.concatenate([half.astype(BF16) for half in _unpack_rows(x_ref[...])], axis=-1)
        a = _dot(x, wg_ref[...])
        b = _dot(x, wu_ref[...])
        hid = (a * jax.nn.sigmoid(a) * b).astype(BF16)
        y_ref[...] = _pack_rows(_dot(hid, wd_ref[...]))

    @pl.when(i >= nused_ref[0])
    def _():
        y_ref[...] = jnp.zeros_like(y_ref)


def _experts(xs, block_expert, n_used, w_gate, w_up, w_down):
    n_slots = xs.shape[0]
    n_blocks = n_slots // ROUTE_BLOCK
    grid_spec = pltpu.PrefetchScalarGridSpec(
        num_scalar_prefetch=2,
        grid=(n_blocks,),
        in_specs=[pl.BlockSpec((ROUTE_BLOCK, PACKED), lambda i, be, nu: (i, 0)),
                  pl.BlockSpec((None, D_MODEL, D_EXPERT), lambda i, be, nu: (be[i], 0, 0)),
                  pl.BlockSpec((None, D_MODEL, D_EXPERT), lambda i, be, nu: (be[i], 0, 0)),
                  pl.BlockSpec((None, D_EXPERT, D_MODEL), lambda i, be, nu: (be[i], 0, 0))],
        out_specs=pl.BlockSpec((ROUTE_BLOCK, PACKED), lambda i, be, nu: (i, 0)),
    )
    return pl.pallas_call(
        _expert_body,
        grid_spec=grid_spec,
        out_shape=jax.ShapeDtypeStruct((n_slots, PACKED), jnp.uint32),
        compiler_params=_params("arbitrary"),
        name="moe_experts",
    )(block_expert, n_used, xs, w_gate, w_up, w_down)


def _combine_body(dest_ref, x_ref, route_ref, g_ref, yb_ref, o_ref, buf_ref, sem, *, tm):
    def issue(t, c):
        for k in range(TOP_K):
            src = yb_ref.at[pl.ds(dest_ref[0, 0, t * TOP_K + k], 1)]
            pltpu.make_async_copy(src, buf_ref.at[k, pl.ds(t, 1)], sem).start(priority=k % 2)
        return c

    lax.fori_loop(0, tm, issue, 0, unroll=DMA_UNROLL)
    for k in range(TOP_K):
        pltpu.make_async_copy(yb_ref.at[pl.ds(0, tm)], buf_ref.at[k], sem).wait()
    route = route_ref[...]
    hi0, lo0 = _unpack_rows(buf_ref[0])
    hi1, lo1 = _unpack_rows(buf_ref[1])
    g0, g1 = route[:, 2:3], route[:, 3:4]
    y = jnp.concatenate([g0 * hi0 + g1 * hi1, g0 * lo0 + g1 * lo1], axis=-1)
    o_ref[...] = _rms(x_ref[...] + y, g_ref[...])


def _combine(x2, route, dest, yb, norm_final, tm=1024):
    n = x2.shape[0]
    dest3 = dest.reshape(n // tm, 1, tm * TOP_K)
    return pl.pallas_call(
        functools.partial(_combine_body, tm=tm),
        grid=(n // tm,),
        in_specs=[pl.BlockSpec((1, 1, tm * TOP_K), lambda i: (i, 0, 0), memory_space=pltpu.SMEM),
                  pl.BlockSpec((tm, D_MODEL), lambda i: (i, 0)),
                  pl.BlockSpec((tm, LANES), lambda i: (i, 0)),
                  pl.BlockSpec((1, D_MODEL), lambda i: (0, 0)),
                  pl.BlockSpec(memory_space=pl.ANY)],
        out_specs=pl.BlockSpec((tm, D_MODEL), lambda i: (i, 0)),
        out_shape=jax.ShapeDtypeStruct((n, D_MODEL), F32),
        scratch_shapes=[pltpu.VMEM((TOP_K, tm, PACKED), jnp.uint32), pltpu.SemaphoreType.DMA(())],
        compiler_params=_params("arbitrary"),
        name="moe_combine",
    )(dest3, x2, route, norm_final.reshape(1, D_MODEL), yb)


def _moe(x2, h2, route, counts, w_gate, w_up, w_down, norm_final):
    n = x2.shape[0]
    m_slots = n * TOP_K
    n_blocks = -(-(m_slots + N_EXPERTS * (ROUTE_BLOCK - 1)) // ROUTE_BLOCK)
    cnt = counts[0, N_GROUPS:N_GROUPS + N_EXPERTS].astype(jnp.int32)
    padded = (cnt + ROUTE_BLOCK - 1) // ROUTE_BLOCK * ROUTE_BLOCK
    pad_end = jnp.cumsum(padded)
    seg_start = pad_end - padded
    eid = route[:, 0:TOP_K].astype(jnp.int32)
    rank = route[:, 4:4 + TOP_K].astype(jnp.int32)
    dest = (seg_start[eid] + rank).reshape(-1)
    block_start = jnp.arange(n_blocks, dtype=jnp.int32) * ROUTE_BLOCK
    block_expert = jnp.minimum(jnp.sum(pad_end[None, :] <= block_start[:, None], axis=1), N_EXPERTS - 1)
    block_expert = block_expert.astype(jnp.int32)
    n_used = (pad_end[-1:] // ROUTE_BLOCK).astype(jnp.int32)
    xs = _dispatch(h2, dest, n_blocks * ROUTE_BLOCK)
    yb = _experts(xs, block_expert, n_used, w_gate, w_up, w_down)
    return _combine(x2, route, dest, yb, norm_final)


GDN_TB = 512
GDN_HALO = 16
GDN_GROUP = 4
GDN_HPG = 2
GDN_GW = GDN_HPG * HEAD_DIM


def _gdn_body(gd_ref, prev_ref, next_ref, abr_ref, cw_ref, alog_ref, dt_ref, *rest, rev, final):
    if final:
        z_ref, oprev_ref, gn_ref, o_ref, xpad_ref, state_ref = rest
    else:
        o_ref, xpad_ref, state_ref = rest
    i = pl.program_id(1)
    nblk = pl.num_programs(1)
    n_chunks = GDN_TB // CHUNK
    n_grp = GDN_HEADS // GDN_HPG
    width3 = 3 * GDN_WIDTH

    @pl.when(i == 0)
    def _():
        state_ref[...] = jnp.zeros_like(state_ref)

    blk = (nblk - 1 - i) if rev else i
    zero_halo = jnp.zeros((GDN_HALO, width3), BF16)
    xpad_ref[:GDN_HALO, :] = jnp.where(blk == 0, zero_halo, prev_ref[...])
    xpad_ref[GDN_HALO:GDN_HALO + GDN_TB, :] = gd_ref[...]
    xpad_ref[GDN_HALO + GDN_TB:, :] = jnp.where(blk == nblk - 1, zero_halo, next_ref[...])
    win = CHUNK + 2 * GDN_HALO
    side_taps = [j for j in range(CONV_K) if j != CONV_K // 2]
    sr = lax.broadcasted_iota(jnp.int32, (len(side_taps) * CHUNK, 1), 0)
    sc = lax.broadcasted_iota(jnp.int32, (1, win), 1)
    tap = sr // CHUNK
    tap = tap + jnp.where(tap >= CONV_K // 2, 1, 0)
    shift_sel = jnp.where(sc == sr % CHUNK + tap + (GDN_HALO - CONV_K // 2), 1.0, 0.0).astype(BF16)

    lane = lax.broadcasted_iota(jnp.int32, (1, GDN_GW), 1)
    tok = lane % CHUNK
    row = lax.broadcasted_iota(jnp.int32, (CHUNK, 1), 0)
    causal = (tok >= row) if rev else (tok <= row)
    strict = (tok > row) if rev else (tok < row)
    eye = jnp.where(tok == row, 1.0, 0.0)
    r2 = lax.broadcasted_iota(jnp.int32, (GDN_GW, 1), 0)
    same_head = (r2 // CHUNK) == (lane // CHUNK)
    bd_ones = jnp.where(same_head, 1.0, 0.0).astype(BF16)
    tri_in = (r2 % CHUNK >= tok) if rev else (r2 % CHUNK <= tok)
    tri_bd = jnp.where(jnp.logical_and(same_head, tri_in), 1.0, 0.0).astype(BF16)
    tri_and_ones = jnp.concatenate([tri_bd, bd_ones], axis=1)
    row8 = lax.broadcasted_iota(jnp.int32, (8, 1), 0)

    def blockdiag(x):
        xb = x.astype(BF16)
        zero = jnp.zeros_like(xb)
        return jnp.concatenate([jnp.where(lane // CHUNK == h, xb, zero) for h in range(GDN_HPG)], axis=0)

    def wide_mm(x, y):
        return _dot(x.astype(BF16), blockdiag(y))

    def head_sums(x):
        return _dot(x.astype(BF16), bd_ones)

    def group(gi, carry):
        g0 = (n_chunks // GDN_GROUP - 1 - gi) if rev else gi
        chunks = [g0 * GDN_GROUP + ((GDN_GROUP - 1 - j) if rev else j) for j in range(GDN_GROUP)]
        units = [(ci, p) for ci in range(GDN_GROUP) for p in range(n_grp)]
        r0 = [pl.multiple_of(c * CHUNK, CHUNK) for c in chunks]

        gates, gc_rows, gl_rows = [], [], []
        for c in chunks:
            graw = abr_ref[c]
            xg = graw + dt_ref[...]
            softplus = jnp.maximum(xg, 0.0) + jnp.log(1.0 + jnp.exp(-jnp.abs(xg)))
            g = jnp.where(row8 < n_grp, jax.nn.sigmoid(graw), -jnp.exp(alog_ref[...]) * softplus)
            g3 = _split3(g)
            gates.append(g)
            sums = sum(_dot(t, tri_and_ones) for t in g3)
            gc_rows.append(sums[:, :GDN_GW])
            gl_rows.append(sums[:, GDN_GW:])

        shifted = [_dot(shift_sel, xpad_ref[pl.ds(r0[ci], win), :]) for ci in range(GDN_GROUP)]

        def conv(ci, p, col0):
            cols = slice(col0 + p * GDN_GW, col0 + (p + 1) * GDN_GW)
            centre = xpad_ref[pl.ds(r0[ci] + GDN_HALO, CHUNK), cols].astype(F32)
            acc = centre * cw_ref[CONV_K // 2:CONV_K // 2 + 1, cols]
            for i, j in enumerate(side_taps):
                acc = acc + shifted[ci][i * CHUNK:(i + 1) * CHUNK, cols] * cw_ref[j:j + 1, cols]
            return acc * jax.nn.sigmoid(acc)

        qkv = {u: [conv(*u, col0) for col0 in (0, GDN_WIDTH, 2 * GDN_WIDTH)] for u in units}
        ss = {u: head_sums(jnp.concatenate([qkv[u][0] * qkv[u][0], qkv[u][1] * qkv[u][1]], axis=0))
              for u in units}
        prep = {}
        for (ci, p) in units:
            beta_r = gates[ci][p:p + 1]
            g_r = gates[ci][n_grp + p:n_grp + p + 1]
            lhs = jnp.concatenate([jnp.where(causal, g_r, 0.0), eye * beta_r], axis=0)
            prep[(ci, p)] = sum(_dot(t, bd_ones) for t in _split2(lhs))

        kq, kn_bd, e_gc, k_upd, decay_b = {}, {}, {}, {}, {}
        for u in units:
            ci, p = u
            qn = qkv[u][0] * lax.rsqrt(ss[u][:CHUNK] + EPS) * (HEAD_DIM ** -0.5)
            kn = qkv[u][1] * lax.rsqrt(ss[u][CHUNK:] + EPS)
            kq[u] = jnp.concatenate([kn, qn], axis=0).astype(BF16)
            kn_bd[u] = blockdiag(kn)
            beta_r = gates[ci][p:p + 1]
            gc_r = gc_rows[ci][n_grp + p:n_grp + p + 1]
            gl_r = gl_rows[ci][n_grp + p:n_grp + p + 1]
            gc_b, beta_b = prep[u][:CHUNK], prep[u][CHUNK:]
            decay_b[u] = jnp.exp(jnp.where(causal, gc_b - gc_r, NEG_BIG)) * beta_r
            e_gc[u] = jnp.exp(gc_b)
            k_upd[u] = (kn * (jnp.exp(gl_r - gc_b) * beta_b)).astype(BF16)

        kk_qk = {u: _dot_nt(kq[u], kn_bd[u]) for u in units}
        lpow = {u: jnp.where(strict, kk_qk[u][:CHUNK] * decay_b[u], 0.0) for u in units}
        amat = {u: kk_qk[u][CHUNK:] * decay_b[u] for u in units}
        n_sq = 5
        lpow = {u: -lpow[u] for u in units}
        tinv = {u: eye + lpow[u] for u in units}
        lpow = {u: wide_mm(lpow[u], lpow[u]) for u in units}
        for k in range(n_sq):
            if k < n_sq - 1:
                prod = {u: _dot(lpow[u].astype(BF16),
                                jnp.concatenate([blockdiag(tinv[u]), blockdiag(lpow[u])], axis=1)) for u in units}
                lpow = {u: prod[u][:, GDN_GW:] for u in units}
            else:
                prod = {u: wide_mm(lpow[u], tinv[u]) for u in units}
            tinv = {u: tinv[u] + prod[u][:, :GDN_GW] for u in units}

        for ci in range(GDN_GROUP):
            us = [(ci, p) for p in range(n_grp)]
            state = {u: state_ref[u[1]] for u in us}
            pq = {u: _dot(kq[u], state[u].astype(BF16)) for u in us}
            vhat = {u: wide_mm(tinv[u], qkv[u][2] - e_gc[u] * pq[u][:CHUNK]) for u in us}
            upd = {u: _dot_tn(k_upd[u], vhat[u].astype(BF16)) for u in us}
            for u in us:
                gl_r = gl_rows[ci][n_grp + u[1]:n_grp + u[1] + 1]
                state_ref[u[1]] = state[u] * jnp.exp(gl_r) + jnp.where(same_head, upd[u], 0.0)
            o = {u: e_gc[u] * pq[u][CHUNK:] + wide_mm(amat[u], vhat[u]) for u in us}
            for u in us:
                cs = slice(u[1] * GDN_GW, (u[1] + 1) * GDN_GW)
                rows = pl.ds(r0[ci], CHUNK)
                if final:
                    ob = o[u] + oprev_ref[rows, cs]
                    ms = head_sums(ob * ob) * (1.0 / HEAD_DIM)
                    zz = z_ref[rows, cs].astype(F32)
                    out = ob * lax.rsqrt(ms + EPS) * gn_ref[:, cs] * (zz * jax.nn.sigmoid(zz))
                    o_ref[rows, cs] = out.astype(o_ref.dtype)
                else:
                    o_ref[rows, cs] = o[u]
        return carry

    lax.fori_loop(0, n_chunks // GDN_GROUP, group, 0)


def _gdn_pass(gd3, abr, conv_w, alog_row, dt_row, rev, final_args=None):
    bsz, seq, width3 = gd3.shape
    nblk = seq // GDN_TB
    hpb = GDN_TB // GDN_HALO
    blk = (lambda i: nblk - 1 - i) if rev else (lambda i: i)
    main = lambda w: pl.BlockSpec((None, GDN_TB, w), lambda b, i: (b, blk(i), 0))
    full = lambda shape: pl.BlockSpec(shape, lambda b, i: (0,) * len(shape))
    in_specs = [main(width3),
                pl.BlockSpec((None, GDN_HALO, width3), lambda b, i: (b, jnp.maximum(blk(i) * hpb - 1, 0), 0)),
                pl.BlockSpec((None, GDN_HALO, width3),
                             lambda b, i: (b, jnp.minimum((blk(i) + 1) * hpb, seq // GDN_HALO - 1), 0)),
                pl.BlockSpec((None, GDN_TB // CHUNK, 8, GDN_GW), lambda b, i: (b, blk(i), 0, 0)),
                full((CONV_K, width3)), full((8, GDN_GW)), full((8, GDN_GW))]
    args = [gd3, gd3, gd3, abr, conv_w, alog_row, dt_row]
    final = final_args is not None
    if final:
        z3, o_prev, gn = final_args
        in_specs += [main(GDN_WIDTH), main(GDN_WIDTH), full((1, GDN_WIDTH))]
        args += [z3, o_prev, gn]
    return pl.pallas_call(
        functools.partial(_gdn_body, rev=rev, final=final),
        grid=(bsz, nblk),
        in_specs=in_specs,
        out_specs=main(GDN_WIDTH),
        out_shape=jax.ShapeDtypeStruct((bsz, seq, GDN_WIDTH), BF16 if final else F32),
        scratch_shapes=[pltpu.VMEM((GDN_TB + 2 * GDN_HALO, width3), BF16),
                        pltpu.VMEM((GDN_HEADS // GDN_HPG, GDN_GW, GDN_GW), F32)],
        compiler_params=_params("parallel", "arbitrary"),
        name="gdn_bwd" if rev else "gdn_fwd",
    )(*args)


def _group_rows(x, bsz, seq):
    x = x.reshape(bsz, seq // CHUNK, CHUNK, GDN_HEADS // GDN_HPG, GDN_HPG)
    return x.transpose(0, 1, 3, 4, 2).reshape(bsz, seq // CHUNK, GDN_HEADS // GDN_HPG, GDN_GW)


def _gdn(gd, z, ab, conv_w, a_log, dt_bias, gdn_norm, bsz, seq):
    gd3 = gd.reshape(bsz, seq, 3 * GDN_WIDTH)
    z3 = z.reshape(bsz, seq, GDN_WIDTH)
    gn = jnp.tile(gdn_norm, GDN_HEADS).reshape(1, GDN_WIDTH)
    out = None
    for d in range(2):
        n_grp = GDN_HEADS // GDN_HPG
        beta = _group_rows(ab[:, d * GDN_HEADS:(d + 1) * GDN_HEADS], bsz, seq)
        araw = _group_rows(ab[:, (2 + d) * GDN_HEADS:(3 + d) * GDN_HEADS], bsz, seq)
        abr = jnp.pad(jnp.concatenate([beta, araw], axis=2), ((0, 0), (0, 0), (0, 8 - 2 * n_grp), (0, 0)))
        per_head = lambda v: jnp.pad(jnp.repeat(v[d], CHUNK).reshape(n_grp, GDN_GW), ((n_grp, 8 - 2 * n_grp), (0, 0)))
        final_args = None if d == 0 else (z3, out, gn)
        out = _gdn_pass(gd3, abr, conv_w, per_head(a_log), per_head(dt_bias), rev=(d == 1), final_args=final_args)
    return out.reshape(bsz * seq, GDN_WIDTH)


def _gdn_scan_jax(q, k, v, g, beta):
    B, T, H, dk = q.shape
    C = CHUNK
    N = T // C
    chunks = lambda t: t.reshape(B, N, C, H, -1).transpose(1, 0, 3, 2, 4)
    qc, kc, vc = chunks(q), chunks(k), chunks(v)
    gc = jnp.cumsum(g.reshape(B, N, C, H).transpose(1, 0, 3, 2), axis=-1)
    bc = beta.reshape(B, N, C, H).transpose(1, 0, 3, 2)[..., None]
    lower = jnp.tril(jnp.ones((C, C), bool))
    strict = jnp.tril(jnp.ones((C, C), bool), -1)
    decay = jnp.exp(jnp.where(lower, gc[..., :, None] - gc[..., None, :], -jnp.inf))
    kbeta = kc * bc
    lmat = jnp.where(strict, jnp.einsum('nbhid,nbhjd->nbhij', kbeta, kc) * decay, 0.0)
    solve = lambda rhs: lax.linalg.triangular_solve(lmat, rhs, left_side=True, lower=True, unit_diagonal=True)
    u = solve(vc * bc)
    w = solve(kbeta * jnp.exp(gc)[..., None])
    a_qk = jnp.einsum('nbhid,nbhjd->nbhij', qc, kc) * decay

    def step(state, inp):
        q_i, k_i, u_i, w_i, g_i, a_i = inp
        v_new = u_i - jnp.einsum('bhck,bhkv->bhcv', w_i, state)
        o_i = (jnp.einsum('bhck,bhkv->bhcv', q_i * jnp.exp(g_i)[..., None], state)
               + jnp.einsum('bhij,bhjv->bhiv', a_i, v_new))
        g_last = g_i[..., -1:]
        state = (state * jnp.exp(g_last)[..., None]
                 + jnp.einsum('bhck,bhcv->bhkv', k_i * jnp.exp(g_last - g_i)[..., None], v_new))
        return state, o_i

    s0 = jnp.zeros((B, H, dk, dk), F32)
    _, o = lax.scan(step, s0, (qc, kc, u, w, gc, a_qk))
    return o.transpose(1, 0, 3, 2, 4).reshape(B, T, H, dk)


def _gdn_jax(gd, z, ab, conv_w, a_log, dt_bias, gdn_norm, bsz, seq):
    c = lax.conv_general_dilated(
        gd.astype(F32).reshape(bsz, seq, 3 * GDN_WIDTH), conv_w[:, None, :], window_strides=(1,),
        padding=[(CONV_K // 2, CONV_K // 2)], dimension_numbers=('NWC', 'WIO', 'NWC'),
        feature_group_count=3 * GDN_WIDTH)
    c = jax.nn.silu(c)
    l2 = lambda t: t * lax.rsqrt(jnp.sum(t * t, axis=-1, keepdims=True) + EPS)
    qb, kb, vb = [t.reshape(bsz, seq, GDN_HEADS, HEAD_DIM) for t in jnp.split(c, 3, axis=-1)]
    qb = l2(qb) * (HEAD_DIM ** -0.5)
    kb = l2(kb)
    ab4 = ab[:, :4 * GDN_HEADS].reshape(bsz, seq, 4, GDN_HEADS)
    beta = jax.nn.sigmoid(ab4[:, :, 0:2])
    g = -jnp.exp(a_log) * jax.nn.softplus(ab4[:, :, 2:4] + dt_bias)
    fwd = _gdn_scan_jax(qb, kb, vb, g[:, :, 0], beta[:, :, 0])
    flip = lambda t: t[:, ::-1]
    bwd = flip(_gdn_scan_jax(flip(qb), flip(kb), flip(vb), flip(g[:, :, 1]), flip(beta[:, :, 1])))
    ob = fwd + bwd
    ob = (ob * lax.rsqrt(jnp.mean(ob * ob, axis=-1, keepdims=True) + EPS) * gdn_norm
          * jax.nn.silu(z.astype(F32).reshape(bsz, seq, GDN_HEADS, HEAD_DIM)))
    return ob.reshape(bsz * seq, GDN_WIDTH).astype(BF16)


def _encoder(x, mem, norm_mix, w_in, conv_w, a_log, dt_bias, gdn_norm, w_out, norm_mem_q, norm_mem_kv,
             w_mq, w_mkv, w_mo, norm_ffn, w_router_g, w_router_e, w_gate, w_up, w_down, norm_final):
    bsz, seq, _ = x.shape
    x2d = x.reshape(bsz * seq, D_MODEL)
    qkv1, qkv4, qkv16, gd, z, ab = _inproj(x2d, norm_mix, w_in, bsz, seq)
    slopes = jnp.exp2(-8.0 * jnp.arange(1, SWA_HEADS + 1, dtype=F32) / SWA_HEADS)
    branches = [_attn_branch(qkv, slopes, d) for qkv, d in zip((qkv1, qkv4, qkv16), DILATIONS)]
    ob = _gdn(gd, z, ab, conv_w, a_log, dt_bias, gdn_norm, bsz, seq)
    mem_k, mem_v = _memkv(mem.reshape(-1, D_MODEL), norm_mem_kv, w_mkv)
    mem_k = mem_k.reshape(bsz, -1, MEM_HEADS * MEM_HEAD_DIM)
    mem_v = mem_v.reshape(bsz, -1, MEM_HEADS * MEM_HEAD_DIM)
    x2, h2, route, counts = _trunk(x2d, branches, ob, mem_k, mem_v, w_out, norm_mem_q, w_mq, w_mo,
                                   norm_ffn, w_router_g, w_router_e, bsz, seq)
    y = _moe(x2, h2, route, counts, w_gate, w_up, w_down, norm_final)
    return y.reshape(bsz, seq, D_MODEL)


def kernel(x_prompt, x_sample, mem_prompt, mem_sample, norm_mix, w_in, conv_w, a_log, dt_bias, gdn_norm,
           w_out, norm_mem_q, norm_mem_kv, w_mq, w_mkv, w_mo, norm_ffn, w_router_g, w_router_e,
           w_gate, w_up, w_down, norm_final):
    p = dict(norm_mix=norm_mix[0], w_in=w_in[0], conv_w=conv_w[0], a_log=a_log[0], dt_bias=dt_bias[0],
             gdn_norm=gdn_norm[0], w_out=w_out[0], norm_mem_q=norm_mem_q[0], norm_mem_kv=norm_mem_kv[0],
             w_mq=w_mq[0], w_mkv=w_mkv[0], w_mo=w_mo[0], norm_ffn=norm_ffn[0], w_router_g=w_router_g[0],
             w_router_e=w_router_e[0], w_gate=w_gate[0].astype(BF16), w_up=w_up[0].astype(BF16),
             w_down=w_down[0].astype(BF16), norm_final=norm_final)
    return (_encoder(x_prompt, mem_prompt, **p), _encoder(x_sample, mem_sample, **p))
```

```python
import functools

import jax
import jax.numpy as jnp
import numpy as np
from jax import lax
from jax.experimental import pallas as pl
from jax.experimental.pallas import tpu as pltpu

F32 = jnp.float32
BF16 = jnp.bfloat16

D_MODEL = 1024
HEAD_DIM = 64
SWA_HEADS = 8
GDN_HEADS = 8
SWA_WIDTH = SWA_HEADS * HEAD_DIM
GDN_WIDTH = GDN_HEADS * HEAD_DIM
DILATIONS = (1, 4, 16)
ATT_W = 64
CONV_K = 5
CHUNK = 64
MEM_HEADS = 4
MEM_HEAD_DIM = 256
N_GROUPS = 4
EXPERTS_PER_GROUP = 8
N_EXPERTS = 32
TOP_K = 2
D_EXPERT = 512
ROUTE_BLOCK = 512
EPS = 1e-6

LANES = 128
VMEM_LIMIT = 56 * 1024 * 1024
NEG_BIG = -1e30


def _dot(a, b):
    return jnp.dot(a, b, preferred_element_type=F32)


def _dot_nt(a, b):
    return lax.dot_general(a, b, (((1,), (1,)), ((), ())), preferred_element_type=F32)


def _dot_tn(a, b):
    return lax.dot_general(a, b, (((0,), (0,)), ((), ())), preferred_element_type=F32)


def _split2(x):
    hi = x.astype(BF16)
    lo = (x - hi.astype(F32)).astype(BF16)
    return hi, lo


def _split3(x):
    hi = x.astype(BF16)
    r = x - hi.astype(F32)
    mid = r.astype(BF16)
    lo = (r - mid.astype(F32)).astype(BF16)
    return hi, mid, lo


def _rms(x, g):
    return x * lax.rsqrt(jnp.mean(x * x, axis=-1, keepdims=True) + EPS) * g


def _params(*sem):
    return pltpu.CompilerParams(dimension_semantics=sem, vmem_limit_bytes=VMEM_LIMIT)


def _inproj_body(x_ref, g_ref, wa_ref, wgd_ref, wz_ref, wab_hi_ref, wab_lo_ref,
                 qkv1_ref, qkv4_ref, qkv16_ref, gd_ref, z_ref, ab_ref, stage_ref, *, tm):
    h = _rms(x_ref[...], g_ref[...])
    h_hi, h_lo = _split2(h)
    qkv = _dot(h_hi, wa_ref[...])
    qkv1_ref[...] = qkv.astype(BF16)
    for c in range(qkv.shape[1] // LANES):
        cols = slice(c * LANES, (c + 1) * LANES)
        stage_ref[c] = qkv[:, cols]
        for dil, out_ref in ((DILATIONS[1], qkv4_ref), (DILATIONS[2], qkv16_ref)):
            for r in range(dil):
                out_ref[r, :, cols] = stage_ref[c, pl.ds(r, tm // dil, stride=dil), :].astype(BF16)
    gd_ref[...] = _dot(h_hi, wgd_ref[...]).astype(BF16)
    z_ref[...] = _dot(h_hi, wz_ref[...]).astype(BF16)
    wab_hi = wab_hi_ref[...]
    ab = _dot(h_hi, wab_hi) + _dot(h_lo, wab_hi) + _dot(h_hi, wab_lo_ref[...])
    ab_ref[...] = ab


def _inproj(x2, norm_mix, w_in, bsz, seq, tm=512):
    n = x2.shape[0]
    nblk = seq // tm
    width = 3 * SWA_WIDTH
    wa = w_in[:, :3 * SWA_WIDTH].astype(BF16)
    wgd = w_in[:, 3 * SWA_WIDTH:3 * SWA_WIDTH + 3 * GDN_WIDTH].astype(BF16)
    wz = w_in[:, 3 * SWA_WIDTH + 3 * GDN_WIDTH:3 * SWA_WIDTH + 4 * GDN_WIDTH].astype(BF16)
    wab = jnp.pad(w_in[:, 3 * SWA_WIDTH + 4 * GDN_WIDTH:], ((0, 0), (0, LANES - 4 * GDN_HEADS)))
    wab_hi, wab_lo = _split2(wab)
    full = lambda shape: pl.BlockSpec(shape, lambda b, i: (0, 0))
    rows = lambda w: pl.BlockSpec((tm, w), lambda b, i: (b * nblk + i, 0))
    strided = lambda d: pl.BlockSpec((None, d, tm // d, width), lambda b, i: (b, 0, i, 0))
    d1, d4, d16 = DILATIONS
    return pl.pallas_call(
        functools.partial(_inproj_body, tm=tm),
        grid=(bsz, nblk),
        in_specs=[rows(D_MODEL), full((1, D_MODEL)), full(wa.shape), full(wgd.shape), full(wz.shape),
                  full(wab_hi.shape), full(wab_lo.shape)],
        out_specs=[pl.BlockSpec((None, None, tm, width), lambda b, i: (b, 0, i, 0)), strided(d4), strided(d16),
                   rows(3 * GDN_WIDTH), rows(GDN_WIDTH), rows(LANES)],
        out_shape=[jax.ShapeDtypeStruct((bsz, d, seq // d, width), BF16) for d in (d1, d4, d16)]
                  + [jax.ShapeDtypeStruct((n, 3 * GDN_WIDTH), BF16),
                     jax.ShapeDtypeStruct((n, GDN_WIDTH), BF16),
                     jax.ShapeDtypeStruct((n, LANES), F32)],
        scratch_shapes=[pltpu.VMEM((width // LANES, tm, LANES), F32)],
        compiler_params=_params("parallel", "parallel"),
        name="inproj",
    )(x2, norm_mix.reshape(1, D_MODEL), wa, wgd, wz, wab_hi, wab_lo)


ATT_QB = 128
ATT_KB = ATT_QB + 2 * ATT_W
ATT_GROUP = 8


def _attn_bias(slopes, dil):
    row = jnp.arange(ATT_QB)[:, None]
    col = jnp.arange(ATT_KB)[None, :]
    rel = jnp.stack([jnp.abs(col - row - var * ATT_W) for var in range(3)])
    dist = (rel * dil).astype(F32)
    bias = jnp.where(rel <= ATT_W, -slopes[:, None, None, None] * dist, NEG_BIG)
    bias = bias.reshape(SWA_HEADS // 2, 2, 3, ATT_QB, ATT_KB).transpose(0, 2, 1, 3, 4)
    return bias.reshape(SWA_HEADS // 2, 6, ATT_QB, ATT_KB)


def _attn_body(bias_ref, q_ref, k_ref, v_ref, o_ref, lse_ref, *, n_qb, seq_l, n_pw):
    lb = pl.program_id(3)
    lane = lax.broadcasted_iota(jnp.int32, (1, LANES), 1)
    left = lane < HEAD_DIM
    group = min(ATT_GROUP, n_qb)

    def qgroup(gi, carry):
        rows, var, kb, vb, q, pw_of, cols = [], [], [], [], [], [], []
        for j in range(group):
            qi = gi * group + j
            n0 = (lb * n_qb + qi) * ATT_QB
            kstart = pl.multiple_of(jnp.clip(n0 - ATT_W, 0, seq_l - ATT_KB), ATT_W)
            for pw in range(n_pw):
                cs = slice(pw * LANES, (pw + 1) * LANES)
                var.append((n0 - kstart) // ATT_W)
                rows.append(pl.ds(pl.multiple_of(qi * ATT_QB, ATT_QB), ATT_QB))
                q.append(q_ref[rows[-1], cs] * jnp.asarray(HEAD_DIM ** -0.5, BF16))
                kb.append(k_ref[pl.ds(kstart, ATT_KB), cs])
                vb.append(v_ref[pl.ds(kstart, ATT_KB), cs])
                pw_of.append(pw)
                cols.append(cs)
        blocks = range(group * n_pw)
        units = [(j, h) for j in blocks for h in range(2)]
        mine = [left, jnp.logical_not(left)]
        s = {(j, h): _dot_nt(jnp.where(mine[h], q[j], jnp.zeros_like(q[j])), kb[j])
                     + bias_ref[pw_of[j], var[j] * 2 + h] for (j, h) in units}
        m = {u: jnp.max(s[u], axis=-1, keepdims=True) for u in units}
        p = {u: jnp.exp(s[u] - m[u]).astype(BF16) for u in units}
        acc = {(j, h): _dot(p[(j, h)], jnp.where(mine[h], vb[j], jnp.ones_like(vb[j]))) for (j, h) in units}
        for j in blocks:
            num = jnp.where(left, acc[(j, 0)], acc[(j, 1)])
            den = pltpu.roll(jnp.where(left, acc[(j, 1)], acc[(j, 0)]), HEAD_DIM, 1)
            mx = jnp.where(left, m[(j, 0)], m[(j, 1)])
            o_ref[rows[j], cols[j]] = (num / den).astype(BF16)
            lse_ref[rows[j], cols[j]] = mx + jnp.log(den)
        return carry

    lax.fori_loop(0, n_qb // group, qgroup, 0)


def _attn_branch(qkv, slopes, dil):
    bsz, _, seq_l, _ = qkv.shape
    lblk = min(2048, seq_l)
    n_qb = lblk // ATT_QB
    n_pw = 1 if n_qb >= ATT_GROUP else 2
    n_steps = SWA_HEADS // 2 // n_pw
    width = n_pw * LANES
    qspec = pl.BlockSpec((None, None, lblk, width), lambda p, b, r, l: (b, r, l, p))
    kspec = pl.BlockSpec((None, None, seq_l, width), lambda p, b, r, l: (b, r, 0, n_steps + p))
    vspec = pl.BlockSpec((None, None, seq_l, width), lambda p, b, r, l: (b, r, 0, 2 * n_steps + p))
    ospec = pl.BlockSpec((None, None, lblk, width), lambda p, b, r, l: (b, r, l, p))
    bspec = pl.BlockSpec((n_pw, 6, ATT_QB, ATT_KB), lambda p, b, r, l: (p, 0, 0, 0))
    return pl.pallas_call(
        functools.partial(_attn_body, n_qb=n_qb, seq_l=seq_l, n_pw=n_pw),
        grid=(n_steps, bsz, dil, seq_l // lblk),
        in_specs=[bspec, qspec, kspec, vspec],
        out_specs=[ospec, ospec],
        out_shape=[jax.ShapeDtypeStruct((bsz, dil, seq_l, SWA_WIDTH), BF16),
                   jax.ShapeDtypeStruct((bsz, dil, seq_l, SWA_WIDTH), F32)],
        compiler_params=_params("parallel", "parallel", "parallel", "arbitrary"),
        name=f"dilated_attn_d{dil}",
    )(_attn_bias(slopes, dil), qkv, qkv, qkv)


def _memkv_body(m_ref, g_ref, w_ref, k_ref, v_ref):
    h = _rms(m_ref[...], g_ref[...]).astype(BF16)
    kv = _dot(h, w_ref[...])
    width = MEM_HEADS * MEM_HEAD_DIM
    k_ref[...] = (kv[:, :width] * (MEM_HEAD_DIM ** -0.5)).astype(BF16)
    v_ref[...] = kv[:, width:].astype(BF16)


def _memkv(mem2, norm_kv, w_mkv, tm=256):
    n = mem2.shape[0]
    width = MEM_HEADS * MEM_HEAD_DIM
    return pl.pallas_call(
        _memkv_body,
        grid=(n // tm,),
        in_specs=[pl.BlockSpec((tm, D_MODEL), lambda i: (i, 0)),
                  pl.BlockSpec((1, D_MODEL), lambda i: (0, 0)),
                  pl.BlockSpec((D_MODEL, 2 * width), lambda i: (0, 0))],
        out_specs=[pl.BlockSpec((tm, width), lambda i: (i, 0))] * 2,
        out_shape=[jax.ShapeDtypeStruct((n, width), BF16)] * 2,
        compiler_params=_params("parallel"),
        name="mem_kv",
    )(mem2, norm_kv.reshape(1, D_MODEL), w_mkv.astype(BF16))


ROUTE_COLS = N_GROUPS + N_EXPERTS


TRUNK_TM = 512
TRUNK_SUB = 256


def _trunk_body(x_ref, o1_ref, o2_ref, o3_ref, l1_ref, l2_ref, l3_ref, ob_ref,
                wo_a_ref, wo_b_ref, gq_ref, wq_ref, k_ref, v_ref, wmo_ref,
                gf_ref, wr_hi_ref, wr_lo_ref,
                x2_ref, h2_ref, route_ref, count_ref, run_ref, obuf_ref, lbuf_ref):
    first = jnp.logical_and(pl.program_id(0) == 0, pl.program_id(1) == 0)

    @pl.when(first)
    def _():
        run_ref[...] = jnp.zeros_like(run_ref)

    n_cb = SWA_WIDTH // LANES
    for bi, (o_ref, l_ref) in enumerate(((o2_ref, l2_ref), (o3_ref, l3_ref))):
        dil = DILATIONS[bi + 1]
        for r in range(dil):
            rows = pl.ds(r, TRUNK_TM // dil, stride=dil)
            o_r = o_ref[r].astype(F32)
            l_r = l_ref[r]
            for c in range(n_cb):
                obuf_ref[bi * n_cb + c, rows, :] = o_r[:, c * LANES:(c + 1) * LANES]
                lbuf_ref[bi * n_cb + c, rows, :] = l_r[:, c * LANES:(c + 1) * LANES]

    slabs = [slice(j * TRUNK_SUB, (j + 1) * TRUNK_SUB) for j in range(TRUNK_TM // TRUNK_SUB)]

    def token_major(buf_ref, bi, sl):
        return jnp.concatenate([buf_ref[bi * n_cb + c, sl, :] for c in range(n_cb)], axis=-1)

    def merge(sl):
        l1, l2, l3 = l1_ref[sl, :], token_major(lbuf_ref, 0, sl), token_major(lbuf_ref, 1, sl)
        mx = jnp.maximum(jnp.maximum(l1, l2), l3)
        e1, e2, e3 = jnp.exp(l1 - mx), jnp.exp(l2 - mx), jnp.exp(l3 - mx)
        ya = (e1 * o1_ref[sl, :].astype(F32) + e2 * token_major(obuf_ref, 0, sl)
              + e3 * token_major(obuf_ref, 1, sl))
        return (ya / (e1 + e2 + e3)).astype(BF16)

    ya = [merge(sl) for sl in slabs]
    x1 = [x_ref[sl, :] + _dot(y, wo_a_ref[...]) + _dot(ob_ref[sl, :], wo_b_ref[...]) for y, sl in zip(ya, slabs)]

    q = [_dot(_rms(x, gq_ref[...]).astype(BF16), wq_ref[...]).astype(BF16) for x in x1]
    cols = [slice(h * MEM_HEAD_DIM, (h + 1) * MEM_HEAD_DIM) for h in range(MEM_HEADS)]
    units = [(j, cs) for j in range(len(slabs)) for cs in cols]
    s = [_dot_nt(q[j][:, cs], k_ref[:, cs]) for j, cs in units]
    p = [jnp.exp(sh - jnp.max(sh, axis=-1, keepdims=True)) for sh in s]
    den = [jnp.sum(ph, axis=-1, keepdims=True) for ph in p]
    pv = [_dot(ph.astype(BF16), v_ref[:, cs]) for ph, (j, cs) in zip(p, units)]
    heads = [(a / d).astype(BF16) for a, d in zip(pv, den)]
    nh = MEM_HEADS
    x2 = [x + _dot(jnp.concatenate(heads[j * nh:(j + 1) * nh], axis=-1), wmo_ref[...]) for j, x in enumerate(x1)]
    h2 = [_rms(x, gf_ref[...]) for x in x2]
    for sl, x, h in zip(slabs, x2, h2):
        x2_ref[sl, :] = x
        h2_ref[sl, :] = _pack_rows(h)

    wr_hi = wr_hi_ref[...]

    def route_logits(h):
        h_hi, h_lo = _split2(h)
        return _dot(h_hi, wr_hi) + _dot(h_lo, wr_hi) + _dot(h_hi, wr_lo_ref[...])

    logits_all = [route_logits(h) for h in h2]
    lane = lax.broadcasted_iota(jnp.int32, (TRUNK_SUB, LANES), 1)
    big = jnp.int32(LANES)
    is_g = lane < N_GROUPS

    def top2(logits):
        lg = jnp.where(is_g, logits, NEG_BIG)
        mg = jnp.max(lg, axis=-1, keepdims=True)
        g_idx = jnp.min(jnp.where(jnp.logical_and(is_g, lg == mg), lane, big), axis=-1, keepdims=True)
        g_w = 1.0 / jnp.sum(jnp.exp(lg - mg), axis=-1, keepdims=True)
        lo_lane = N_GROUPS + g_idx * EXPERTS_PER_GROUP
        in_grp = jnp.logical_and(lane >= lo_lane, lane < lo_lane + EXPERTS_PER_GROUP)
        le = jnp.where(in_grp, logits, NEG_BIG)
        m1 = jnp.max(le, axis=-1, keepdims=True)
        i1 = jnp.min(jnp.where(jnp.logical_and(in_grp, le == m1), lane, big), axis=-1, keepdims=True)
        le2 = jnp.where(lane == i1, NEG_BIG, le)
        m2 = jnp.max(le2, axis=-1, keepdims=True)
        i2 = jnp.min(jnp.where(jnp.logical_and(in_grp, le2 == m2), lane, big), axis=-1, keepdims=True)
        r2 = jnp.exp(m2 - m1)
        return i1, i2, g_w / (1.0 + r2), g_w * r2 / (1.0 + r2)

    picks = [top2(lg) for lg in logits_all]

    r_i = lax.broadcasted_iota(jnp.int32, (TRUNK_SUB, TRUNK_SUB), 0)
    c_i = lax.broadcasted_iota(jnp.int32, (TRUNK_SUB, TRUNK_SUB), 1)
    tri = jnp.where(c_i < r_i, 1.0, 0.0).astype(BF16)
    oh = [jnp.where(jnp.logical_or(lane == i1, lane == i2), 1.0, 0.0) for i1, i2, _, _ in picks]
    prefix = [_dot(tri, o.astype(BF16)) for o in oh]
    base = run_ref[...]
    for sl, (i1, i2, gate1, gate2), o, pre in zip(slabs, picks, oh, prefix):
        before = pre + base
        rank1 = jnp.sum(jnp.where(lane == i1, before, 0.0), axis=-1, keepdims=True)
        rank2 = jnp.sum(jnp.where(lane == i2, before, 0.0), axis=-1, keepdims=True)
        base = base + jnp.sum(o, axis=0, keepdims=True)
        route = jnp.where(lane == 0, (i1 - N_GROUPS).astype(F32), 0.0)
        for j, val in enumerate(((i2 - N_GROUPS).astype(F32), gate1, gate2, rank1, rank2), start=1):
            route = jnp.where(lane == j, val, route)
        route_ref[sl, :] = route
    run_ref[...] = base
    count_ref[...] = base


def _trunk(x2d, branches, ob, mem_k, mem_v, w_out, norm_mem_q, w_mq, w_mo, norm_ffn,
           w_router_g, w_router_e, bsz, seq):
    tm = TRUNK_TM
    n = bsz * seq
    nblk = seq // tm
    (o1, l1), (o2, l2), (o3, l3) = branches
    wo = w_out.astype(BF16)
    wr = jnp.concatenate([w_router_g, jnp.moveaxis(w_router_e, 0, 1).reshape(D_MODEL, N_EXPERTS)], axis=1)
    wr = jnp.pad(wr, ((0, 0), (0, LANES - ROUTE_COLS)))
    wr_hi, wr_lo = _split2(wr)
    rows = lambda w: pl.BlockSpec((tm, w), lambda b, i: (b * nblk + i, 0))
    full = lambda shape: pl.BlockSpec(shape, lambda b, i: (0, 0))
    memspec = pl.BlockSpec((None, mem_k.shape[1], mem_k.shape[2]), lambda b, i: (b, 0, 0))
    d1, d4, d16 = DILATIONS
    natural = pl.BlockSpec((None, None, tm, SWA_WIDTH), lambda b, i: (b, 0, i, 0))
    strided = lambda d: pl.BlockSpec((None, d, tm // d, SWA_WIDTH), lambda b, i: (b, 0, i, 0))
    branch_specs = [natural, strided(d4), strided(d16)]
    return pl.pallas_call(
        _trunk_body,
        grid=(bsz, nblk),
        in_specs=[rows(D_MODEL)] + branch_specs + branch_specs + [rows(GDN_WIDTH)]
                 + [full((SWA_WIDTH, D_MODEL)), full((GDN_WIDTH, D_MODEL)), full((1, D_MODEL)),
                    full((D_MODEL, D_MODEL)), memspec, memspec, full((D_MODEL, D_MODEL)),
                    full((1, D_MODEL)), full((D_MODEL, LANES)), full((D_MODEL, LANES))],
        out_specs=[rows(D_MODEL), rows(PACKED), rows(LANES), full((1, LANES))],
        out_shape=[jax.ShapeDtypeStruct((n, D_MODEL), F32), jax.ShapeDtypeStruct((n, PACKED), jnp.uint32),
                   jax.ShapeDtypeStruct((n, LANES), F32), jax.ShapeDtypeStruct((1, LANES), F32)],
        scratch_shapes=[pltpu.VMEM((1, LANES), F32), pltpu.VMEM((2 * SWA_WIDTH // LANES, tm, LANES), F32),
                        pltpu.VMEM((2 * SWA_WIDTH // LANES, tm, LANES), F32)],
        compiler_params=_params("arbitrary", "arbitrary"),
        name="trunk",
    )(x2d, o1, o2, o3, l1, l2, l3, ob, wo[:SWA_WIDTH], wo[SWA_WIDTH:], norm_mem_q.reshape(1, D_MODEL),
      w_mq.astype(BF16), mem_k, mem_v, w_mo.astype(BF16), norm_ffn.reshape(1, D_MODEL), wr_hi, wr_lo)


DMA_UNROLL = 8
PACKED = D_MODEL // 2
HI_MASK = 0xFFFF0000


def _pack_rows(x):
    bits = lambda v: lax.bitcast_convert_type(v.astype(BF16).astype(F32), jnp.uint32)
    return (bits(x[:, :PACKED]) & jnp.uint32(HI_MASK)) | (bits(x[:, PACKED:]) >> 16)


def _unpack_rows(u):
    hi = lax.bitcast_convert_type(u & jnp.uint32(HI_MASK), F32)
    lo = lax.bitcast_convert_type(u << 16, F32)
    return hi, lo


def _dispatch_body(dest_ref, h_ref, xs_in_ref, xs_ref, sem, *, tm):
    del xs_in_ref

    def issue(t, c):
        for k in range(TOP_K):
            dst = xs_ref.at[pl.ds(dest_ref[0, 0, t * TOP_K + k], 1)]
            pltpu.make_async_copy(h_ref.at[pl.ds(t, 1)], dst, sem).start(priority=k % 2)
        return c

    lax.fori_loop(0, tm, issue, 0, unroll=DMA_UNROLL)
    for _ in range(TOP_K):
        pltpu.make_async_copy(h_ref, xs_ref.at[pl.ds(0, tm)], sem).wait()


def _dispatch(h2, dest, n_slots, tm=2048):
    n = h2.shape[0]
    dest3 = dest.reshape(n // tm, 1, tm * TOP_K)
    xs0 = jnp.zeros((n_slots, PACKED), jnp.uint32)
    return pl.pallas_call(
        functools.partial(_dispatch_body, tm=tm),
        grid=(n // tm,),
        in_specs=[pl.BlockSpec((1, 1, tm * TOP_K), lambda i: (i, 0, 0), memory_space=pltpu.SMEM),
                  pl.BlockSpec((tm, PACKED), lambda i: (i, 0)),
                  pl.BlockSpec(memory_space=pl.ANY)],
        out_specs=pl.BlockSpec(memory_space=pl.ANY),
        out_shape=jax.ShapeDtypeStruct((n_slots, PACKED), jnp.uint32),
        scratch_shapes=[pltpu.SemaphoreType.DMA(())],
        input_output_aliases={2: 0},
        compiler_params=_params("arbitrary"),
        name="moe_dispatch",
    )(dest3, h2, xs0)


def _expert_body(be_ref, nused_ref, x_ref, wg_ref, wu_ref, wd_ref, y_ref):
    i = pl.program_id(0)

    @pl.when(i < nused_ref[0])
    def _():
        x = jnp.concatenate([half.astype(BF16) for half in _unpack_rows(x_ref[...])], axis=-1)
        a = _dot(x, wg_ref[...])
        b = _dot(x, wu_ref[...])
        hid = (a * jax.nn.sigmoid(a) * b).astype(BF16)
        y_ref[...] = _pack_rows(_dot(hid, wd_ref[...]))

    @pl.when(i >= nused_ref[0])
    def _():
        y_ref[...] = jnp.zeros_like(y_ref)


def _experts(xs, block_expert, n_used, w_gate, w_up, w_down):
    n_slots = xs.shape[0]
    n_blocks = n_slots // ROUTE_BLOCK
    grid_spec = pltpu.PrefetchScalarGridSpec(
        num_scalar_prefetch=2,
        grid=(n_blocks,),
        in_specs=[pl.BlockSpec((ROUTE_BLOCK, PACKED), lambda i, be, nu: (i, 0)),
                  pl.BlockSpec((None, D_MODEL, D_EXPERT), lambda i, be, nu: (be[i], 0, 0)),
                  pl.BlockSpec((None, D_MODEL, D_EXPERT), lambda i, be, nu: (be[i], 0, 0)),
                  pl.BlockSpec((None, D_EXPERT, D_MODEL), lambda i, be, nu: (be[i], 0, 0))],
        out_specs=pl.BlockSpec((ROUTE_BLOCK, PACKED), lambda i, be, nu: (i, 0)),
    )
    return pl.pallas_call(
        _expert_body,
        grid_spec=grid_spec,
        out_shape=jax.ShapeDtypeStruct((n_slots, PACKED), jnp.uint32),
        compiler_params=_params("arbitrary"),
        name="moe_experts",
    )(block_expert, n_used, xs, w_gate, w_up, w_down)


def _combine_body(dest_ref, x_ref, route_ref, g_ref, yb_ref, o_ref, buf_ref, sem, *, tm):
    def issue(t, c):
        for k in range(TOP_K):
            src = yb_ref.at[pl.ds(dest_ref[0, 0, t * TOP_K + k], 1)]
            pltpu.make_async_copy(src, buf_ref.at[k, pl.ds(t, 1)], sem).start(priority=k % 2)
        return c

    lax.fori_loop(0, tm, issue, 0, unroll=DMA_UNROLL)
    for k in range(TOP_K):
        pltpu.make_async_copy(yb_ref.at[pl.ds(0, tm)], buf_ref.at[k], sem).wait()
    route = route_ref[...]
    hi0, lo0 = _unpack_rows(buf_ref[0])
    hi1, lo1 = _unpack_rows(buf_ref[1])
    g0, g1 = route[:, 2:3], route[:, 3:4]
    y = jnp.concatenate([g0 * hi0 + g1 * hi1, g0 * lo0 + g1 * lo1], axis=-1)
    o_ref[...] = _rms(x_ref[...] + y, g_ref[...])


def _combine(x2, route, dest, yb, norm_final, tm=1024):
    n = x2.shape[0]
    dest3 = dest.reshape(n // tm, 1, tm * TOP_K)
    return pl.pallas_call(
        functools.partial(_combine_body, tm=tm),
        grid=(n // tm,),
        in_specs=[pl.BlockSpec((1, 1, tm * TOP_K), lambda i: (i, 0, 0), memory_space=pltpu.SMEM),
                  pl.BlockSpec((tm, D_MODEL), lambda i: (i, 0)),
                  pl.BlockSpec((tm, LANES), lambda i: (i, 0)),
                  pl.BlockSpec((1, D_MODEL), lambda i: (0, 0)),
                  pl.BlockSpec(memory_space=pl.ANY)],
        out_specs=pl.BlockSpec((tm, D_MODEL), lambda i: (i, 0)),
        out_shape=jax.ShapeDtypeStruct((n, D_MODEL), F32),
        scratch_shapes=[pltpu.VMEM((TOP_K, tm, PACKED), jnp.uint32), pltpu.SemaphoreType.DMA(())],
        compiler_params=_params("arbitrary"),
        name="moe_combine",
    )(dest3, x2, route, norm_final.reshape(1, D_MODEL), yb)


def _moe(x2, h2, route, counts, w_gate, w_up, w_down, norm_final):
    n = x2.shape[0]
    m_slots = n * TOP_K
    n_blocks = -(-(m_slots + N_EXPERTS * (ROUTE_BLOCK - 1)) // ROUTE_BLOCK)
    cnt = counts[0, N_GROUPS:N_GROUPS + N_EXPERTS].astype(jnp.int32)
    padded = (cnt + ROUTE_BLOCK - 1) // ROUTE_BLOCK * ROUTE_BLOCK
    pad_end = jnp.cumsum(padded)
    seg_start = pad_end - padded
    eid = route[:, 0:TOP_K].astype(jnp.int32)
    rank = route[:, 4:4 + TOP_K].astype(jnp.int32)
    dest = (seg_start[eid] + rank).reshape(-1)
    block_start = jnp.arange(n_blocks, dtype=jnp.int32) * ROUTE_BLOCK
    block_expert = jnp.minimum(jnp.sum(pad_end[None, :] <= block_start[:, None], axis=1), N_EXPERTS - 1)
    block_expert = block_expert.astype(jnp.int32)
    n_used = (pad_end[-1:] // ROUTE_BLOCK).astype(jnp.int32)
    xs = _dispatch(h2, dest, n_blocks * ROUTE_BLOCK)
    yb = _experts(xs, block_expert, n_used, w_gate, w_up, w_down)
    return _combine(x2, route, dest, yb, norm_final)


GDN_TB = 1024
GDN_HALO = 16
GDN_GROUP = 4
GDN_HPG = 2
GDN_GW = GDN_HPG * HEAD_DIM


def _gdn_body(gd_ref, prev_ref, next_ref, abr_ref, cw_ref, alog_ref, dt_ref, *rest, rev, final):
    if final:
        z_ref, oprev_ref, gn_ref, o_ref, xpad_ref, state_ref = rest
    else:
        o_ref, xpad_ref, state_ref = rest
    i = pl.program_id(1)
    nblk = pl.num_programs(1)
    n_chunks = GDN_TB // CHUNK
    n_grp = GDN_HEADS // GDN_HPG
    width3 = 3 * GDN_WIDTH

    @pl.when(i == 0)
    def _():
        state_ref[...] = jnp.zeros_like(state_ref)

    blk = (nblk - 1 - i) if rev else i
    zero_halo = jnp.zeros((GDN_HALO, width3), BF16)
    xpad_ref[:GDN_HALO, :] = jnp.where(blk == 0, zero_halo, prev_ref[...])
    xpad_ref[GDN_HALO:GDN_HALO + GDN_TB, :] = gd_ref[...]
    xpad_ref[GDN_HALO + GDN_TB:, :] = jnp.where(blk == nblk - 1, zero_halo, next_ref[...])
    win = CHUNK + 2 * GDN_HALO
    side_taps = [j for j in range(CONV_K) if j != CONV_K // 2]
    sr = lax.broadcasted_iota(jnp.int32, (len(side_taps) * CHUNK, 1), 0)
    sc = lax.broadcasted_iota(jnp.int32, (1, win), 1)
    tap = sr // CHUNK
    tap = tap + jnp.where(tap >= CONV_K // 2, 1, 0)
    shift_sel = jnp.where(sc == sr % CHUNK + tap + (GDN_HALO - CONV_K // 2), 1.0, 0.0).astype(BF16)

    lane = lax.broadcasted_iota(jnp.int32, (1, GDN_GW), 1)
    tok = lane % CHUNK
    row = lax.broadcasted_iota(jnp.int32, (CHUNK, 1), 0)
    causal = (tok >= row) if rev else (tok <= row)
    strict = (tok > row) if rev else (tok < row)
    eye = jnp.where(tok == row, 1.0, 0.0)
    r2 = lax.broadcasted_iota(jnp.int32, (GDN_GW, 1), 0)
    same_head = (r2 // CHUNK) == (lane // CHUNK)
    bd_ones = jnp.where(same_head, 1.0, 0.0).astype(BF16)
    tri_in = (r2 % CHUNK >= tok) if rev else (r2 % CHUNK <= tok)
    tri_bd = jnp.where(jnp.logical_and(same_head, tri_in), 1.0, 0.0).astype(BF16)
    tri_and_ones = jnp.concatenate([tri_bd, bd_ones], axis=1)
    row8 = lax.broadcasted_iota(jnp.int32, (8, 1), 0)

    def blockdiag(x):
        xb = x.astype(BF16)
        zero = jnp.zeros_like(xb)
        return jnp.concatenate([jnp.where(lane // CHUNK == h, xb, zero) for h in range(GDN_HPG)], axis=0)

    def wide_mm(x, y):
        return _dot(x.astype(BF16), blockdiag(y))

    def head_sums(x):
        return _dot(x.astype(BF16), bd_ones)

    def local_phase(chunks, res):
        units = [(ci, p) for ci in range(len(chunks)) for p in range(n_grp)]
        r0 = [c * CHUNK for c in chunks]

        gates, gc_rows, gl_rows = [], [], []
        for c in chunks:
            graw = abr_ref[c]
            xg = graw + dt_ref[...]
            softplus = jnp.maximum(xg, 0.0) + jnp.log(1.0 + jnp.exp(-jnp.abs(xg)))
            g = jnp.where(row8 < n_grp, jax.nn.sigmoid(graw), -jnp.exp(alog_ref[...]) * softplus)
            g3 = _split3(g)
            gates.append(g)
            sums = sum(_dot(t, tri_and_ones) for t in g3)
            gc_rows.append(sums[:, :GDN_GW])
            gl_rows.append(sums[:, GDN_GW:])
        yield

        shifted = [_dot(shift_sel, xpad_ref[r0[ci]:r0[ci] + win, :]) for ci in range(len(chunks))]
        yield

        def conv(ci, p, col0):
            cols = slice(col0 + p * GDN_GW, col0 + (p + 1) * GDN_GW)
            centre = xpad_ref[r0[ci] + GDN_HALO:r0[ci] + GDN_HALO + CHUNK, cols].astype(F32)
            acc = centre * cw_ref[CONV_K // 2:CONV_K // 2 + 1, cols]
            for i, j in enumerate(side_taps):
                acc = acc + shifted[ci][i * CHUNK:(i + 1) * CHUNK, cols] * cw_ref[j:j + 1, cols]
            return acc * jax.nn.sigmoid(acc)

        qkv = {u: [conv(*u, col0) for col0 in (0, GDN_WIDTH, 2 * GDN_WIDTH)] for u in units}
        yield
        ss = {u: head_sums(jnp.concatenate([qkv[u][0] * qkv[u][0], qkv[u][1] * qkv[u][1]], axis=0))
              for u in units}
        yield
        prep = {}
        for (ci, p) in units:
            beta_r = gates[ci][p:p + 1]
            g_r = gates[ci][n_grp + p:n_grp + p + 1]
            lhs = jnp.concatenate([jnp.where(causal, g_r, 0.0), eye * beta_r], axis=0)
            prep[(ci, p)] = sum(_dot(t, bd_ones) for t in _split2(lhs))
        yield

        kq, kn_bd, e_gc, k_upd, decay_b = {}, {}, {}, {}, {}
        for u in units:
            ci, p = u
            qn = qkv[u][0] * lax.rsqrt(ss[u][:CHUNK] + EPS) * (HEAD_DIM ** -0.5)
            kn = qkv[u][1] * lax.rsqrt(ss[u][CHUNK:] + EPS)
            kq[u] = jnp.concatenate([kn, qn], axis=0).astype(BF16)
            kn_bd[u] = blockdiag(kn)
            beta_r = gates[ci][p:p + 1]
            gc_r = gc_rows[ci][n_grp + p:n_grp + p + 1]
            gl_r = gl_rows[ci][n_grp + p:n_grp + p + 1]
            gc_b, beta_b = prep[u][:CHUNK], prep[u][CHUNK:]
            decay_b[u] = jnp.exp(jnp.where(causal, gc_b - gc_r, NEG_BIG)) * beta_r
            e_gc[u] = jnp.exp(gc_b)
            k_upd[u] = (kn * (jnp.exp(gl_r - gc_b) * beta_b)).astype(BF16)
        yield

        kk_qk = {u: _dot_nt(kq[u], kn_bd[u]) for u in units}
        yield
        lpow = {u: jnp.where(strict, kk_qk[u][:CHUNK] * decay_b[u], 0.0) for u in units}
        amat = {u: kk_qk[u][CHUNK:] * decay_b[u] for u in units}
        n_sq = 5
        lpow = {u: -lpow[u] for u in units}
        tinv = {u: eye + lpow[u] for u in units}
        lpow = {u: wide_mm(lpow[u], lpow[u]) for u in units}
        yield
        for k in range(n_sq):
            if k < n_sq - 1:
                prod = {u: _dot(lpow[u].astype(BF16),
                                jnp.concatenate([blockdiag(tinv[u]), blockdiag(lpow[u])], axis=1)) for u in units}
                lpow = {u: prod[u][:, GDN_GW:] for u in units}
            else:
                prod = {u: wide_mm(lpow[u], tinv[u]) for u in units}
            tinv = {u: tinv[u] + prod[u][:, :GDN_GW] for u in units}
            yield
        res.update(r0=r0, kq=kq, tinv=tinv, amat=amat, e_gc=e_gc, k_upd=k_upd, gl_rows=gl_rows,
                   v={u: qkv[u][2] for u in units})

    def scan_phase(res):
        for ci in range(len(res["r0"])):
            us = [(ci, p) for p in range(n_grp)]
            state = {u: state_ref[u[1]] for u in us}
            pq = {u: _dot(res["kq"][u], state[u].astype(BF16)) for u in us}
            yield
            vhat = {u: wide_mm(res["tinv"][u], res["v"][u] - res["e_gc"][u] * pq[u][:CHUNK]) for u in us}
            yield
            upd = {u: _dot_tn(res["k_upd"][u], vhat[u].astype(BF16)) for u in us}
            for u in us:
                gl_r = res["gl_rows"][ci][n_grp + u[1]:n_grp + u[1] + 1]
                state_ref[u[1]] = state[u] * jnp.exp(gl_r) + jnp.where(same_head, upd[u], 0.0)
            yield
            o = {u: res["e_gc"][u] * pq[u][CHUNK:] + wide_mm(res["amat"][u], vhat[u]) for u in us}
            for u in us:
                cs = slice(u[1] * GDN_GW, (u[1] + 1) * GDN_GW)
                rows = slice(res["r0"][ci], res["r0"][ci] + CHUNK)
                if final:
                    ob = o[u] + oprev_ref[rows, cs]
                    ms = head_sums(ob * ob) * (1.0 / HEAD_DIM)
                    zz = z_ref[rows, cs].astype(F32)
                    out = ob * lax.rsqrt(ms + EPS) * gn_ref[:, cs] * (zz * jax.nn.sigmoid(zz))
                    o_ref[rows, cs] = out.astype(o_ref.dtype)
                else:
                    o_ref[rows, cs] = o[u]
            yield

    def emit_interleaved(*gens):
        live = list(gens)
        while live:
            for g in list(live):
                if next(g, StopIteration) is StopIteration:
                    live.remove(g)

    order = list(range(n_chunks))[::-1] if rev else list(range(n_chunks))
    groups = [order[j:j + GDN_GROUP] for j in range(0, n_chunks, GDN_GROUP)]
    results = [dict() for _ in groups]
    emit_interleaved(local_phase(groups[0], results[0]))
    for gi in range(len(groups)):
        nxt = [local_phase(groups[gi + 1], results[gi + 1])] if gi + 1 < len(groups) else []
        emit_interleaved(scan_phase(results[gi]), *nxt)


def _gdn_pass(gd3, abr, conv_w, alog_row, dt_row, rev, final_args=None):
    bsz, seq, width3 = gd3.shape
    nblk = seq // GDN_TB
    hpb = GDN_TB // GDN_HALO
    blk = (lambda i: nblk - 1 - i) if rev else (lambda i: i)
    main = lambda w: pl.BlockSpec((None, GDN_TB, w), lambda b, i: (b, blk(i), 0))
    full = lambda shape: pl.BlockSpec(shape, lambda b, i: (0,) * len(shape))
    in_specs = [main(width3),
                pl.BlockSpec((None, GDN_HALO, width3), lambda b, i: (b, jnp.maximum(blk(i) * hpb - 1, 0), 0)),
                pl.BlockSpec((None, GDN_HALO, width3),
                             lambda b, i: (b, jnp.minimum((blk(i) + 1) * hpb, seq // GDN_HALO - 1), 0)),
                pl.BlockSpec((None, GDN_TB // CHUNK, 8, GDN_GW), lambda b, i: (b, blk(i), 0, 0)),
                full((CONV_K, width3)), full((8, GDN_GW)), full((8, GDN_GW))]
    args = [gd3, gd3, gd3, abr, conv_w, alog_row, dt_row]
    final = final_args is not None
    if final:
        z3, o_prev, gn = final_args
        in_specs += [main(GDN_WIDTH), main(GDN_WIDTH), full((1, GDN_WIDTH))]
        args += [z3, o_prev, gn]
    return pl.pallas_call(
        functools.partial(_gdn_body, rev=rev, final=final),
        grid=(bsz, nblk),
        in_specs=in_specs,
        out_specs=main(GDN_WIDTH),
        out_shape=jax.ShapeDtypeStruct((bsz, seq, GDN_WIDTH), BF16 if final else F32),
        scratch_shapes=[pltpu.VMEM((GDN_TB + 2 * GDN_HALO, width3), BF16),
                        pltpu.VMEM((GDN_HEADS // GDN_HPG, GDN_GW, GDN_GW), F32)],
        compiler_params=_params("parallel", "arbitrary"),
        name="gdn_bwd" if rev else "gdn_fwd",
    )(*args)


def _group_rows(x, bsz, seq):
    x = x.reshape(bsz, seq // CHUNK, CHUNK, GDN_HEADS // GDN_HPG, GDN_HPG)
    return x.transpose(0, 1, 3, 4, 2).reshape(bsz, seq // CHUNK, GDN_HEADS // GDN_HPG, GDN_GW)


def _gdn(gd, z, ab, conv_w, a_log, dt_bias, gdn_norm, bsz, seq):
    gd3 = gd.reshape(bsz, seq, 3 * GDN_WIDTH)
    z3 = z.reshape(bsz, seq, GDN_WIDTH)
    gn = jnp.tile(gdn_norm, GDN_HEADS).reshape(1, GDN_WIDTH)
    out = None
    for d in range(2):
        n_grp = GDN_HEADS // GDN_HPG
        beta = _group_rows(ab[:, d * GDN_HEADS:(d + 1) * GDN_HEADS], bsz, seq)
        araw = _group_rows(ab[:, (2 + d) * GDN_HEADS:(3 + d) * GDN_HEADS], bsz, seq)
        abr = jnp.pad(jnp.concatenate([beta, araw], axis=2), ((0, 0), (0, 0), (0, 8 - 2 * n_grp), (0, 0)))
        per_head = lambda v: jnp.pad(jnp.repeat(v[d], CHUNK).reshape(n_grp, GDN_GW), ((n_grp, 8 - 2 * n_grp), (0, 0)))
        final_args = None if d == 0 else (z3, out, gn)
        out = _gdn_pass(gd3, abr, conv_w, per_head(a_log), per_head(dt_bias), rev=(d == 1), final_args=final_args)
    return out.reshape(bsz * seq, GDN_WIDTH)


def _gdn_scan_jax(q, k, v, g, beta):
    B, T, H, dk = q.shape
    C = CHUNK
    N = T // C
    chunks = lambda t: t.reshape(B, N, C, H, -1).transpose(1, 0, 3, 2, 4)
    qc, kc, vc = chunks(q), chunks(k), chunks(v)
    gc = jnp.cumsum(g.reshape(B, N, C, H).transpose(1, 0, 3, 2), axis=-1)
    bc = beta.reshape(B, N, C, H).transpose(1, 0, 3, 2)[..., None]
    lower = jnp.tril(jnp.ones((C, C), bool))
    strict = jnp.tril(jnp.ones((C, C), bool), -1)
    decay = jnp.exp(jnp.where(lower, gc[..., :, None] - gc[..., None, :], -jnp.inf))
    kbeta = kc * bc
    lmat = jnp.where(strict, jnp.einsum('nbhid,nbhjd->nbhij', kbeta, kc) * decay, 0.0)
    solve = lambda rhs: lax.linalg.triangular_solve(lmat, rhs, left_side=True, lower=True, unit_diagonal=True)
    u = solve(vc * bc)
    w = solve(kbeta * jnp.exp(gc)[..., None])
    a_qk = jnp.einsum('nbhid,nbhjd->nbhij', qc, kc) * decay

    def step(state, inp):
        q_i, k_i, u_i, w_i, g_i, a_i = inp
        v_new = u_i - jnp.einsum('bhck,bhkv->bhcv', w_i, state)
        o_i = (jnp.einsum('bhck,bhkv->bhcv', q_i * jnp.exp(g_i)[..., None], state)
               + jnp.einsum('bhij,bhjv->bhiv', a_i, v_new))
        g_last = g_i[..., -1:]
        state = (state * jnp.exp(g_last)[..., None]
                 + jnp.einsum('bhck,bhcv->bhkv', k_i * jnp.exp(g_last - g_i)[..., None], v_new))
        return state, o_i

    s0 = jnp.zeros((B, H, dk, dk), F32)
    _, o = lax.scan(step, s0, (qc, kc, u, w, gc, a_qk))
    return o.transpose(1, 0, 3, 2, 4).reshape(B, T, H, dk)


def _gdn_jax(gd, z, ab, conv_w, a_log, dt_bias, gdn_norm, bsz, seq):
    c = lax.conv_general_dilated(
        gd.astype(F32).reshape(bsz, seq, 3 * GDN_WIDTH), conv_w[:, None, :], window_strides=(1,),
        padding=[(CONV_K // 2, CONV_K // 2)], dimension_numbers=('NWC', 'WIO', 'NWC'),
        feature_group_count=3 * GDN_WIDTH)
    c = jax.nn.silu(c)
    l2 = lambda t: t * lax.rsqrt(jnp.sum(t * t, axis=-1, keepdims=True) + EPS)
    qb, kb, vb = [t.reshape(bsz, seq, GDN_HEADS, HEAD_DIM) for t in jnp.split(c, 3, axis=-1)]
    qb = l2(qb) * (HEAD_DIM ** -0.5)
    kb = l2(kb)
    ab4 = ab[:, :4 * GDN_HEADS].reshape(bsz, seq, 4, GDN_HEADS)
    beta = jax.nn.sigmoid(ab4[:, :, 0:2])
    g = -jnp.exp(a_log) * jax.nn.softplus(ab4[:, :, 2:4] + dt_bias)
    fwd = _gdn_scan_jax(qb, kb, vb, g[:, :, 0], beta[:, :, 0])
    flip = lambda t: t[:, ::-1]
    bwd = flip(_gdn_scan_jax(flip(qb), flip(kb), flip(vb), flip(g[:, :, 1]), flip(beta[:, :, 1])))
    ob = fwd + bwd
    ob = (ob * lax.rsqrt(jnp.mean(ob * ob, axis=-1, keepdims=True) + EPS) * gdn_norm
          * jax.nn.silu(z.astype(F32).reshape(bsz, seq, GDN_HEADS, HEAD_DIM)))
    return ob.reshape(bsz * seq, GDN_WIDTH).astype(BF16)


def _encoder(x, mem, norm_mix, w_in, conv_w, a_log, dt_bias, gdn_norm, w_out, norm_mem_q, norm_mem_kv,
             w_mq, w_mkv, w_mo, norm_ffn, w_router_g, w_router_e, w_gate, w_up, w_down, norm_final):
    bsz, seq, _ = x.shape
    x2d = x.reshape(bsz * seq, D_MODEL)
    qkv1, qkv4, qkv16, gd, z, ab = _inproj(x2d, norm_mix, w_in, bsz, seq)
    slopes = jnp.exp2(-8.0 * jnp.arange(1, SWA_HEADS + 1, dtype=F32) / SWA_HEADS)
    branches = [_attn_branch(qkv, slopes, d) for qkv, d in zip((qkv1, qkv4, qkv16), DILATIONS)]
    ob = _gdn(gd, z, ab, conv_w, a_log, dt_bias, gdn_norm, bsz, seq)
    mem_k, mem_v = _memkv(mem.reshape(-1, D_MODEL), norm_mem_kv, w_mkv)
    mem_k = mem_k.reshape(bsz, -1, MEM_HEADS * MEM_HEAD_DIM)
    mem_v = mem_v.reshape(bsz, -1, MEM_HEADS * MEM_HEAD_DIM)
    x2, h2, route, counts = _trunk(x2d, branches, ob, mem_k, mem_v, w_out, norm_mem_q, w_mq, w_mo,
                                   norm_ffn, w_router_g, w_router_e, bsz, seq)
    y = _moe(x2, h2, route, counts, w_gate, w_up, w_down, norm_final)
    return y.reshape(bsz, seq, D_MODEL)


def kernel(x_prompt, x_sample, mem_prompt, mem_sample, norm_mix, w_in, conv_w, a_log, dt_bias, gdn_norm,
           w_out, norm_mem_q, norm_mem_kv, w_mq, w_mkv, w_mo, norm_ffn, w_router_g, w_router_e,
           w_gate, w_up, w_down, norm_final):
    p = dict(norm_mix=norm_mix[0], w_in=w_in[0], conv_w=conv_w[0], a_log=a_log[0], dt_bias=dt_bias[0],
             gdn_norm=gdn_norm[0], w_out=w_out[0], norm_mem_q=norm_mem_q[0], norm_mem_kv=norm_mem_kv[0],
             w_mq=w_mq[0], w_mkv=w_mkv[0], w_mo=w_mo[0], norm_ffn=norm_ffn[0], w_router_g=w_router_g[0],
             w_router_e=w_router_e[0], w_gate=w_gate[0].astype(BF16), w_up=w_up[0].astype(BF16),
             w_down=w_down[0].astype(BF16), norm_final=norm_final)
    return (_encoder(x_prompt, mem_prompt, **p), _encoder(x_sample, mem_sample, **p))
```

```python
import functools

import jax
import jax.numpy as jnp
from jax import lax
from jax.experimental import pallas as pl
from jax.experimental.pallas import tpu as pltpu

F32 = jnp.float32
BF16 = jnp.bfloat16

D_MODEL = 1024
HEAD_DIM = 64
SWA_HEADS = 8
GDN_HEADS = 8
SWA_WIDTH = SWA_HEADS * HEAD_DIM
GDN_WIDTH = GDN_HEADS * HEAD_DIM
DILATIONS = (1, 4, 16)
ATT_W = 64
CONV_K = 5
CHUNK = 64
MEM_HEADS = 4
MEM_HEAD_DIM = 256
N_GROUPS = 4
EXPERTS_PER_GROUP = 8
N_EXPERTS = 32
TOP_K = 2
D_EXPERT = 512
ROUTE_BLOCK = 512
EPS = 1e-6

LANES = 128
VMEM_LIMIT = 56 * 1024 * 1024
NEG_BIG = -1e30


def _dot(a, b):
    return jnp.dot(a, b, preferred_element_type=F32)


def _dot_nt(a, b):
    return lax.dot_general(a, b, (((1,), (1,)), ((), ())), preferred_element_type=F32)


def _dot_tn(a, b):
    return lax.dot_general(a, b, (((0,), (0,)), ((), ())), preferred_element_type=F32)


def _split2(x):
    hi = x.astype(BF16)
    lo = (x - hi.astype(F32)).astype(BF16)
    return hi, lo


def _split3(x):
    hi = x.astype(BF16)
    r = x - hi.astype(F32)
    mid = r.astype(BF16)
    lo = (r - mid.astype(F32)).astype(BF16)
    return hi, mid, lo


def _rms(x, g):
    return x * lax.rsqrt(jnp.mean(x * x, axis=-1, keepdims=True) + EPS) * g


def _params(*sem):
    return pltpu.CompilerParams(dimension_semantics=sem, vmem_limit_bytes=VMEM_LIMIT)


def _inproj_body(x_ref, g_ref, wa_ref, wgd_ref, wz_ref, wab_hi_ref, wab_lo_ref,
                 qkv1_ref, qkv4_ref, qkv16_ref, gd_ref, z_ref, ab_ref, stage_ref, *, tm):
    h = _rms(x_ref[...], g_ref[...])
    h_hi, h_lo = _split2(h)
    qkv = _dot(h_hi, wa_ref[...])
    qkv1_ref[...] = qkv.astype(BF16)
    for c in range(qkv.shape[1] // LANES):
        cols = slice(c * LANES, (c + 1) * LANES)
        stage_ref[c] = qkv[:, cols]
        for dil, out_ref in ((DILATIONS[1], qkv4_ref), (DILATIONS[2], qkv16_ref)):
            for r in range(dil):
                out_ref[r, :, cols] = stage_ref[c, pl.ds(r, tm // dil, stride=dil), :].astype(BF16)
    gd_ref[...] = _dot(h_hi, wgd_ref[...]).astype(BF16)
    z_ref[...] = _dot(h_hi, wz_ref[...]).astype(BF16)
    wab_hi = wab_hi_ref[...]
    ab = _dot(h_hi, wab_hi) + _dot(h_lo, wab_hi) + _dot(h_hi, wab_lo_ref[...])
    ab_ref[...] = ab


def _inproj(x2, norm_mix, w_in, bsz, seq, tm=512):
    n = x2.shape[0]
    nblk = seq // tm
    width = 3 * SWA_WIDTH
    wa = w_in[:, :3 * SWA_WIDTH].astype(BF16)
    wgd = w_in[:, 3 * SWA_WIDTH:3 * SWA_WIDTH + 3 * GDN_WIDTH].astype(BF16)
    wz = w_in[:, 3 * SWA_WIDTH + 3 * GDN_WIDTH:3 * SWA_WIDTH + 4 * GDN_WIDTH].astype(BF16)
    wab = jnp.pad(w_in[:, 3 * SWA_WIDTH + 4 * GDN_WIDTH:], ((0, 0), (0, LANES - 4 * GDN_HEADS)))
    wab_hi, wab_lo = _split2(wab)
    full = lambda shape: pl.BlockSpec(shape, lambda b, i: (0, 0))
    rows = lambda w: pl.BlockSpec((tm, w), lambda b, i: (b * nblk + i, 0))
    strided = lambda d: pl.BlockSpec((None, d, tm // d, width), lambda b, i: (b, 0, i, 0))
    d1, d4, d16 = DILATIONS
    return pl.pallas_call(
        functools.partial(_inproj_body, tm=tm),
        grid=(bsz, nblk),
        in_specs=[rows(D_MODEL), full((1, D_MODEL)), full(wa.shape), full(wgd.shape), full(wz.shape),
                  full(wab_hi.shape), full(wab_lo.shape)],
        out_specs=[pl.BlockSpec((None, None, tm, width), lambda b, i: (b, 0, i, 0)), strided(d4), strided(d16),
                   rows(3 * GDN_WIDTH), rows(GDN_WIDTH), rows(LANES)],
        out_shape=[jax.ShapeDtypeStruct((bsz, d, seq // d, width), BF16) for d in (d1, d4, d16)]
                  + [jax.ShapeDtypeStruct((n, 3 * GDN_WIDTH), BF16),
                     jax.ShapeDtypeStruct((n, GDN_WIDTH), BF16),
                     jax.ShapeDtypeStruct((n, LANES), F32)],
        scratch_shapes=[pltpu.VMEM((width // LANES, tm, LANES), F32)],
        compiler_params=_params("parallel", "parallel"),
        name="inproj",
    )(x2, norm_mix.reshape(1, D_MODEL), wa, wgd, wz, wab_hi, wab_lo)


ATT_QB = 128
ATT_KB = ATT_QB + 2 * ATT_W
ATT_GROUP = 8


def _attn_bias(slopes, dil):
    row = jnp.arange(ATT_QB)[:, None]
    col = jnp.arange(ATT_KB)[None, :]
    rel = jnp.stack([jnp.abs(col - row - var * ATT_W) for var in range(3)])
    dist = (rel * dil).astype(F32)
    bias = jnp.where(rel <= ATT_W, -slopes[:, None, None, None] * dist, NEG_BIG)
    bias = bias.reshape(SWA_HEADS // 2, 2, 3, ATT_QB, ATT_KB).transpose(0, 2, 1, 3, 4)
    return bias.reshape(SWA_HEADS // 2, 6, ATT_QB, ATT_KB)


def _attn_body(bias_ref, q_ref, k_ref, v_ref, o_ref, lse_ref, *, n_qb, seq_l, n_pw, n_res):
    lb = pl.program_id(3)
    lane = lax.broadcasted_iota(jnp.int32, (1, LANES), 1)
    left = lane < HEAD_DIM
    group = min(ATT_GROUP, n_qb)

    def qgroup(gi, carry):
        rows, var, kb, vb, q, pw_of, cols = [], [], [], [], [], [], []
        for j in range(group):
            qi = gi * group + j
            n0 = (lb * n_qb + qi) * ATT_QB
            kstart = pl.multiple_of(jnp.clip(n0 - ATT_W, 0, seq_l - ATT_KB), ATT_W)
            for rr in range(n_res):
                for pw in range(n_pw):
                    cs = slice(pw * LANES, (pw + 1) * LANES)
                    var.append((n0 - kstart) // ATT_W)
                    rows.append((rr, pl.ds(pl.multiple_of(qi * ATT_QB, ATT_QB), ATT_QB)))
                    q.append(q_ref[rows[-1] + (cs,)] * jnp.asarray(HEAD_DIM ** -0.5, BF16))
                    kb.append(k_ref[rr, pl.ds(kstart, ATT_KB), cs])
                    vb.append(v_ref[rr, pl.ds(kstart, ATT_KB), cs])
                    pw_of.append(pw)
                    cols.append(cs)
        blocks = range(group * n_res * n_pw)
        units = [(j, h) for j in blocks for h in range(2)]
        mine = [left, jnp.logical_not(left)]
        s = {(j, h): _dot_nt(jnp.where(mine[h], q[j], jnp.zeros_like(q[j])), kb[j])
                     + bias_ref[pw_of[j], var[j] * 2 + h] for (j, h) in units}
        m = {u: jnp.max(s[u], axis=-1, keepdims=True) for u in units}
        p = {u: jnp.exp(s[u] - m[u]).astype(BF16) for u in units}
        acc = {(j, h): _dot(p[(j, h)], jnp.where(mine[h], vb[j], jnp.ones_like(vb[j]))) for (j, h) in units}
        for j in blocks:
            num = jnp.where(left, acc[(j, 0)], acc[(j, 1)])
            den = pltpu.roll(jnp.where(left, acc[(j, 1)], acc[(j, 0)]), HEAD_DIM, 1)
            mx = jnp.where(left, m[(j, 0)], m[(j, 1)])
            o_ref[rows[j] + (cols[j],)] = (num / den).astype(BF16)
            lse_ref[rows[j] + (cols[j],)] = mx + jnp.log(den)
        return carry

    lax.fori_loop(0, n_qb // group, qgroup, 0)


def _attn_branch(qkv, slopes, dil):
    bsz, _, seq_l, _ = qkv.shape
    lblk = min(2048, seq_l)
    n_qb = lblk // ATT_QB
    n_pw = 1 if n_qb >= ATT_GROUP else 2
    n_res = min(dil, max(1, ATT_GROUP // (n_qb * n_pw)))
    n_steps = SWA_HEADS // 2 // n_pw
    width = n_pw * LANES
    qspec = pl.BlockSpec((None, n_res, lblk, width), lambda p, b, r, l: (b, r, l, p))
    kspec = pl.BlockSpec((None, n_res, seq_l, width), lambda p, b, r, l: (b, r, 0, n_steps + p))
    vspec = pl.BlockSpec((None, n_res, seq_l, width), lambda p, b, r, l: (b, r, 0, 2 * n_steps + p))
    ospec = pl.BlockSpec((None, n_res, lblk, width), lambda p, b, r, l: (b, r, l, p))
    bspec = pl.BlockSpec((n_pw, 6, ATT_QB, ATT_KB), lambda p, b, r, l: (p, 0, 0, 0))
    return pl.pallas_call(
        functools.partial(_attn_body, n_qb=n_qb, seq_l=seq_l, n_pw=n_pw, n_res=n_res),
        grid=(n_steps, bsz, dil // n_res, seq_l // lblk),
        in_specs=[bspec, qspec, kspec, vspec],
        out_specs=[ospec, ospec],
        out_shape=[jax.ShapeDtypeStruct((bsz, dil, seq_l, SWA_WIDTH), BF16),
                   jax.ShapeDtypeStruct((bsz, dil, seq_l, SWA_WIDTH), F32)],
        compiler_params=_params("parallel", "parallel", "parallel", "arbitrary"),
        name=f"dilated_attn_d{dil}",
    )(_attn_bias(slopes, dil), qkv, qkv, qkv)


def _memkv_body(m_ref, g_ref, w_ref, k_ref, v_ref):
    h = _rms(m_ref[...], g_ref[...]).astype(BF16)
    kv = _dot(h, w_ref[...])
    width = MEM_HEADS * MEM_HEAD_DIM
    k_ref[...] = (kv[:, :width] * (MEM_HEAD_DIM ** -0.5)).astype(BF16)
    v_ref[...] = kv[:, width:].astype(BF16)


def _memkv(mem2, norm_kv, w_mkv, tm=256):
    n = mem2.shape[0]
    width = MEM_HEADS * MEM_HEAD_DIM
    return pl.pallas_call(
        _memkv_body,
        grid=(n // tm,),
        in_specs=[pl.BlockSpec((tm, D_MODEL), lambda i: (i, 0)),
                  pl.BlockSpec((1, D_MODEL), lambda i: (0, 0)),
                  pl.BlockSpec((D_MODEL, 2 * width), lambda i: (0, 0))],
        out_specs=[pl.BlockSpec((tm, width), lambda i: (i, 0))] * 2,
        out_shape=[jax.ShapeDtypeStruct((n, width), BF16)] * 2,
        compiler_params=_params("parallel"),
        name="mem_kv",
    )(mem2, norm_kv.reshape(1, D_MODEL), w_mkv.astype(BF16))


ROUTE_COLS = N_GROUPS + N_EXPERTS


TRUNK_TM = 512
TRUNK_SUB = 256


def _trunk_body(x_ref, o1_ref, o2_ref, o3_ref, l1_ref, l2_ref, l3_ref, ob_ref,
                wo_a_ref, wo_b_ref, gq_ref, wq_ref, k_ref, v_ref, wmo_ref,
                gf_ref, wr_hi_ref, wr_lo_ref,
                x2_ref, h2_ref, route_ref, count_ref, run_ref, obuf_ref, lbuf_ref):
    first = jnp.logical_and(pl.program_id(0) == 0, pl.program_id(1) == 0)

    @pl.when(first)
    def _():
        run_ref[...] = jnp.zeros_like(run_ref)

    n_cb = SWA_WIDTH // LANES
    for bi, (o_ref, l_ref) in enumerate(((o2_ref, l2_ref), (o3_ref, l3_ref))):
        dil = DILATIONS[bi + 1]
        for r in range(dil):
            rows = pl.ds(r, TRUNK_TM // dil, stride=dil)
            o_r = o_ref[r].astype(F32)
            l_r = l_ref[r]
            for c in range(n_cb):
                obuf_ref[bi * n_cb + c, rows, :] = o_r[:, c * LANES:(c + 1) * LANES]
                lbuf_ref[bi * n_cb + c, rows, :] = l_r[:, c * LANES:(c + 1) * LANES]

    slabs = [slice(j * TRUNK_SUB, (j + 1) * TRUNK_SUB) for j in range(TRUNK_TM // TRUNK_SUB)]

    def token_major(buf_ref, bi, sl):
        return jnp.concatenate([buf_ref[bi * n_cb + c, sl, :] for c in range(n_cb)], axis=-1)

    def merge(sl):
        l1, l2, l3 = l1_ref[sl, :], token_major(lbuf_ref, 0, sl), token_major(lbuf_ref, 1, sl)
        mx = jnp.maximum(jnp.maximum(l1, l2), l3)
        e1, e2, e3 = jnp.exp(l1 - mx), jnp.exp(l2 - mx), jnp.exp(l3 - mx)
        ya = (e1 * o1_ref[sl, :].astype(F32) + e2 * token_major(obuf_ref, 0, sl)
              + e3 * token_major(obuf_ref, 1, sl))
        return (ya / (e1 + e2 + e3)).astype(BF16)

    cols = [slice(h * MEM_HEAD_DIM, (h + 1) * MEM_HEAD_DIM) for h in range(MEM_HEADS)]
    lane = lax.broadcasted_iota(jnp.int32, (TRUNK_SUB, LANES), 1)
    big = jnp.int32(LANES)
    is_g = lane < N_GROUPS
    r_i = lax.broadcasted_iota(jnp.int32, (TRUNK_SUB, TRUNK_SUB), 0)
    c_i = lax.broadcasted_iota(jnp.int32, (TRUNK_SUB, TRUNK_SUB), 1)
    tri = jnp.where(c_i < r_i, 1.0, 0.0).astype(BF16)

    def top2(logits):
        lg = jnp.where(is_g, logits, NEG_BIG)
        mg = jnp.max(lg, axis=-1, keepdims=True)
        g_idx = jnp.min(jnp.where(jnp.logical_and(is_g, lg == mg), lane, big), axis=-1, keepdims=True)
        g_w = 1.0 / jnp.sum(jnp.exp(lg - mg), axis=-1, keepdims=True)
        lo_lane = N_GROUPS + g_idx * EXPERTS_PER_GROUP
        in_grp = jnp.logical_and(lane >= lo_lane, lane < lo_lane + EXPERTS_PER_GROUP)
        le = jnp.where(in_grp, logits, NEG_BIG)
        m1 = jnp.max(le, axis=-1, keepdims=True)
        i1 = jnp.min(jnp.where(jnp.logical_and(in_grp, le == m1), lane, big), axis=-1, keepdims=True)
        le2 = jnp.where(lane == i1, NEG_BIG, le)
        m2 = jnp.max(le2, axis=-1, keepdims=True)
        i2 = jnp.min(jnp.where(jnp.logical_and(in_grp, le2 == m2), lane, big), axis=-1, keepdims=True)
        r2 = jnp.exp(m2 - m1)
        return i1, i2, g_w / (1.0 + r2), g_w * r2 / (1.0 + r2)

    col_sums = []

    def slab(sl):
        ya = merge(sl)
        yield
        x1 = x_ref[sl, :] + _dot(ya, wo_a_ref[...]) + _dot(ob_ref[sl, :], wo_b_ref[...])
        yield
        q = _dot(_rms(x1, gq_ref[...]).astype(BF16), wq_ref[...]).astype(BF16)
        yield
        s = [_dot_nt(q[:, cs], k_ref[:, cs]) for cs in cols]
        yield
        p = [jnp.exp(sh - jnp.max(sh, axis=-1, keepdims=True)) for sh in s]
        den = [jnp.sum(ph, axis=-1, keepdims=True) for ph in p]
        yield
        pv = [_dot(ph.astype(BF16), v_ref[:, cs]) for ph, cs in zip(p, cols)]
        yield
        heads = [(a / d).astype(BF16) for a, d in zip(pv, den)]
        x2 = x1 + _dot(jnp.concatenate(heads, axis=-1), wmo_ref[...])
        x2_ref[sl, :] = x2
        yield
        h2 = _rms(x2, gf_ref[...])
        h2_ref[sl, :] = _pack_rows(h2)
        h_hi, h_lo = _split2(h2)
        logits = _dot(h_hi, wr_hi_ref[...]) + _dot(h_lo, wr_hi_ref[...]) + _dot(h_hi, wr_lo_ref[...])
        yield
        i1, i2, gate1, gate2 = top2(logits)
        yield
        oh = jnp.where(jnp.logical_or(lane == i1, lane == i2), 1.0, 0.0)
        prefix = _dot(tri, oh.astype(BF16))
        earlier = list(col_sums)
        col_sums.append(jnp.sum(oh, axis=0, keepdims=True))
        yield
        before = prefix + sum(earlier, run_ref[...])
        rank1 = jnp.sum(jnp.where(lane == i1, before, 0.0), axis=-1, keepdims=True)
        rank2 = jnp.sum(jnp.where(lane == i2, before, 0.0), axis=-1, keepdims=True)
        route = jnp.where(lane == 0, (i1 - N_GROUPS).astype(F32), 0.0)
        for j, val in enumerate(((i2 - N_GROUPS).astype(F32), gate1, gate2, rank1, rank2), start=1):
            route = jnp.where(lane == j, val, route)
        route_ref[sl, :] = route

    live = [slab(sl) for sl in slabs]
    while live:
        live = [h for h in live if next(h, StopIteration) is not StopIteration]
    total = sum(col_sums, run_ref[...])
    run_ref[...] = total
    count_ref[...] = total


def _trunk(x2d, branches, ob, mem_k, mem_v, w_out, norm_mem_q, w_mq, w_mo, norm_ffn,
           w_router_g, w_router_e, bsz, seq):
    tm = TRUNK_TM
    n = bsz * seq
    nblk = seq // tm
    (o1, l1), (o2, l2), (o3, l3) = branches
    wo = w_out.astype(BF16)
    wr = jnp.concatenate([w_router_g, jnp.moveaxis(w_router_e, 0, 1).reshape(D_MODEL, N_EXPERTS)], axis=1)
    wr = jnp.pad(wr, ((0, 0), (0, LANES - ROUTE_COLS)))
    wr_hi, wr_lo = _split2(wr)
    rows = lambda w: pl.BlockSpec((tm, w), lambda b, i: (b * nblk + i, 0))
    full = lambda shape: pl.BlockSpec(shape, lambda b, i: (0, 0))
    memspec = pl.BlockSpec((None, mem_k.shape[1], mem_k.shape[2]), lambda b, i: (b, 0, 0))
    d1, d4, d16 = DILATIONS
    natural = pl.BlockSpec((None, None, tm, SWA_WIDTH), lambda b, i: (b, 0, i, 0))
    strided = lambda d: pl.BlockSpec((None, d, tm // d, SWA_WIDTH), lambda b, i: (b, 0, i, 0))
    branch_specs = [natural, strided(d4), strided(d16)]
    return pl.pallas_call(
        _trunk_body,
        grid=(bsz, nblk),
        in_specs=[rows(D_MODEL)] + branch_specs + branch_specs + [rows(GDN_WIDTH)]
                 + [full((SWA_WIDTH, D_MODEL)), full((GDN_WIDTH, D_MODEL)), full((1, D_MODEL)),
                    full((D_MODEL, D_MODEL)), memspec, memspec, full((D_MODEL, D_MODEL)),
                    full((1, D_MODEL)), full((D_MODEL, LANES)), full((D_MODEL, LANES))],
        out_specs=[rows(D_MODEL), rows(PACKED), rows(LANES), full((1, LANES))],
        out_shape=[jax.ShapeDtypeStruct((n, D_MODEL), F32), jax.ShapeDtypeStruct((n, PACKED), jnp.uint32),
                   jax.ShapeDtypeStruct((n, LANES), F32), jax.ShapeDtypeStruct((1, LANES), F32)],
        scratch_shapes=[pltpu.VMEM((1, LANES), F32), pltpu.VMEM((2 * SWA_WIDTH // LANES, tm, LANES), F32),
                        pltpu.VMEM((2 * SWA_WIDTH // LANES, tm, LANES), F32)],
        compiler_params=_params("arbitrary", "arbitrary"),
        name="trunk",
    )(x2d, o1, o2, o3, l1, l2, l3, ob, wo[:SWA_WIDTH], wo[SWA_WIDTH:], norm_mem_q.reshape(1, D_MODEL),
      w_mq.astype(BF16), mem_k, mem_v, w_mo.astype(BF16), norm_ffn.reshape(1, D_MODEL), wr_hi, wr_lo)


DMA_UNROLL = 8
PACKED = D_MODEL // 2
HI_MASK = 0xFFFF0000


def _pack_rows(x):
    bits = lambda v: lax.bitcast_convert_type(v.astype(BF16).astype(F32), jnp.uint32)
    return (bits(x[:, :PACKED]) & jnp.uint32(HI_MASK)) | (bits(x[:, PACKED:]) >> 16)


def _unpack_rows(u):
    hi = lax.bitcast_convert_type(u & jnp.uint32(HI_MASK), F32)
    lo = lax.bitcast_convert_type(u << 16, F32)
    return hi, lo


def _dispatch_body(dest_ref, h_ref, xs_in_ref, xs_ref, sem, *, tm):
    del xs_in_ref

    def issue(t, c):
        for k in range(TOP_K):
            dst = xs_ref.at[pl.ds(dest_ref[0, 0, t * TOP_K + k], 1)]
            pltpu.make_async_copy(h_ref.at[pl.ds(t, 1)], dst, sem).start(priority=k % 2)
        return c

    lax.fori_loop(0, tm, issue, 0, unroll=DMA_UNROLL)
    for _ in range(TOP_K):
        pltpu.make_async_copy(h_ref, xs_ref.at[pl.ds(0, tm)], sem).wait()


def _dispatch(h2, dest, n_slots, tm=4096):
    n = h2.shape[0]
    dest3 = dest.reshape(n // tm, 1, tm * TOP_K)
    xs0 = jnp.zeros((n_slots, PACKED), jnp.uint32)
    return pl.pallas_call(
        functools.partial(_dispatch_body, tm=tm),
        grid=(n // tm,),
        in_specs=[pl.BlockSpec((1, 1, tm * TOP_K), lambda i: (i, 0, 0), memory_space=pltpu.SMEM),
                  pl.BlockSpec((tm, PACKED), lambda i: (i, 0)),
                  pl.BlockSpec(memory_space=pl.ANY)],
        out_specs=pl.BlockSpec(memory_space=pl.ANY),
        out_shape=jax.ShapeDtypeStruct((n_slots, PACKED), jnp.uint32),
        scratch_shapes=[pltpu.SemaphoreType.DMA(())],
        input_output_aliases={2: 0},
        compiler_params=_params("arbitrary"),
        name="moe_dispatch",
    )(dest3, h2, xs0)


def _expert_body(be_ref, nused_ref, x_ref, wg_ref, wu_ref, wd_ref, y_ref):
    i = pl.program_id(0)

    @pl.when(i < nused_ref[0])
    def _():
        x = jnp.concatenate([half.astype(BF16) for half in _unpack_rows(x_ref[...])], axis=-1)
        a = _dot(x, wg_ref[...])
        b = _dot(x, wu_ref[...])
        hid = (a * jax.nn.sigmoid(a) * b).astype(BF16)
        y_ref[...] = _pack_rows(_dot(hid, wd_ref[...]))

    @pl.when(i >= nused_ref[0])
    def _():
        y_ref[...] = jnp.zeros_like(y_ref)


def _experts(xs, block_expert, n_used, w_gate, w_up, w_down):
    n_slots = xs.shape[0]
    n_blocks = n_slots // ROUTE_BLOCK
    grid_spec = pltpu.PrefetchScalarGridSpec(
        num_scalar_prefetch=2,
        grid=(n_blocks,),
        in_specs=[pl.BlockSpec((ROUTE_BLOCK, PACKED), lambda i, be, nu: (i, 0)),
                  pl.BlockSpec((None, D_MODEL, D_EXPERT), lambda i, be, nu: (be[i], 0, 0)),
                  pl.BlockSpec((None, D_MODEL, D_EXPERT), lambda i, be, nu: (be[i], 0, 0)),
                  pl.BlockSpec((None, D_EXPERT, D_MODEL), lambda i, be, nu: (be[i], 0, 0))],
        out_specs=pl.BlockSpec((ROUTE_BLOCK, PACKED), lambda i, be, nu: (i, 0)),
    )
    return pl.pallas_call(
        _expert_body,
        grid_spec=grid_spec,
        out_shape=jax.ShapeDtypeStruct((n_slots, PACKED), jnp.uint32),
        compiler_params=_params("arbitrary"),
        name="moe_experts",
    )(block_expert, n_used, xs, w_gate, w_up, w_down)


def _combine_body(dest_ref, x_ref, route_ref, g_ref, yb_ref, o_ref, buf_ref, sem, *, tm):
    def issue(t, c):
        for k in range(TOP_K):
            src = yb_ref.at[pl.ds(dest_ref[0, 0, t * TOP_K + k], 1)]
            pltpu.make_async_copy(src, buf_ref.at[k, pl.ds(t, 1)], sem).start(priority=k % 2)
        return c

    lax.fori_loop(0, tm, issue, 0, unroll=DMA_UNROLL)
    for k in range(TOP_K):
        pltpu.make_async_copy(yb_ref.at[pl.ds(0, tm)], buf_ref.at[k], sem).wait()
    route = route_ref[...]
    hi0, lo0 = _unpack_rows(buf_ref[0])
    hi1, lo1 = _unpack_rows(buf_ref[1])
    g0, g1 = route[:, 2:3], route[:, 3:4]
    y = jnp.concatenate([g0 * hi0 + g1 * hi1, g0 * lo0 + g1 * lo1], axis=-1)
    o_ref[...] = _rms(x_ref[...] + y, g_ref[...])


def _combine(x2, route, dest, yb, norm_final, tm=2048):
    n = x2.shape[0]
    dest3 = dest.reshape(n // tm, 1, tm * TOP_K)
    return pl.pallas_call(
        functools.partial(_combine_body, tm=tm),
        grid=(n // tm,),
        in_specs=[pl.BlockSpec((1, 1, tm * TOP_K), lambda i: (i, 0, 0), memory_space=pltpu.SMEM),
                  pl.BlockSpec((tm, D_MODEL), lambda i: (i, 0)),
                  pl.BlockSpec((tm, LANES), lambda i: (i, 0)),
                  pl.BlockSpec((1, D_MODEL), lambda i: (0, 0)),
                  pl.BlockSpec(memory_space=pl.ANY)],
        out_specs=pl.BlockSpec((tm, D_MODEL), lambda i: (i, 0)),
        out_shape=jax.ShapeDtypeStruct((n, D_MODEL), F32),
        scratch_shapes=[pltpu.VMEM((TOP_K, tm, PACKED), jnp.uint32), pltpu.SemaphoreType.DMA(())],
        compiler_params=_params("arbitrary"),
        name="moe_combine",
    )(dest3, x2, route, norm_final.reshape(1, D_MODEL), yb)


def _moe(x2, h2, route, counts, w_gate, w_up, w_down, norm_final):
    n = x2.shape[0]
    m_slots = n * TOP_K
    n_blocks = -(-(m_slots + N_EXPERTS * (ROUTE_BLOCK - 1)) // ROUTE_BLOCK)
    cnt = counts[0, N_GROUPS:N_GROUPS + N_EXPERTS].astype(jnp.int32)
    padded = (cnt + ROUTE_BLOCK - 1) // ROUTE_BLOCK * ROUTE_BLOCK
    pad_end = jnp.cumsum(padded)
    seg_start = pad_end - padded
    eid = route[:, 0:TOP_K].astype(jnp.int32)
    rank = route[:, 4:4 + TOP_K].astype(jnp.int32)
    dest = (seg_start[eid] + rank).reshape(-1)
    block_start = jnp.arange(n_blocks, dtype=jnp.int32) * ROUTE_BLOCK
    block_expert = jnp.minimum(jnp.sum(pad_end[None, :] <= block_start[:, None], axis=1), N_EXPERTS - 1)
    block_expert = block_expert.astype(jnp.int32)
    n_used = (pad_end[-1:] // ROUTE_BLOCK).astype(jnp.int32)
    xs = _dispatch(h2, dest, n_blocks * ROUTE_BLOCK)
    yb = _experts(xs, block_expert, n_used, w_gate, w_up, w_down)
    return _combine(x2, route, dest, yb, norm_final)


GDN_TB = 1024
GDN_HALO = 16
GDN_GROUP = 4
GDN_HPG = 2
GDN_GW = GDN_HPG * HEAD_DIM


def _gdn_body(gd_ref, prev_ref, next_ref, abr_ref, cw_ref, alog_ref, dt_ref, *rest, rev, final):
    if final:
        z_ref, oprev_ref, gn_ref, o_ref, xpad_ref, state_ref = rest
    else:
        o_ref, xpad_ref, state_ref = rest
    i = pl.program_id(1)
    nblk = pl.num_programs(1)
    n_chunks = GDN_TB // CHUNK
    n_grp = GDN_HEADS // GDN_HPG
    width3 = 3 * GDN_WIDTH

    @pl.when(i == 0)
    def _():
        state_ref[...] = jnp.zeros_like(state_ref)

    blk = (nblk - 1 - i) if rev else i
    zero_halo = jnp.zeros((GDN_HALO, width3), BF16)
    xpad_ref[:GDN_HALO, :] = jnp.where(blk == 0, zero_halo, prev_ref[...])
    xpad_ref[GDN_HALO:GDN_HALO + GDN_TB, :] = gd_ref[...]
    xpad_ref[GDN_HALO + GDN_TB:, :] = jnp.where(blk == nblk - 1, zero_halo, next_ref[...])
    win = CHUNK + 2 * GDN_HALO
    side_taps = [j for j in range(CONV_K) if j != CONV_K // 2]
    sr = lax.broadcasted_iota(jnp.int32, (len(side_taps) * CHUNK, 1), 0)
    sc = lax.broadcasted_iota(jnp.int32, (1, win), 1)
    tap = sr // CHUNK
    tap = tap + jnp.where(tap >= CONV_K // 2, 1, 0)
    shift_sel = jnp.where(sc == sr % CHUNK + tap + (GDN_HALO - CONV_K // 2), 1.0, 0.0).astype(BF16)

    lane = lax.broadcasted_iota(jnp.int32, (1, GDN_GW), 1)
    tok = lane % CHUNK
    row = lax.broadcasted_iota(jnp.int32, (CHUNK, 1), 0)
    causal = (tok >= row) if rev else (tok <= row)
    strict = (tok > row) if rev else (tok < row)
    eye = jnp.where(tok == row, 1.0, 0.0)
    r2 = lax.broadcasted_iota(jnp.int32, (GDN_GW, 1), 0)
    same_head = (r2 // CHUNK) == (lane // CHUNK)
    bd_ones = jnp.where(same_head, 1.0, 0.0).astype(BF16)
    tri_in = (r2 % CHUNK >= tok) if rev else (r2 % CHUNK <= tok)
    tri_bd = jnp.where(jnp.logical_and(same_head, tri_in), 1.0, 0.0).astype(BF16)
    tri_and_ones = jnp.concatenate([tri_bd, bd_ones], axis=1)
    row8 = lax.broadcasted_iota(jnp.int32, (8, 1), 0)

    def blockdiag(x):
        xb = x.astype(BF16)
        zero = jnp.zeros_like(xb)
        return jnp.concatenate([jnp.where(lane // CHUNK == h, xb, zero) for h in range(GDN_HPG)], axis=0)

    def wide_mm(x, y):
        return _dot(x.astype(BF16), blockdiag(y))

    def head_sums(x):
        return _dot(x.astype(BF16), bd_ones)

    def local_phase(chunks, res):
        units = [(ci, p) for ci in range(len(chunks)) for p in range(n_grp)]
        r0 = [c * CHUNK for c in chunks]

        gates, gc_rows, gl_rows = [], [], []
        for c in chunks:
            graw = abr_ref[c]
            xg = graw + dt_ref[...]
            softplus = jnp.maximum(xg, 0.0) + jnp.log(1.0 + jnp.exp(-jnp.abs(xg)))
            g = jnp.where(row8 < n_grp, jax.nn.sigmoid(graw), -jnp.exp(alog_ref[...]) * softplus)
            g3 = _split3(g)
            gates.append(g)
            sums = sum(_dot(t, tri_and_ones) for t in g3)
            gc_rows.append(sums[:, :GDN_GW])
            gl_rows.append(sums[:, GDN_GW:])
        yield

        shifted = [_dot(shift_sel, xpad_ref[r0[ci]:r0[ci] + win, :]) for ci in range(len(chunks))]
        yield

        def conv(ci, p, col0):
            cols = slice(col0 + p * GDN_GW, col0 + (p + 1) * GDN_GW)
            centre = xpad_ref[r0[ci] + GDN_HALO:r0[ci] + GDN_HALO + CHUNK, cols].astype(F32)
            acc = centre * cw_ref[CONV_K // 2:CONV_K // 2 + 1, cols]
            for i, j in enumerate(side_taps):
                acc = acc + shifted[ci][i * CHUNK:(i + 1) * CHUNK, cols] * cw_ref[j:j + 1, cols]
            return acc * jax.nn.sigmoid(acc)

        qkv = {u: [conv(*u, col0) for col0 in (0, GDN_WIDTH, 2 * GDN_WIDTH)] for u in units}
        yield
        ss = {u: head_sums(jnp.concatenate([qkv[u][0] * qkv[u][0], qkv[u][1] * qkv[u][1]], axis=0))
              for u in units}
        yield
        prep = {}
        for (ci, p) in units:
            beta_r = gates[ci][p:p + 1]
            g_r = gates[ci][n_grp + p:n_grp + p + 1]
            lhs = jnp.concatenate([jnp.where(causal, g_r, 0.0), eye * beta_r], axis=0)
            prep[(ci, p)] = sum(_dot(t, bd_ones) for t in _split2(lhs))
        yield

        kq, kn_bd, e_gc, k_upd, decay_b = {}, {}, {}, {}, {}
        for u in units:
            ci, p = u
            qn = qkv[u][0] * lax.rsqrt(ss[u][:CHUNK] + EPS) * (HEAD_DIM ** -0.5)
            kn = qkv[u][1] * lax.rsqrt(ss[u][CHUNK:] + EPS)
            kq[u] = jnp.concatenate([kn, qn], axis=0).astype(BF16)
            kn_bd[u] = blockdiag(kn)
            beta_r = gates[ci][p:p + 1]
            gc_r = gc_rows[ci][n_grp + p:n_grp + p + 1]
            gl_r = gl_rows[ci][n_grp + p:n_grp + p + 1]
            gc_b, beta_b = prep[u][:CHUNK], prep[u][CHUNK:]
            decay_b[u] = jnp.exp(jnp.where(causal, gc_b - gc_r, NEG_BIG)) * beta_r
            e_gc[u] = jnp.exp(gc_b)
            k_upd[u] = (kn * (jnp.exp(gl_r - gc_b) * beta_b)).astype(BF16)
        yield

        kk_qk = {u: _dot_nt(kq[u], kn_bd[u]) for u in units}
        yield
        lpow = {u: jnp.where(strict, kk_qk[u][:CHUNK] * decay_b[u], 0.0) for u in units}
        amat = {u: kk_qk[u][CHUNK:] * decay_b[u] for u in units}
        n_sq = 5
        lpow = {u: -lpow[u] for u in units}
        tinv = {u: eye + lpow[u] for u in units}
        lpow = {u: wide_mm(lpow[u], lpow[u]) for u in units}
        yield
        for k in range(n_sq):
            if k < n_sq - 1:
                prod = {u: _dot(lpow[u].astype(BF16),
                                jnp.concatenate([blockdiag(tinv[u]), blockdiag(lpow[u])], axis=1)) for u in units}
                lpow = {u: prod[u][:, GDN_GW:] for u in units}
            else:
                prod = {u: wide_mm(lpow[u], tinv[u]) for u in units}
            tinv = {u: tinv[u] + prod[u][:, :GDN_GW] for u in units}
            yield
        res.update(r0=r0, kq=kq, tinv=tinv, amat=amat, e_gc=e_gc, k_upd=k_upd, gl_rows=gl_rows,
                   v={u: qkv[u][2] for u in units})

    def scan_phase(res):
        for ci in range(len(res["r0"])):
            us = [(ci, p) for p in range(n_grp)]
            state = {u: state_ref[u[1]] for u in us}
            pq = {u: _dot(res["kq"][u], state[u].astype(BF16)) for u in us}
            yield
            vhat = {u: wide_mm(res["tinv"][u], res["v"][u] - res["e_gc"][u] * pq[u][:CHUNK]) for u in us}
            yield
            upd = {u: _dot_tn(res["k_upd"][u], vhat[u].astype(BF16)) for u in us}
            for u in us:
                gl_r = res["gl_rows"][ci][n_grp + u[1]:n_grp + u[1] + 1]
                state_ref[u[1]] = state[u] * jnp.exp(gl_r) + jnp.where(same_head, upd[u], 0.0)
            yield
            o = {u: res["e_gc"][u] * pq[u][CHUNK:] + wide_mm(res["amat"][u], vhat[u]) for u in us}
            for u in us:
                cs = slice(u[1] * GDN_GW, (u[1] + 1) * GDN_GW)
                rows = slice(res["r0"][ci], res["r0"][ci] + CHUNK)
                if final:
                    ob = o[u] + oprev_ref[rows, cs]
                    ms = head_sums(ob * ob) * (1.0 / HEAD_DIM)
                    zz = z_ref[rows, cs].astype(F32)
                    out = ob * lax.rsqrt(ms + EPS) * gn_ref[:, cs] * (zz * jax.nn.sigmoid(zz))
                    o_ref[rows, cs] = out.astype(o_ref.dtype)
                else:
                    o_ref[rows, cs] = o[u]
            yield

    def emit_interleaved(*gens):
        live = list(gens)
        while live:
            for g in list(live):
                if next(g, StopIteration) is StopIteration:
                    live.remove(g)

    order = list(range(n_chunks))[::-1] if rev else list(range(n_chunks))
    groups = [order[j:j + GDN_GROUP] for j in range(0, n_chunks, GDN_GROUP)]
    results = [dict() for _ in groups]
    emit_interleaved(local_phase(groups[0], results[0]))
    for gi in range(len(groups)):
        nxt = [local_phase(groups[gi + 1], results[gi + 1])] if gi + 1 < len(groups) else []
        emit_interleaved(scan_phase(results[gi]), *nxt)


def _gdn_pass(gd3, abr, conv_w, alog_row, dt_row, rev, final_args=None):
    bsz, seq, width3 = gd3.shape
    nblk = seq // GDN_TB
    hpb = GDN_TB // GDN_HALO
    blk = (lambda i: nblk - 1 - i) if rev else (lambda i: i)
    main = lambda w: pl.BlockSpec((None, GDN_TB, w), lambda b, i: (b, blk(i), 0))
    full = lambda shape: pl.BlockSpec(shape, lambda b, i: (0,) * len(shape))
    in_specs = [main(width3),
                pl.BlockSpec((None, GDN_HALO, width3), lambda b, i: (b, jnp.maximum(blk(i) * hpb - 1, 0), 0)),
                pl.BlockSpec((None, GDN_HALO, width3),
                             lambda b, i: (b, jnp.minimum((blk(i) + 1) * hpb, seq // GDN_HALO - 1), 0)),
                pl.BlockSpec((None, GDN_TB // CHUNK, 8, GDN_GW), lambda b, i: (b, blk(i), 0, 0)),
                full((CONV_K, width3)), full((8, GDN_GW)), full((8, GDN_GW))]
    args = [gd3, gd3, gd3, abr, conv_w, alog_row, dt_row]
    final = final_args is not None
    if final:
        z3, o_prev, gn = final_args
        in_specs += [main(GDN_WIDTH), main(GDN_WIDTH), full((1, GDN_WIDTH))]
        args += [z3, o_prev, gn]
    return pl.pallas_call(
        functools.partial(_gdn_body, rev=rev, final=final),
        grid=(bsz, nblk),
        in_specs=in_specs,
        out_specs=main(GDN_WIDTH),
        out_shape=jax.ShapeDtypeStruct((bsz, seq, GDN_WIDTH), BF16 if final else F32),
        scratch_shapes=[pltpu.VMEM((GDN_TB + 2 * GDN_HALO, width3), BF16),
                        pltpu.VMEM((GDN_HEADS // GDN_HPG, GDN_GW, GDN_GW), F32)],
        compiler_params=_params("parallel", "arbitrary"),
        name="gdn_bwd" if rev else "gdn_fwd",
    )(*args)


def _group_rows(x, bsz, seq):
    x = x.reshape(bsz, seq // CHUNK, CHUNK, GDN_HEADS // GDN_HPG, GDN_HPG)
    return x.transpose(0, 1, 3, 4, 2).reshape(bsz, seq // CHUNK, GDN_HEADS // GDN_HPG, GDN_GW)


def _gdn(gd, z, ab, conv_w, a_log, dt_bias, gdn_norm, bsz, seq):
    gd3 = gd.reshape(bsz, seq, 3 * GDN_WIDTH)
    z3 = z.reshape(bsz, seq, GDN_WIDTH)
    gn = jnp.tile(gdn_norm, GDN_HEADS).reshape(1, GDN_WIDTH)
    out = None
    for d in range(2):
        n_grp = GDN_HEADS // GDN_HPG
        beta = _group_rows(ab[:, d * GDN_HEADS:(d + 1) * GDN_HEADS], bsz, seq)
        araw = _group_rows(ab[:, (2 + d) * GDN_HEADS:(3 + d) * GDN_HEADS], bsz, seq)
        abr = jnp.pad(jnp.concatenate([beta, araw], axis=2), ((0, 0), (0, 0), (0, 8 - 2 * n_grp), (0, 0)))
        per_head = lambda v: jnp.pad(jnp.repeat(v[d], CHUNK).reshape(n_grp, GDN_GW), ((n_grp, 8 - 2 * n_grp), (0, 0)))
        final_args = None if d == 0 else (z3, out, gn)
        out = _gdn_pass(gd3, abr, conv_w, per_head(a_log), per_head(dt_bias), rev=(d == 1), final_args=final_args)
    return out.reshape(bsz * seq, GDN_WIDTH)


def _encoder(x, mem, norm_mix, w_in, conv_w, a_log, dt_bias, gdn_norm, w_out, norm_mem_q, norm_mem_kv,
             w_mq, w_mkv, w_mo, norm_ffn, w_router_g, w_router_e, w_gate, w_up, w_down, norm_final):
    bsz, seq, _ = x.shape
    x2d = x.reshape(bsz * seq, D_MODEL)
    qkv1, qkv4, qkv16, gd, z, ab = _inproj(x2d, norm_mix, w_in, bsz, seq)
    slopes = jnp.exp2(-8.0 * jnp.arange(1, SWA_HEADS + 1, dtype=F32) / SWA_HEADS)
    branches = [_attn_branch(qkv, slopes, d) for qkv, d in zip((qkv1, qkv4, qkv16), DILATIONS)]
    ob = _gdn(gd, z, ab, conv_w, a_log, dt_bias, gdn_norm, bsz, seq)
    mem_k, mem_v = _memkv(mem.reshape(-1, D_MODEL), norm_mem_kv, w_mkv)
    mem_k = mem_k.reshape(bsz, -1, MEM_HEADS * MEM_HEAD_DIM)
    mem_v = mem_v.reshape(bsz, -1, MEM_HEADS * MEM_HEAD_DIM)
    x2, h2, route, counts = _trunk(x2d, branches, ob, mem_k, mem_v, w_out, norm_mem_q, w_mq, w_mo,
                                   norm_ffn, w_router_g, w_router_e, bsz, seq)
    y = _moe(x2, h2, route, counts, w_gate, w_up, w_down, norm_final)
    return y.reshape(bsz, seq, D_MODEL)


def kernel(x_prompt, x_sample, mem_prompt, mem_sample, norm_mix, w_in, conv_w, a_log, dt_bias, gdn_norm,
           w_out, norm_mem_q, norm_mem_kv, w_mq, w_mkv, w_mo, norm_ffn, w_router_g, w_router_e,
           w_gate, w_up, w_down, norm_final):
    p = dict(norm_mix=norm_mix[0], w_in=w_in[0], conv_w=conv_w[0], a_log=a_log[0], dt_bias=dt_bias[0],
             gdn_norm=gdn_norm[0], w_out=w_out[0], norm_mem_q=norm_mem_q[0], norm_mem_kv=norm_mem_kv[0],
             w_mq=w_mq[0], w_mkv=w_mkv[0], w_mo=w_mo[0], norm_ffn=norm_ffn[0], w_router_g=w_router_g[0],
             w_router_e=w_router_e[0], w_gate=w_gate[0].astype(BF16), w_up=w_up[0].astype(BF16),
             w_down=w_down[0].astype(BF16), norm_final=norm_final)
    return (_encoder(x_prompt, mem_prompt, **p), _encoder(x_sample, mem_sample, **p))
```

```python
import functools

import jax
import jax.numpy as jnp
from jax import lax
from jax.experimental import pallas as pl
from jax.experimental.pallas import tpu as pltpu

F32 = jnp.float32
BF16 = jnp.bfloat16

D_MODEL = 1024
HEAD_DIM = 64
SWA_HEADS = 8
GDN_HEADS = 8
SWA_WIDTH = SWA_HEADS * HEAD_DIM
GDN_WIDTH = GDN_HEADS * HEAD_DIM
DILATIONS = (1, 4, 16)
ATT_W = 64
CONV_K = 5
CHUNK = 64
MEM_HEADS = 4
MEM_HEAD_DIM = 256
N_GROUPS = 4
EXPERTS_PER_GROUP = 8
N_EXPERTS = 32
TOP_K = 2
D_EXPERT = 512
ROUTE_BLOCK = 512
EPS = 1e-6

LANES = 128
VMEM_LIMIT = 56 * 1024 * 1024
NEG_BIG = -1e30


def _dot(a, b):
    return jnp.dot(a, b, preferred_element_type=F32)


def _dot_nt(a, b):
    return lax.dot_general(a, b, (((1,), (1,)), ((), ())), preferred_element_type=F32)


def _dot_tn(a, b):
    return lax.dot_general(a, b, (((0,), (0,)), ((), ())), preferred_element_type=F32)


def _split2(x):
    hi = x.astype(BF16)
    lo = (x - hi.astype(F32)).astype(BF16)
    return hi, lo


def _split3(x):
    hi = x.astype(BF16)
    r = x - hi.astype(F32)
    mid = r.astype(BF16)
    lo = (r - mid.astype(F32)).astype(BF16)
    return hi, mid, lo


def _rms(x, g):
    return x * lax.rsqrt(jnp.mean(x * x, axis=-1, keepdims=True) + EPS) * g


def _params(*sem):
    return pltpu.CompilerParams(dimension_semantics=sem, vmem_limit_bytes=VMEM_LIMIT)


def _inproj_body(x_ref, g_ref, wa_ref, wgd_ref, wz_ref, wab_hi_ref, wab_lo_ref,
                 qkv1_ref, qkv4_ref, qkv16_ref, gd_ref, z_ref, ab_ref, stage_ref, *, tm):
    h = _rms(x_ref[...], g_ref[...])
    h_hi, h_lo = _split2(h)
    qkv = _dot(h_hi, wa_ref[...])
    qkv1_ref[...] = qkv.astype(BF16)
    for c in range(qkv.shape[1] // LANES):
        cols = slice(c * LANES, (c + 1) * LANES)
        stage_ref[c] = qkv[:, cols]
        for dil, out_ref in ((DILATIONS[1], qkv4_ref), (DILATIONS[2], qkv16_ref)):
            for r in range(dil):
                out_ref[r, :, cols] = stage_ref[c, pl.ds(r, tm // dil, stride=dil), :].astype(BF16)
    gd_ref[...] = _dot(h_hi, wgd_ref[...]).astype(BF16)
    z_ref[...] = _dot(h_hi, wz_ref[...]).astype(BF16)
    wab_hi = wab_hi_ref[...]
    ab = _dot(h_hi, wab_hi) + _dot(h_lo, wab_hi) + _dot(h_hi, wab_lo_ref[...])
    ab_ref[...] = ab


def _inproj(x2, norm_mix, w_in, bsz, seq, tm=512):
    n = x2.shape[0]
    nblk = seq // tm
    width = 3 * SWA_WIDTH
    wa = w_in[:, :3 * SWA_WIDTH].astype(BF16)
    wgd = w_in[:, 3 * SWA_WIDTH:3 * SWA_WIDTH + 3 * GDN_WIDTH].astype(BF16)
    wz = w_in[:, 3 * SWA_WIDTH + 3 * GDN_WIDTH:3 * SWA_WIDTH + 4 * GDN_WIDTH].astype(BF16)
    wab = jnp.pad(w_in[:, 3 * SWA_WIDTH + 4 * GDN_WIDTH:], ((0, 0), (0, LANES - 4 * GDN_HEADS)))
    wab_hi, wab_lo = _split2(wab)
    full = lambda shape: pl.BlockSpec(shape, lambda b, i: (0, 0))
    rows = lambda w: pl.BlockSpec((tm, w), lambda b, i: (b * nblk + i, 0))
    strided = lambda d: pl.BlockSpec((None, d, tm // d, width), lambda b, i: (b, 0, i, 0))
    d1, d4, d16 = DILATIONS
    return pl.pallas_call(
        functools.partial(_inproj_body, tm=tm),
        grid=(bsz, nblk),
        in_specs=[rows(D_MODEL), full((1, D_MODEL)), full(wa.shape), full(wgd.shape), full(wz.shape),
                  full(wab_hi.shape), full(wab_lo.shape)],
        out_specs=[pl.BlockSpec((None, None, tm, width), lambda b, i: (b, 0, i, 0)), strided(d4), strided(d16),
                   rows(3 * GDN_WIDTH), rows(GDN_WIDTH), rows(LANES)],
        out_shape=[jax.ShapeDtypeStruct((bsz, d, seq // d, width), BF16) for d in (d1, d4, d16)]
                  + [jax.ShapeDtypeStruct((n, 3 * GDN_WIDTH), BF16),
                     jax.ShapeDtypeStruct((n, GDN_WIDTH), BF16),
                     jax.ShapeDtypeStruct((n, LANES), F32)],
        scratch_shapes=[pltpu.VMEM((width // LANES, tm, LANES), F32)],
        compiler_params=_params("parallel", "parallel"),
        name="inproj",
    )(x2, norm_mix.reshape(1, D_MODEL), wa, wgd, wz, wab_hi, wab_lo)


ATT_QB = 128
ATT_KB = ATT_QB + 2 * ATT_W
ATT_GROUP = 8


def _attn_bias(slopes, dil):
    row = jnp.arange(ATT_QB)[:, None]
    col = jnp.arange(ATT_KB)[None, :]
    rel = jnp.stack([jnp.abs(col - row - var * ATT_W) for var in range(3)])
    dist = (rel * dil).astype(F32)
    bias = jnp.where(rel <= ATT_W, -slopes[:, None, None, None] * dist, NEG_BIG)
    bias = bias.reshape(SWA_HEADS // 2, 2, 3, ATT_QB, ATT_KB).transpose(0, 2, 1, 3, 4)
    return bias.reshape(SWA_HEADS // 2, 6, ATT_QB, ATT_KB)


def _attn_body(bias_ref, q_ref, k_ref, v_ref, o_ref, lse_ref, *, n_qb, seq_l, n_pw, n_res):
    lb = pl.program_id(3)
    lane = lax.broadcasted_iota(jnp.int32, (1, LANES), 1)
    left = lane < HEAD_DIM
    group = min(ATT_GROUP, n_qb)

    def qgroup(gi, carry):
        rows, var, kb, vb, q, pw_of, cols = [], [], [], [], [], [], []
        for j in range(group):
            qi = gi * group + j
            n0 = (lb * n_qb + qi) * ATT_QB
            kstart = pl.multiple_of(jnp.clip(n0 - ATT_W, 0, seq_l - ATT_KB), ATT_W)
            for rr in range(n_res):
                for pw in range(n_pw):
                    cs = slice(pw * LANES, (pw + 1) * LANES)
                    var.append((n0 - kstart) // ATT_W)
                    rows.append((rr, pl.ds(pl.multiple_of(qi * ATT_QB, ATT_QB), ATT_QB)))
                    q.append(q_ref[rows[-1] + (cs,)] * jnp.asarray(HEAD_DIM ** -0.5, BF16))
                    kb.append(k_ref[rr, pl.ds(kstart, ATT_KB), cs])
                    vb.append(v_ref[rr, pl.ds(kstart, ATT_KB), cs])
                    pw_of.append(pw)
                    cols.append(cs)
        blocks = range(group * n_res * n_pw)
        units = [(j, h) for j in blocks for h in range(2)]
        mine = [left, jnp.logical_not(left)]
        s = {(j, h): _dot_nt(jnp.where(mine[h], q[j], jnp.zeros_like(q[j])), kb[j])
                     + bias_ref[pw_of[j], var[j] * 2 + h] for (j, h) in units}
        m = {u: jnp.max(s[u], axis=-1, keepdims=True) for u in units}
        p = {u: jnp.exp(s[u] - m[u]).astype(BF16) for u in units}
        acc = {(j, h): _dot(p[(j, h)], jnp.where(mine[h], vb[j], jnp.ones_like(vb[j]))) for (j, h) in units}
        for j in blocks:
            num = jnp.where(left, acc[(j, 0)], acc[(j, 1)])
            den = pltpu.roll(jnp.where(left, acc[(j, 1)], acc[(j, 0)]), HEAD_DIM, 1)
            mx = jnp.where(left, m[(j, 0)], m[(j, 1)])
            o_ref[rows[j] + (cols[j],)] = (num / den).astype(BF16)
            lse_ref[rows[j] + (cols[j],)] = mx + jnp.log(den)
        return carry

    lax.fori_loop(0, n_qb // group, qgroup, 0)


def _attn_branch(qkv, slopes, dil):
    bsz, _, seq_l, _ = qkv.shape
    lblk = min(2048, seq_l)
    n_qb = lblk // ATT_QB
    n_pw = 1 if n_qb >= ATT_GROUP else 2
    n_res = min(dil, max(1, ATT_GROUP // (n_qb * n_pw)))
    n_steps = SWA_HEADS // 2 // n_pw
    width = n_pw * LANES
    qspec = pl.BlockSpec((None, n_res, lblk, width), lambda p, b, r, l: (b, r, l, p))
    kspec = pl.BlockSpec((None, n_res, seq_l, width), lambda p, b, r, l: (b, r, 0, n_steps + p))
    vspec = pl.BlockSpec((None, n_res, seq_l, width), lambda p, b, r, l: (b, r, 0, 2 * n_steps + p))
    ospec = pl.BlockSpec((None, n_res, lblk, width), lambda p, b, r, l: (b, r, l, p))
    bspec = pl.BlockSpec((n_pw, 6, ATT_QB, ATT_KB), lambda p, b, r, l: (p, 0, 0, 0))
    return pl.pallas_call(
        functools.partial(_attn_body, n_qb=n_qb, seq_l=seq_l, n_pw=n_pw, n_res=n_res),
        grid=(n_steps, bsz, dil // n_res, seq_l // lblk),
        in_specs=[bspec, qspec, kspec, vspec],
        out_specs=[ospec, ospec],
        out_shape=[jax.ShapeDtypeStruct((bsz, dil, seq_l, SWA_WIDTH), BF16),
                   jax.ShapeDtypeStruct((bsz, dil, seq_l, SWA_WIDTH), F32)],
        compiler_params=_params("parallel", "parallel", "parallel", "arbitrary"),
        name=f"dilated_attn_d{dil}",
    )(_attn_bias(slopes, dil), qkv, qkv, qkv)


def _memkv_body(m_ref, g_ref, w_ref, k_ref, v_ref):
    h = _rms(m_ref[...], g_ref[...]).astype(BF16)
    kv = _dot(h, w_ref[...])
    width = MEM_HEADS * MEM_HEAD_DIM
    k_ref[...] = (kv[:, :width] * (MEM_HEAD_DIM ** -0.5)).astype(BF16)
    v_ref[...] = kv[:, width:].astype(BF16)


def _memkv(mem2, norm_kv, w_mkv, tm=256):
    n = mem2.shape[0]
    width = MEM_HEADS * MEM_HEAD_DIM
    return pl.pallas_call(
        _memkv_body,
        grid=(n // tm,),
        in_specs=[pl.BlockSpec((tm, D_MODEL), lambda i: (i, 0)),
                  pl.BlockSpec((1, D_MODEL), lambda i: (0, 0)),
                  pl.BlockSpec((D_MODEL, 2 * width), lambda i: (0, 0))],
        out_specs=[pl.BlockSpec((tm, width), lambda i: (i, 0))] * 2,
        out_shape=[jax.ShapeDtypeStruct((n, width), BF16)] * 2,
        compiler_params=_params("parallel"),
        name="mem_kv",
    )(mem2, norm_kv.reshape(1, D_MODEL), w_mkv.astype(BF16))


ROUTE_COLS = N_GROUPS + N_EXPERTS


TRUNK_TM = 512
TRUNK_SUB = 256


def _trunk_body(x_ref, o1_ref, o2_ref, o3_ref, l1_ref, l2_ref, l3_ref, ob_ref,
                wo_a_ref, wo_b_ref, gq_ref, wq_ref, k_ref, v_ref, wmo_ref,
                gf_ref, wr_hi_ref, wr_lo_ref,
                x2_ref, h2_ref, route_ref, count_ref, run_ref, obuf_ref, lbuf_ref):
    first = jnp.logical_and(pl.program_id(0) == 0, pl.program_id(1) == 0)

    @pl.when(first)
    def _():
        run_ref[...] = jnp.zeros_like(run_ref)

    n_cb = SWA_WIDTH // LANES
    for bi, (o_ref, l_ref) in enumerate(((o2_ref, l2_ref), (o3_ref, l3_ref))):
        dil = DILATIONS[bi + 1]
        for r in range(dil):
            rows = pl.ds(r, TRUNK_TM // dil, stride=dil)
            o_r = o_ref[r].astype(F32)
            l_r = l_ref[r]
            for c in range(n_cb):
                obuf_ref[bi * n_cb + c, rows, :] = o_r[:, c * LANES:(c + 1) * LANES]
                lbuf_ref[bi * n_cb + c, rows, :] = l_r[:, c * LANES:(c + 1) * LANES]

    slabs = [slice(j * TRUNK_SUB, (j + 1) * TRUNK_SUB) for j in range(TRUNK_TM // TRUNK_SUB)]

    def token_major(buf_ref, bi, sl):
        return jnp.concatenate([buf_ref[bi * n_cb + c, sl, :] for c in range(n_cb)], axis=-1)

    def merge(sl):
        l1, l2, l3 = l1_ref[sl, :], token_major(lbuf_ref, 0, sl), token_major(lbuf_ref, 1, sl)
        mx = jnp.maximum(jnp.maximum(l1, l2), l3)
        e1, e2, e3 = jnp.exp(l1 - mx), jnp.exp(l2 - mx), jnp.exp(l3 - mx)
        ya = (e1 * o1_ref[sl, :].astype(F32) + e2 * token_major(obuf_ref, 0, sl)
              + e3 * token_major(obuf_ref, 1, sl))
        return (ya / (e1 + e2 + e3)).astype(BF16)

    cols = [slice(h * MEM_HEAD_DIM, (h + 1) * MEM_HEAD_DIM) for h in range(MEM_HEADS)]
    lane = lax.broadcasted_iota(jnp.int32, (TRUNK_SUB, LANES), 1)
    big = jnp.int32(LANES)
    is_g = lane < N_GROUPS
    r_i = lax.broadcasted_iota(jnp.int32, (TRUNK_SUB, TRUNK_SUB), 0)
    c_i = lax.broadcasted_iota(jnp.int32, (TRUNK_SUB, TRUNK_SUB), 1)
    tri = jnp.where(c_i < r_i, 1.0, 0.0).astype(BF16)

    def top2(logits):
        lg = jnp.where(is_g, logits, NEG_BIG)
        mg = jnp.max(lg, axis=-1, keepdims=True)
        g_idx = jnp.min(jnp.where(jnp.logical_and(is_g, lg == mg), lane, big), axis=-1, keepdims=True)
        g_w = 1.0 / jnp.sum(jnp.exp(lg - mg), axis=-1, keepdims=True)
        lo_lane = N_GROUPS + g_idx * EXPERTS_PER_GROUP
        in_grp = jnp.logical_and(lane >= lo_lane, lane < lo_lane + EXPERTS_PER_GROUP)
        le = jnp.where(in_grp, logits, NEG_BIG)
        m1 = jnp.max(le, axis=-1, keepdims=True)
        i1 = jnp.min(jnp.where(jnp.logical_and(in_grp, le == m1), lane, big), axis=-1, keepdims=True)
        le2 = jnp.where(lane == i1, NEG_BIG, le)
        m2 = jnp.max(le2, axis=-1, keepdims=True)
        i2 = jnp.min(jnp.where(jnp.logical_and(in_grp, le2 == m2), lane, big), axis=-1, keepdims=True)
        r2 = jnp.exp(m2 - m1)
        return i1, i2, g_w / (1.0 + r2), g_w * r2 / (1.0 + r2)

    col_sums = []

    def slab(sl):
        ya = merge(sl)
        yield
        x1 = x_ref[sl, :] + _dot(ya, wo_a_ref[...]) + _dot(ob_ref[sl, :], wo_b_ref[...])
        yield
        q = _dot(_rms(x1, gq_ref[...]).astype(BF16), wq_ref[...]).astype(BF16)
        yield
        s = [_dot_nt(q[:, cs], k_ref[:, cs]) for cs in cols]
        yield
        p = [jnp.exp(sh - jnp.max(sh, axis=-1, keepdims=True)) for sh in s]
        den = [jnp.sum(ph, axis=-1, keepdims=True) for ph in p]
        yield
        pv = [_dot(ph.astype(BF16), v_ref[:, cs]) for ph, cs in zip(p, cols)]
        yield
        heads = [(a / d).astype(BF16) for a, d in zip(pv, den)]
        x2 = x1 + _dot(jnp.concatenate(heads, axis=-1), wmo_ref[...])
        x2_ref[sl, :] = x2
        yield
        h2 = _rms(x2, gf_ref[...])
        h2_ref[sl, :] = _pack_rows(h2)
        h_hi, h_lo = _split2(h2)
        logits = _dot(h_hi, wr_hi_ref[...]) + _dot(h_lo, wr_hi_ref[...]) + _dot(h_hi, wr_lo_ref[...])
        yield
        i1, i2, gate1, gate2 = top2(logits)
        yield
        oh = jnp.where(jnp.logical_or(lane == i1, lane == i2), 1.0, 0.0)
        prefix = _dot(tri, oh.astype(BF16))
        earlier = list(col_sums)
        col_sums.append(jnp.sum(oh, axis=0, keepdims=True))
        yield
        before = prefix + sum(earlier, run_ref[...])
        rank1 = jnp.sum(jnp.where(lane == i1, before, 0.0), axis=-1, keepdims=True)
        rank2 = jnp.sum(jnp.where(lane == i2, before, 0.0), axis=-1, keepdims=True)
        route = jnp.where(lane == 0, (i1 - N_GROUPS).astype(F32), 0.0)
        for j, val in enumerate(((i2 - N_GROUPS).astype(F32), gate1, gate2, rank1, rank2), start=1):
            route = jnp.where(lane == j, val, route)
        route_ref[sl, :] = route

    live = [slab(sl) for sl in slabs]
    while live:
        live = [h for h in live if next(h, StopIteration) is not StopIteration]
    total = sum(col_sums, run_ref[...])
    run_ref[...] = total
    count_ref[...] = total


def _trunk(x2d, branches, ob, mem_k, mem_v, w_out, norm_mem_q, w_mq, w_mo, norm_ffn,
           w_router_g, w_router_e, bsz, seq):
    tm = TRUNK_TM
    n = bsz * seq
    nblk = seq // tm
    (o1, l1), (o2, l2), (o3, l3) = branches
    wo = w_out.astype(BF16)
    wr = jnp.concatenate([w_router_g, jnp.moveaxis(w_router_e, 0, 1).reshape(D_MODEL, N_EXPERTS)], axis=1)
    wr = jnp.pad(wr, ((0, 0), (0, LANES - ROUTE_COLS)))
    wr_hi, wr_lo = _split2(wr)
    rows = lambda w: pl.BlockSpec((tm, w), lambda b, i: (b * nblk + i, 0))
    full = lambda shape: pl.BlockSpec(shape, lambda b, i: (0, 0))
    memspec = pl.BlockSpec((None, mem_k.shape[1], mem_k.shape[2]), lambda b, i: (b, 0, 0))
    d1, d4, d16 = DILATIONS
    natural = pl.BlockSpec((None, None, tm, SWA_WIDTH), lambda b, i: (b, 0, i, 0))
    strided = lambda d: pl.BlockSpec((None, d, tm // d, SWA_WIDTH), lambda b, i: (b, 0, i, 0))
    branch_specs = [natural, strided(d4), strided(d16)]
    return pl.pallas_call(
        _trunk_body,
        grid=(bsz, nblk),
        in_specs=[rows(D_MODEL)] + branch_specs + branch_specs + [rows(GDN_WIDTH)]
                 + [full((SWA_WIDTH, D_MODEL)), full((GDN_WIDTH, D_MODEL)), full((1, D_MODEL)),
                    full((D_MODEL, D_MODEL)), memspec, memspec, full((D_MODEL, D_MODEL)),
                    full((1, D_MODEL)), full((D_MODEL, LANES)), full((D_MODEL, LANES))],
        out_specs=[rows(D_MODEL), rows(PACKED), rows(LANES), full((1, LANES))],
        out_shape=[jax.ShapeDtypeStruct((n, D_MODEL), F32), jax.ShapeDtypeStruct((n, PACKED), jnp.uint32),
                   jax.ShapeDtypeStruct((n, LANES), F32), jax.ShapeDtypeStruct((1, LANES), F32)],
        scratch_shapes=[pltpu.VMEM((1, LANES), F32), pltpu.VMEM((2 * SWA_WIDTH // LANES, tm, LANES), F32),
                        pltpu.VMEM((2 * SWA_WIDTH // LANES, tm, LANES), F32)],
        compiler_params=_params("arbitrary", "arbitrary"),
        name="trunk",
    )(x2d, o1, o2, o3, l1, l2, l3, ob, wo[:SWA_WIDTH], wo[SWA_WIDTH:], norm_mem_q.reshape(1, D_MODEL),
      w_mq.astype(BF16), mem_k, mem_v, w_mo.astype(BF16), norm_ffn.reshape(1, D_MODEL), wr_hi, wr_lo)


DMA_UNROLL = 8
PACKED = D_MODEL // 2
HI_MASK = 0xFFFF0000


def _pack_rows(x):
    bits = lambda v: lax.bitcast_convert_type(v.astype(BF16).astype(F32), jnp.uint32)
    return (bits(x[:, :PACKED]) & jnp.uint32(HI_MASK)) | (bits(x[:, PACKED:]) >> 16)


def _unpack_rows(u):
    hi = lax.bitcast_convert_type(u & jnp.uint32(HI_MASK), F32)
    lo = lax.bitcast_convert_type(u << 16, F32)
    return hi, lo


def _dispatch_body(dest_ref, h_ref, xs_in_ref, xs_ref, sem, *, tm):
    del xs_in_ref

    def issue(t, c):
        for k in range(TOP_K):
            dst = xs_ref.at[pl.ds(dest_ref[0, 0, t * TOP_K + k], 1)]
            pltpu.make_async_copy(h_ref.at[pl.ds(t, 1)], dst, sem).start(priority=k % 2)
        return c

    lax.fori_loop(0, tm, issue, 0, unroll=DMA_UNROLL)
    for _ in range(TOP_K):
        pltpu.make_async_copy(h_ref, xs_ref.at[pl.ds(0, tm)], sem).wait()


def _dispatch(h2, dest, n_slots, tm=4096):
    n = h2.shape[0]
    dest3 = dest.reshape(n // tm, 1, tm * TOP_K)
    xs0 = jnp.zeros((n_slots, PACKED), jnp.uint32)
    return pl.pallas_call(
        functools.partial(_dispatch_body, tm=tm),
        grid=(n // tm,),
        in_specs=[pl.BlockSpec((1, 1, tm * TOP_K), lambda i: (i, 0, 0), memory_space=pltpu.SMEM),
                  pl.BlockSpec((tm, PACKED), lambda i: (i, 0)),
                  pl.BlockSpec(memory_space=pl.ANY)],
        out_specs=pl.BlockSpec(memory_space=pl.ANY),
        out_shape=jax.ShapeDtypeStruct((n_slots, PACKED), jnp.uint32),
        scratch_shapes=[pltpu.SemaphoreType.DMA(())],
        input_output_aliases={2: 0},
        compiler_params=_params("arbitrary"),
        name="moe_dispatch",
    )(dest3, h2, xs0)


def _expert_body(be_ref, nused_ref, x_ref, wg_ref, wu_ref, wd_ref, y_ref):
    i = pl.program_id(0)

    @pl.when(i < nused_ref[0])
    def _():
        x = jnp.concatenate([half.astype(BF16) for half in _unpack_rows(x_ref[...])], axis=-1)
        a = _dot(x, wg_ref[...])
        b = _dot(x, wu_ref[...])
        hid = (a * jax.nn.sigmoid(a) * b).astype(BF16)
        y_ref[...] = _pack_rows(_dot(hid, wd_ref[...]))

    @pl.when(i >= nused_ref[0])
    def _():
        y_ref[...] = jnp.zeros_like(y_ref)


def _experts(xs, block_expert, n_used, w_gate, w_up, w_down):
    n_slots = xs.shape[0]
    n_blocks = n_slots // ROUTE_BLOCK
    grid_spec = pltpu.PrefetchScalarGridSpec(
        num_scalar_prefetch=2,
        grid=(n_blocks,),
        in_specs=[pl.BlockSpec((ROUTE_BLOCK, PACKED), lambda i, be, nu: (i, 0)),
                  pl.BlockSpec((None, D_MODEL, D_EXPERT), lambda i, be, nu: (be[i], 0, 0)),
                  pl.BlockSpec((None, D_MODEL, D_EXPERT), lambda i, be, nu: (be[i], 0, 0)),
                  pl.BlockSpec((None, D_EXPERT, D_MODEL), lambda i, be, nu: (be[i], 0, 0))],
        out_specs=pl.BlockSpec((ROUTE_BLOCK, PACKED), lambda i, be, nu: (i, 0)),
    )
    return pl.pallas_call(
        _expert_body,
        grid_spec=grid_spec,
        out_shape=jax.ShapeDtypeStruct((n_slots, PACKED), jnp.uint32),
        compiler_params=_params("arbitrary"),
        name="moe_experts",
    )(block_expert, n_used, xs, w_gate, w_up, w_down)


def _combine_body(dest_ref, x_ref, route_ref, g_ref, yb_ref, o_ref, buf_ref, sem, *, tm):
    def issue(t, c):
        for k in range(TOP_K):
            src = yb_ref.at[pl.ds(dest_ref[0, 0, t * TOP_K + k], 1)]
            pltpu.make_async_copy(src, buf_ref.at[k, pl.ds(t, 1)], sem).start(priority=k % 2)
        return c

    lax.fori_loop(0, tm, issue, 0, unroll=DMA_UNROLL)
    for k in range(TOP_K):
        pltpu.make_async_copy(yb_ref.at[pl.ds(0, tm)], buf_ref.at[k], sem).wait()
    route = route_ref[...]
    hi0, lo0 = _unpack_rows(buf_ref[0])
    hi1, lo1 = _unpack_rows(buf_ref[1])
    g0, g1 = route[:, 2:3], route[:, 3:4]
    y = jnp.concatenate([g0 * hi0 + g1 * hi1, g0 * lo0 + g1 * lo1], axis=-1)
    o_ref[...] = _rms(x_ref[...] + y, g_ref[...])


def _combine(x2, route, dest, yb, norm_final, tm=1024):
    n = x2.shape[0]
    dest3 = dest.reshape(n // tm, 1, tm * TOP_K)
    return pl.pallas_call(
        functools.partial(_combine_body, tm=tm),
        grid=(n // tm,),
        in_specs=[pl.BlockSpec((1, 1, tm * TOP_K), lambda i: (i, 0, 0), memory_space=pltpu.SMEM),
                  pl.BlockSpec((tm, D_MODEL), lambda i: (i, 0)),
                  pl.BlockSpec((tm, LANES), lambda i: (i, 0)),
                  pl.BlockSpec((1, D_MODEL), lambda i: (0, 0)),
                  pl.BlockSpec(memory_space=pl.ANY)],
        out_specs=pl.BlockSpec((tm, D_MODEL), lambda i: (i, 0)),
        out_shape=jax.ShapeDtypeStruct((n, D_MODEL), F32),
        scratch_shapes=[pltpu.VMEM((TOP_K, tm, PACKED), jnp.uint32), pltpu.SemaphoreType.DMA(())],
        compiler_params=_params("arbitrary"),
        name="moe_combine",
    )(dest3, x2, route, norm_final.reshape(1, D_MODEL), yb)


def _moe(x2, h2, route, counts, w_gate, w_up, w_down, norm_final):
    n = x2.shape[0]
    m_slots = n * TOP_K
    n_blocks = -(-(m_slots + N_EXPERTS * (ROUTE_BLOCK - 1)) // ROUTE_BLOCK)
    cnt = counts[0, N_GROUPS:N_GROUPS + N_EXPERTS].astype(jnp.int32)
    padded = (cnt + ROUTE_BLOCK - 1) // ROUTE_BLOCK * ROUTE_BLOCK
    pad_end = jnp.cumsum(padded)
    seg_start = pad_end - padded
    eid = route[:, 0:TOP_K].astype(jnp.int32)
    rank = route[:, 4:4 + TOP_K].astype(jnp.int32)
    dest = (seg_start[eid] + rank).reshape(-1)
    block_start = jnp.arange(n_blocks, dtype=jnp.int32) * ROUTE_BLOCK
    block_expert = jnp.minimum(jnp.sum(pad_end[None, :] <= block_start[:, None], axis=1), N_EXPERTS - 1)
    block_expert = block_expert.astype(jnp.int32)
    n_used = (pad_end[-1:] // ROUTE_BLOCK).astype(jnp.int32)
    xs = _dispatch(h2, dest, n_blocks * ROUTE_BLOCK)
    yb = _experts(xs, block_expert, n_used, w_gate, w_up, w_down)
    return _combine(x2, route, dest, yb, norm_final)


GDN_TB = 1024
GDN_HALO = 16
GDN_GROUP = 4
GDN_HPG = 2
GDN_GW = GDN_HPG * HEAD_DIM


def _gdn_body(gd_ref, prev_ref, next_ref, abr_ref, cw_ref, alog_ref, dt_ref, *rest, rev, final):
    if final:
        z_ref, oprev_ref, gn_ref, o_ref, xpad_ref, state_ref = rest
    else:
        o_ref, xpad_ref, state_ref = rest
    i = pl.program_id(1)
    nblk = pl.num_programs(1)
    n_chunks = GDN_TB // CHUNK
    n_grp = GDN_HEADS // GDN_HPG
    width3 = 3 * GDN_WIDTH

    @pl.when(i == 0)
    def _():
        state_ref[...] = jnp.zeros_like(state_ref)

    blk = (nblk - 1 - i) if rev else i
    zero_halo = jnp.zeros((GDN_HALO, width3), BF16)
    xpad_ref[:GDN_HALO, :] = jnp.where(blk == 0, zero_halo, prev_ref[...])
    xpad_ref[GDN_HALO:GDN_HALO + GDN_TB, :] = gd_ref[...]
    xpad_ref[GDN_HALO + GDN_TB:, :] = jnp.where(blk == nblk - 1, zero_halo, next_ref[...])
    win = CHUNK + 2 * GDN_HALO
    side_taps = [j for j in range(CONV_K) if j != CONV_K // 2]
    sr = lax.broadcasted_iota(jnp.int32, (len(side_taps) * CHUNK, 1), 0)
    sc = lax.broadcasted_iota(jnp.int32, (1, win), 1)
    tap = sr // CHUNK
    tap = tap + jnp.where(tap >= CONV_K // 2, 1, 0)
    shift_sel = jnp.where(sc == sr % CHUNK + tap + (GDN_HALO - CONV_K // 2), 1.0, 0.0).astype(BF16)

    lane = lax.broadcasted_iota(jnp.int32, (1, GDN_GW), 1)
    tok = lane % CHUNK
    row = lax.broadcasted_iota(jnp.int32, (CHUNK, 1), 0)
    causal = (tok >= row) if rev else (tok <= row)
    strict = (tok > row) if rev else (tok < row)
    eye = jnp.where(tok == row, 1.0, 0.0)
    r2 = lax.broadcasted_iota(jnp.int32, (GDN_GW, 1), 0)
    same_head = (r2 // CHUNK) == (lane // CHUNK)
    bd_ones = jnp.where(same_head, 1.0, 0.0).astype(BF16)
    tri_in = (r2 % CHUNK >= tok) if rev else (r2 % CHUNK <= tok)
    tri_bd = jnp.where(jnp.logical_and(same_head, tri_in), 1.0, 0.0).astype(BF16)
    tri_and_ones = jnp.concatenate([tri_bd, bd_ones], axis=1)
    row8 = lax.broadcasted_iota(jnp.int32, (8, 1), 0)

    def blockdiag(x):
        xb = x.astype(BF16)
        zero = jnp.zeros_like(xb)
        return jnp.concatenate([jnp.where(lane // CHUNK == h, xb, zero) for h in range(GDN_HPG)], axis=0)

    def wide_mm(x, y):
        return _dot(x.astype(BF16), blockdiag(y))

    def head_sums(x):
        return _dot(x.astype(BF16), bd_ones)

    def local_phase(chunks, res):
        units = [(ci, p) for ci in range(len(chunks)) for p in range(n_grp)]
        r0 = [c * CHUNK for c in chunks]

        gates, gc_rows, gl_rows = [], [], []
        for c in chunks:
            graw = abr_ref[c]
            xg = graw + dt_ref[...]
            softplus = jnp.maximum(xg, 0.0) + jnp.log(1.0 + jnp.exp(-jnp.abs(xg)))
            g = jnp.where(row8 < n_grp, jax.nn.sigmoid(graw), -jnp.exp(alog_ref[...]) * softplus)
            g3 = _split3(g)
            gates.append(g)
            sums = sum(_dot(t, tri_and_ones) for t in g3)
            gc_rows.append(sums[:, :GDN_GW])
            gl_rows.append(sums[:, GDN_GW:])
        yield

        shifted = [_dot(shift_sel, xpad_ref[r0[ci]:r0[ci] + win, :]) for ci in range(len(chunks))]
        yield

        def conv(ci, p, col0):
            cols = slice(col0 + p * GDN_GW, col0 + (p + 1) * GDN_GW)
            centre = xpad_ref[r0[ci] + GDN_HALO:r0[ci] + GDN_HALO + CHUNK, cols].astype(F32)
            acc = centre * cw_ref[CONV_K // 2:CONV_K // 2 + 1, cols]
            for i, j in enumerate(side_taps):
                acc = acc + shifted[ci][i * CHUNK:(i + 1) * CHUNK, cols] * cw_ref[j:j + 1, cols]
            return acc * jax.nn.sigmoid(acc)

        qkv = {u: [conv(*u, col0) for col0 in (0, GDN_WIDTH, 2 * GDN_WIDTH)] for u in units}
        yield
        ss = {u: head_sums(jnp.concatenate([qkv[u][0] * qkv[u][0], qkv[u][1] * qkv[u][1]], axis=0))
              for u in units}
        yield
        prep = {}
        for (ci, p) in units:
            beta_r = gates[ci][p:p + 1]
            g_r = gates[ci][n_grp + p:n_grp + p + 1]
            lhs = jnp.concatenate([jnp.where(causal, g_r, 0.0), eye * beta_r], axis=0)
            prep[(ci, p)] = sum(_dot(t, bd_ones) for t in _split2(lhs))
        yield

        kq, kn_bd, e_gc, k_upd, decay_b = {}, {}, {}, {}, {}
        for u in units:
            ci, p = u
            qn = qkv[u][0] * lax.rsqrt(ss[u][:CHUNK] + EPS) * (HEAD_DIM ** -0.5)
            kn = qkv[u][1] * lax.rsqrt(ss[u][CHUNK:] + EPS)
            kq[u] = jnp.concatenate([kn, qn], axis=0).astype(BF16)
            kn_bd[u] = blockdiag(kn)
            beta_r = gates[ci][p:p + 1]
            gc_r = gc_rows[ci][n_grp + p:n_grp + p + 1]
            gl_r = gl_rows[ci][n_grp + p:n_grp + p + 1]
            gc_b, beta_b = prep[u][:CHUNK], prep[u][CHUNK:]
            decay_b[u] = jnp.exp(jnp.where(causal, gc_b - gc_r, NEG_BIG)) * beta_r
            e_gc[u] = jnp.exp(gc_b)
            k_upd[u] = (kn * (jnp.exp(gl_r - gc_b) * beta_b)).astype(BF16)
        yield

        kk_qk = {u: _dot_nt(kq[u], kn_bd[u]) for u in units}
        yield
        lpow = {u: jnp.where(strict, kk_qk[u][:CHUNK] * decay_b[u], 0.0) for u in units}
        amat = {u: kk_qk[u][CHUNK:] * decay_b[u] for u in units}
        n_sq = 5
        lpow = {u: -lpow[u] for u in units}
        tinv = {u: eye + lpow[u] for u in units}
        lpow = {u: wide_mm(lpow[u], lpow[u]) for u in units}
        yield
        for k in range(n_sq):
            if k < n_sq - 1:
                prod = {u: _dot(lpow[u].astype(BF16),
                                jnp.concatenate([blockdiag(tinv[u]), blockdiag(lpow[u])], axis=1)) for u in units}
                lpow = {u: prod[u][:, GDN_GW:] for u in units}
            else:
                prod = {u: wide_mm(lpow[u], tinv[u]) for u in units}
            tinv = {u: tinv[u] + prod[u][:, :GDN_GW] for u in units}
            yield
        res.update(r0=r0, kq=kq, tinv=tinv, amat=amat, e_gc=e_gc, k_upd=k_upd, gl_rows=gl_rows,
                   v={u: qkv[u][2] for u in units})

    def scan_phase(res):
        for ci in range(len(res["r0"])):
            us = [(ci, p) for p in range(n_grp)]
            state = {u: state_ref[u[1]] for u in us}
            pq = {u: _dot(res["kq"][u], state[u].astype(BF16)) for u in us}
            yield
            vhat = {u: wide_mm(res["tinv"][u], res["v"][u] - res["e_gc"][u] * pq[u][:CHUNK]) for u in us}
            yield
            upd = {u: _dot_tn(res["k_upd"][u], vhat[u].astype(BF16)) for u in us}
            for u in us:
                gl_r = res["gl_rows"][ci][n_grp + u[1]:n_grp + u[1] + 1]
                state_ref[u[1]] = state[u] * jnp.exp(gl_r) + jnp.where(same_head, upd[u], 0.0)
            yield
            o = {u: res["e_gc"][u] * pq[u][CHUNK:] + wide_mm(res["amat"][u], vhat[u]) for u in us}
            for u in us:
                cs = slice(u[1] * GDN_GW, (u[1] + 1) * GDN_GW)
                rows = slice(res["r0"][ci], res["r0"][ci] + CHUNK)
                if final:
                    ob = o[u] + oprev_ref[rows, cs]
                    ms = head_sums(ob * ob) * (1.0 / HEAD_DIM)
                    zz = z_ref[rows, cs].astype(F32)
                    out = ob * lax.rsqrt(ms + EPS) * gn_ref[:, cs] * (zz * jax.nn.sigmoid(zz))
                    o_ref[rows, cs] = out.astype(o_ref.dtype)
                else:
                    o_ref[rows, cs] = o[u]
            yield

    def emit_interleaved(*gens):
        live = list(gens)
        while live:
            for g in list(live):
                if next(g, StopIteration) is StopIteration:
                    live.remove(g)

    order = list(range(n_chunks))[::-1] if rev else list(range(n_chunks))
    groups = [order[j:j + GDN_GROUP] for j in range(0, n_chunks, GDN_GROUP)]
    results = [dict() for _ in groups]
    emit_interleaved(local_phase(groups[0], results[0]))
    for gi in range(len(groups)):
        nxt = [local_phase(groups[gi + 1], results[gi + 1])] if gi + 1 < len(groups) else []
        emit_interleaved(scan_phase(results[gi]), *nxt)


def _gdn_pass(gd3, abr, conv_w, alog_row, dt_row, rev, final_args=None):
    bsz, seq, width3 = gd3.shape
    nblk = seq // GDN_TB
    hpb = GDN_TB // GDN_HALO
    blk = (lambda i: nblk - 1 - i) if rev else (lambda i: i)
    main = lambda w: pl.BlockSpec((None, GDN_TB, w), lambda b, i: (b, blk(i), 0))
    full = lambda shape: pl.BlockSpec(shape, lambda b, i: (0,) * len(shape))
    in_specs = [main(width3),
                pl.BlockSpec((None, GDN_HALO, width3), lambda b, i: (b, jnp.maximum(blk(i) * hpb - 1, 0), 0)),
                pl.BlockSpec((None, GDN_HALO, width3),
                             lambda b, i: (b, jnp.minimum((blk(i) + 1) * hpb, seq // GDN_HALO - 1), 0)),
                pl.BlockSpec((None, GDN_TB // CHUNK, 8, GDN_GW), lambda b, i: (b, blk(i), 0, 0)),
                full((CONV_K, width3)), full((8, GDN_GW)), full((8, GDN_GW))]
    args = [gd3, gd3, gd3, abr, conv_w, alog_row, dt_row]
    final = final_args is not None
    if final:
        z3, o_prev, gn = final_args
        in_specs += [main(GDN_WIDTH), main(GDN_WIDTH), full((1, GDN_WIDTH))]
        args += [z3, o_prev, gn]
    return pl.pallas_call(
        functools.partial(_gdn_body, rev=rev, final=final),
        grid=(bsz, nblk),
        in_specs=in_specs,
        out_specs=main(GDN_WIDTH),
        out_shape=jax.ShapeDtypeStruct((bsz, seq, GDN_WIDTH), BF16 if final else F32),
        scratch_shapes=[pltpu.VMEM((GDN_TB + 2 * GDN_HALO, width3), BF16),
                        pltpu.VMEM((GDN_HEADS // GDN_HPG, GDN_GW, GDN_GW), F32)],
        compiler_params=_params("parallel", "arbitrary"),
        name="gdn_bwd" if rev else "gdn_fwd",
    )(*args)


def _group_rows(x, bsz, seq):
    x = x.reshape(bsz, seq // CHUNK, CHUNK, GDN_HEADS // GDN_HPG, GDN_HPG)
    return x.transpose(0, 1, 3, 4, 2).reshape(bsz, seq // CHUNK, GDN_HEADS // GDN_HPG, GDN_GW)


def _gdn(gd, z, ab, conv_w, a_log, dt_bias, gdn_norm, bsz, seq):
    gd3 = gd.reshape(bsz, seq, 3 * GDN_WIDTH)
    z3 = z.reshape(bsz, seq, GDN_WIDTH)
    gn = jnp.tile(gdn_norm, GDN_HEADS).reshape(1, GDN_WIDTH)
    out = None
    for d in range(2):
        n_grp = GDN_HEADS // GDN_HPG
        beta = _group_rows(ab[:, d * GDN_HEADS:(d + 1) * GDN_HEADS], bsz, seq)
        araw = _group_rows(ab[:, (2 + d) * GDN_HEADS:(3 + d) * GDN_HEADS], bsz, seq)
        abr = jnp.pad(jnp.concatenate([beta, araw], axis=2), ((0, 0), (0, 0), (0, 8 - 2 * n_grp), (0, 0)))
        per_head = lambda v: jnp.pad(jnp.repeat(v[d], CHUNK).reshape(n_grp, GDN_GW), ((n_grp, 8 - 2 * n_grp), (0, 0)))
        final_args = None if d == 0 else (z3, out, gn)
        out = _gdn_pass(gd3, abr, conv_w, per_head(a_log), per_head(dt_bias), rev=(d == 1), final_args=final_args)
    return out.reshape(bsz * seq, GDN_WIDTH)


def _encoder(x, mem, norm_mix, w_in, conv_w, a_log, dt_bias, gdn_norm, w_out, norm_mem_q, norm_mem_kv,
             w_mq, w_mkv, w_mo, norm_ffn, w_router_g, w_router_e, w_gate, w_up, w_down, norm_final):
    bsz, seq, _ = x.shape
    x2d = x.reshape(bsz * seq, D_MODEL)
    qkv1, qkv4, qkv16, gd, z, ab = _inproj(x2d, norm_mix, w_in, bsz, seq)
    slopes = jnp.exp2(-8.0 * jnp.arange(1, SWA_HEADS + 1, dtype=F32) / SWA_HEADS)
    branches = [_attn_branch(qkv, slopes, d) for qkv, d in zip((qkv1, qkv4, qkv16), DILATIONS)]
    ob = _gdn(gd, z, ab, conv_w, a_log, dt_bias, gdn_norm, bsz, seq)
    mem_k, mem_v = _memkv(mem.reshape(-1, D_MODEL), norm_mem_kv, w_mkv)
    mem_k = mem_k.reshape(bsz, -1, MEM_HEADS * MEM_HEAD_DIM)
    mem_v = mem_v.reshape(bsz, -1, MEM_HEADS * MEM_HEAD_DIM)
    x2, h2, route, counts = _trunk(x2d, branches, ob, mem_k, mem_v, w_out, norm_mem_q, w_mq, w_mo,
                                   norm_ffn, w_router_g, w_router_e, bsz, seq)
    y = _moe(x2, h2, route, counts, w_gate, w_up, w_down, norm_final)
    return y.reshape(bsz, seq, D_MODEL)


def kernel(x_prompt, x_sample, mem_prompt, mem_sample, norm_mix, w_in, conv_w, a_log, dt_bias, gdn_norm,
           w_out, norm_mem_q, norm_mem_kv, w_mq, w_mkv, w_mo, norm_ffn, w_router_g, w_router_e,
           w_gate, w_up, w_down, norm_final):
    p = dict(norm_mix=norm_mix[0], w_in=w_in[0], conv_w=conv_w[0], a_log=a_log[0], dt_bias=dt_bias[0],
             gdn_norm=gdn_norm[0], w_out=w_out[0], norm_mem_q=norm_mem_q[0], norm_mem_kv=norm_mem_kv[0],
             w_mq=w_mq[0], w_mkv=w_mkv[0], w_mo=w_mo[0], norm_ffn=norm_ffn[0], w_router_g=w_router_g[0],
             w_router_e=w_router_e[0], w_gate=w_gate[0].astype(BF16), w_up=w_up[0].astype(BF16),
             w_down=w_down[0].astype(BF16), norm_final=norm_final)
    return (_encoder(x_prompt, mem_prompt, **p), _encoder(x_sample, mem_sample, **p))
```

```python
import functools

import jax
import jax.numpy as jnp
from jax import lax
from jax.experimental import pallas as pl
from jax.experimental.pallas import tpu as pltpu

F32 = jnp.float32
BF16 = jnp.bfloat16

D_MODEL = 1024
HEAD_DIM = 64
SWA_HEADS = 8
GDN_HEADS = 8
SWA_WIDTH = SWA_HEADS * HEAD_DIM
GDN_WIDTH = GDN_HEADS * HEAD_DIM
DILATIONS = (1, 4, 16)
ATT_W = 64
CONV_K = 5
CHUNK = 64
MEM_HEADS = 4
MEM_HEAD_DIM = 256
N_GROUPS = 4
EXPERTS_PER_GROUP = 8
N_EXPERTS = 32
TOP_K = 2
D_EXPERT = 512
ROUTE_BLOCK = 512
EPS = 1e-6

LANES = 128
VMEM_LIMIT = 56 * 1024 * 1024
NEG_BIG = -1e30


def _dot(a, b):
    return jnp.dot(a, b, preferred_element_type=F32)


def _dot_nt(a, b):
    return lax.dot_general(a, b, (((1,), (1,)), ((), ())), preferred_element_type=F32)


def _dot_tn(a, b):
    return lax.dot_general(a, b, (((0,), (0,)), ((), ())), preferred_element_type=F32)


def _split2(x):
    hi = x.astype(BF16)
    lo = (x - hi.astype(F32)).astype(BF16)
    return hi, lo


def _split3(x):
    hi = x.astype(BF16)
    r = x - hi.astype(F32)
    mid = r.astype(BF16)
    lo = (r - mid.astype(F32)).astype(BF16)
    return hi, mid, lo


def _rms(x, g):
    return x * lax.rsqrt(jnp.mean(x * x, axis=-1, keepdims=True) + EPS) * g


def _params(*sem):
    return pltpu.CompilerParams(dimension_semantics=sem, vmem_limit_bytes=VMEM_LIMIT)


def _inproj_body(x_ref, g_ref, wa_ref, wgd_ref, wz_ref, wab_hi_ref, wab_lo_ref,
                 qkv1_ref, qkv4_ref, qkv16_ref, gd_ref, z_ref, ab_ref, stage_ref, *, tm):
    h = _rms(x_ref[...], g_ref[...])
    h_hi, h_lo = _split2(h)
    qkv = _dot(h_hi, wa_ref[...])
    qkv1_ref[...] = qkv.astype(BF16)
    for c in range(qkv.shape[1] // LANES):
        cols = slice(c * LANES, (c + 1) * LANES)
        stage_ref[c] = qkv[:, cols]
        for dil, out_ref in ((DILATIONS[1], qkv4_ref), (DILATIONS[2], qkv16_ref)):
            for r in range(dil):
                out_ref[r, :, cols] = stage_ref[c, pl.ds(r, tm // dil, stride=dil), :].astype(BF16)
    gd_ref[...] = _dot(h_hi, wgd_ref[...]).astype(BF16)
    z_ref[...] = _dot(h_hi, wz_ref[...]).astype(BF16)
    wab_hi = wab_hi_ref[...]
    ab = _dot(h_hi, wab_hi) + _dot(h_lo, wab_hi) + _dot(h_hi, wab_lo_ref[...])
    ab_ref[...] = ab


def _inproj(x2, norm_mix, w_in, bsz, seq, tm=512):
    n = x2.shape[0]
    nblk = seq // tm
    width = 3 * SWA_WIDTH
    wa = w_in[:, :3 * SWA_WIDTH].astype(BF16)
    wgd = w_in[:, 3 * SWA_WIDTH:3 * SWA_WIDTH + 3 * GDN_WIDTH].astype(BF16)
    wz = w_in[:, 3 * SWA_WIDTH + 3 * GDN_WIDTH:3 * SWA_WIDTH + 4 * GDN_WIDTH].astype(BF16)
    wab = jnp.pad(w_in[:, 3 * SWA_WIDTH + 4 * GDN_WIDTH:], ((0, 0), (0, LANES - 4 * GDN_HEADS)))
    wab_hi, wab_lo = _split2(wab)
    full = lambda shape: pl.BlockSpec(shape, lambda b, i: (0, 0))
    rows = lambda w: pl.BlockSpec((tm, w), lambda b, i: (b * nblk + i, 0))
    strided = lambda d: pl.BlockSpec((None, d, tm // d, width), lambda b, i: (b, 0, i, 0))
    d1, d4, d16 = DILATIONS
    return pl.pallas_call(
        functools.partial(_inproj_body, tm=tm),
        grid=(bsz, nblk),
        in_specs=[rows(D_MODEL), full((1, D_MODEL)), full(wa.shape), full(wgd.shape), full(wz.shape),
                  full(wab_hi.shape), full(wab_lo.shape)],
        out_specs=[pl.BlockSpec((None, None, tm, width), lambda b, i: (b, 0, i, 0)), strided(d4), strided(d16),
                   rows(3 * GDN_WIDTH), rows(GDN_WIDTH), rows(LANES)],
        out_shape=[jax.ShapeDtypeStruct((bsz, d, seq // d, width), BF16) for d in (d1, d4, d16)]
                  + [jax.ShapeDtypeStruct((n, 3 * GDN_WIDTH), BF16),
                     jax.ShapeDtypeStruct((n, GDN_WIDTH), BF16),
                     jax.ShapeDtypeStruct((n, LANES), F32)],
        scratch_shapes=[pltpu.VMEM((width // LANES, tm, LANES), F32)],
        compiler_params=_params("parallel", "parallel"),
        name="inproj",
    )(x2, norm_mix.reshape(1, D_MODEL), wa, wgd, wz, wab_hi, wab_lo)


ATT_QB = 128
ATT_KB = ATT_QB + 2 * ATT_W
ATT_GROUP = 8


def _attn_bias(slopes, dil):
    row = jnp.arange(ATT_QB)[:, None]
    col = jnp.arange(ATT_KB)[None, :]
    rel = jnp.stack([jnp.abs(col - row - var * ATT_W) for var in range(3)])
    dist = (rel * dil).astype(F32)
    bias = jnp.where(rel <= ATT_W, -slopes[:, None, None, None] * dist, NEG_BIG)
    bias = bias.reshape(SWA_HEADS // 2, 2, 3, ATT_QB, ATT_KB).transpose(0, 2, 1, 3, 4)
    return bias.reshape(SWA_HEADS // 2, 6, ATT_QB, ATT_KB)


def _attn_body(bias_ref, q_ref, k_ref, v_ref, o_ref, lse_ref, *, n_qb, seq_l, n_pw, n_res):
    lb = pl.program_id(3)
    lane = lax.broadcasted_iota(jnp.int32, (1, LANES), 1)
    left = lane < HEAD_DIM
    group = min(ATT_GROUP, n_qb)

    def qgroup(gi, carry):
        rows, var, kb, vb, q, pw_of, cols = [], [], [], [], [], [], []
        for j in range(group):
            qi = gi * group + j
            n0 = (lb * n_qb + qi) * ATT_QB
            kstart = pl.multiple_of(jnp.clip(n0 - ATT_W, 0, seq_l - ATT_KB), ATT_W)
            for rr in range(n_res):
                for pw in range(n_pw):
                    cs = slice(pw * LANES, (pw + 1) * LANES)
                    var.append((n0 - kstart) // ATT_W)
                    rows.append((rr, pl.ds(pl.multiple_of(qi * ATT_QB, ATT_QB), ATT_QB)))
                    q.append(q_ref[rows[-1] + (cs,)] * jnp.asarray(HEAD_DIM ** -0.5, BF16))
                    kb.append(k_ref[rr, pl.ds(kstart, ATT_KB), cs])
                    vb.append(v_ref[rr, pl.ds(kstart, ATT_KB), cs])
                    pw_of.append(pw)
                    cols.append(cs)
        blocks = range(group * n_res * n_pw)
        units = [(j, h) for j in blocks for h in range(2)]
        mine = [left, jnp.logical_not(left)]
        s = {(j, h): _dot_nt(jnp.where(mine[h], q[j], jnp.zeros_like(q[j])), kb[j])
                     + bias_ref[pw_of[j], var[j] * 2 + h] for (j, h) in units}
        m = {u: jnp.max(s[u], axis=-1, keepdims=True) for u in units}
        p = {u: jnp.exp(s[u] - m[u]).astype(BF16) for u in units}
        acc = {(j, h): _dot(p[(j, h)], jnp.where(mine[h], vb[j], jnp.ones_like(vb[j]))) for (j, h) in units}
        for j in blocks:
            num = jnp.where(left, acc[(j, 0)], acc[(j, 1)])
            den = pltpu.roll(jnp.where(left, acc[(j, 1)], acc[(j, 0)]), HEAD_DIM, 1)
            mx = jnp.where(left, m[(j, 0)], m[(j, 1)])
            o_ref[rows[j] + (cols[j],)] = (num / den).astype(BF16)
            lse_ref[rows[j] + (cols[j],)] = mx + jnp.log(den)
        return carry

    lax.fori_loop(0, n_qb // group, qgroup, 0)


def _attn_branch(qkv, slopes, dil):
    bsz, _, seq_l, _ = qkv.shape
    lblk = min(2048, seq_l)
    n_qb = lblk // ATT_QB
    n_pw = 1 if n_qb >= ATT_GROUP else 2
    n_res = min(dil, max(1, ATT_GROUP // (n_qb * n_pw)))
    n_steps = SWA_HEADS // 2 // n_pw
    width = n_pw * LANES
    qspec = pl.BlockSpec((None, n_res, lblk, width), lambda p, b, r, l: (b, r, l, p))
    kspec = pl.BlockSpec((None, n_res, seq_l, width), lambda p, b, r, l: (b, r, 0, n_steps + p))
    vspec = pl.BlockSpec((None, n_res, seq_l, width), lambda p, b, r, l: (b, r, 0, 2 * n_steps + p))
    ospec = pl.BlockSpec((None, n_res, lblk, width), lambda p, b, r, l: (b, r, l, p))
    bspec = pl.BlockSpec((n_pw, 6, ATT_QB, ATT_KB), lambda p, b, r, l: (p, 0, 0, 0))
    return pl.pallas_call(
        functools.partial(_attn_body, n_qb=n_qb, seq_l=seq_l, n_pw=n_pw, n_res=n_res),
        grid=(n_steps, bsz, dil // n_res, seq_l // lblk),
        in_specs=[bspec, qspec, kspec, vspec],
        out_specs=[ospec, ospec],
        out_shape=[jax.ShapeDtypeStruct((bsz, dil, seq_l, SWA_WIDTH), BF16),
                   jax.ShapeDtypeStruct((bsz, dil, seq_l, SWA_WIDTH), F32)],
        compiler_params=_params("parallel", "parallel", "parallel", "arbitrary"),
        name=f"dilated_attn_d{dil}",
    )(_attn_bias(slopes, dil), qkv, qkv, qkv)


def _memkv_body(m_ref, g_ref, w_ref, k_ref, v_ref):
    h = _rms(m_ref[...], g_ref[...]).astype(BF16)
    kv = _dot(h, w_ref[...])
    width = MEM_HEADS * MEM_HEAD_DIM
    k_ref[...] = (kv[:, :width] * (MEM_HEAD_DIM ** -0.5)).astype(BF16)
    v_ref[...] = kv[:, width:].astype(BF16)


def _memkv(mem2, norm_kv, w_mkv, tm=256):
    n = mem2.shape[0]
    width = MEM_HEADS * MEM_HEAD_DIM
    return pl.pallas_call(
        _memkv_body,
        grid=(n // tm,),
        in_specs=[pl.BlockSpec((tm, D_MODEL), lambda i: (i, 0)),
                  pl.BlockSpec((1, D_MODEL), lambda i: (0, 0)),
                  pl.BlockSpec((D_MODEL, 2 * width), lambda i: (0, 0))],
        out_specs=[pl.BlockSpec((tm, width), lambda i: (i, 0))] * 2,
        out_shape=[jax.ShapeDtypeStruct((n, width), BF16)] * 2,
        compiler_params=_params("parallel"),
        name="mem_kv",
    )(mem2, norm_kv.reshape(1, D_MODEL), w_mkv.astype(BF16))


ROUTE_COLS = N_GROUPS + N_EXPERTS


TRUNK_TM = 512
TRUNK_SUB = 256


def _trunk_body(x_ref, o1_ref, o2_ref, o3_ref, l1_ref, l2_ref, l3_ref, ob_ref,
                wo_a_ref, wo_b_ref, gq_ref, wq_ref, k_ref, v_ref, wmo_ref,
                gf_ref, wr_hi_ref, wr_lo_ref,
                x2_ref, h2_ref, route_ref, count_ref, run_ref, obuf_ref, lbuf_ref):
    first = jnp.logical_and(pl.program_id(0) == 0, pl.program_id(1) == 0)

    @pl.when(first)
    def _():
        run_ref[...] = jnp.zeros_like(run_ref)

    n_cb = SWA_WIDTH // LANES
    for bi, (o_ref, l_ref) in enumerate(((o2_ref, l2_ref), (o3_ref, l3_ref))):
        dil = DILATIONS[bi + 1]
        for r in range(dil):
            rows = pl.ds(r, TRUNK_TM // dil, stride=dil)
            o_r = o_ref[r].astype(F32)
            l_r = l_ref[r]
            for c in range(n_cb):
                obuf_ref[bi * n_cb + c, rows, :] = o_r[:, c * LANES:(c + 1) * LANES]
                lbuf_ref[bi * n_cb + c, rows, :] = l_r[:, c * LANES:(c + 1) * LANES]

    slabs = [slice(j * TRUNK_SUB, (j + 1) * TRUNK_SUB) for j in range(TRUNK_TM // TRUNK_SUB)]

    def token_major(buf_ref, bi, sl):
        return jnp.concatenate([buf_ref[bi * n_cb + c, sl, :] for c in range(n_cb)], axis=-1)

    def merge(sl):
        l1, l2, l3 = l1_ref[sl, :], token_major(lbuf_ref, 0, sl), token_major(lbuf_ref, 1, sl)
        mx = jnp.maximum(jnp.maximum(l1, l2), l3)
        e1, e2, e3 = jnp.exp(l1 - mx), jnp.exp(l2 - mx), jnp.exp(l3 - mx)
        ya = (e1 * o1_ref[sl, :].astype(F32) + e2 * token_major(obuf_ref, 0, sl)
              + e3 * token_major(obuf_ref, 1, sl))
        return (ya / (e1 + e2 + e3)).astype(BF16)

    cols = [slice(h * MEM_HEAD_DIM, (h + 1) * MEM_HEAD_DIM) for h in range(MEM_HEADS)]
    lane = lax.broadcasted_iota(jnp.int32, (TRUNK_SUB, LANES), 1)
    big = jnp.int32(LANES)
    is_g = lane < N_GROUPS
    r_i = lax.broadcasted_iota(jnp.int32, (TRUNK_SUB, TRUNK_SUB), 0)
    c_i = lax.broadcasted_iota(jnp.int32, (TRUNK_SUB, TRUNK_SUB), 1)
    tri = jnp.where(c_i < r_i, 1.0, 0.0).astype(BF16)

    def top2(logits):
        lg = jnp.where(is_g, logits, NEG_BIG)
        mg = jnp.max(lg, axis=-1, keepdims=True)
        g_idx = jnp.min(jnp.where(jnp.logical_and(is_g, lg == mg), lane, big), axis=-1, keepdims=True)
        g_w = 1.0 / jnp.sum(jnp.exp(lg - mg), axis=-1, keepdims=True)
        lo_lane = N_GROUPS + g_idx * EXPERTS_PER_GROUP
        in_grp = jnp.logical_and(lane >= lo_lane, lane < lo_lane + EXPERTS_PER_GROUP)
        le = jnp.where(in_grp, logits, NEG_BIG)
        m1 = jnp.max(le, axis=-1, keepdims=True)
        i1 = jnp.min(jnp.where(jnp.logical_and(in_grp, le == m1), lane, big), axis=-1, keepdims=True)
        le2 = jnp.where(lane == i1, NEG_BIG, le)
        m2 = jnp.max(le2, axis=-1, keepdims=True)
        i2 = jnp.min(jnp.where(jnp.logical_and(in_grp, le2 == m2), lane, big), axis=-1, keepdims=True)
        r2 = jnp.exp(m2 - m1)
        return i1, i2, g_w / (1.0 + r2), g_w * r2 / (1.0 + r2)

    col_sums = []

    def slab(sl):
        ya = merge(sl)
        yield
        x1 = x_ref[sl, :] + _dot(ya, wo_a_ref[...]) + _dot(ob_ref[sl, :], wo_b_ref[...])
        yield
        q = _dot(_rms(x1, gq_ref[...]).astype(BF16), wq_ref[...]).astype(BF16)
        yield
        s = [_dot_nt(q[:, cs], k_ref[:, cs]) for cs in cols]
        yield
        p = [jnp.exp(sh - jnp.max(sh, axis=-1, keepdims=True)) for sh in s]
        den = [jnp.sum(ph, axis=-1, keepdims=True) for ph in p]
        yield
        pv = [_dot(ph.astype(BF16), v_ref[:, cs]) for ph, cs in zip(p, cols)]
        yield
        heads = [(a / d).astype(BF16) for a, d in zip(pv, den)]
        x2 = x1 + _dot(jnp.concatenate(heads, axis=-1), wmo_ref[...])
        x2_ref[sl, :] = x2
        yield
        h2 = _rms(x2, gf_ref[...])
        h2_ref[sl, :] = _pack_rows(h2)
        h_hi, h_lo = _split2(h2)
        logits = _dot(h_hi, wr_hi_ref[...]) + _dot(h_lo, wr_hi_ref[...]) + _dot(h_hi, wr_lo_ref[...])
        yield
        i1, i2, gate1, gate2 = top2(logits)
        yield
        oh = jnp.where(jnp.logical_or(lane == i1, lane == i2), 1.0, 0.0)
        prefix = _dot(tri, oh.astype(BF16))
        earlier = list(col_sums)
        col_sums.append(jnp.sum(oh, axis=0, keepdims=True))
        yield
        before = prefix + sum(earlier, run_ref[...])
        rank1 = jnp.sum(jnp.where(lane == i1, before, 0.0), axis=-1, keepdims=True)
        rank2 = jnp.sum(jnp.where(lane == i2, before, 0.0), axis=-1, keepdims=True)
        route = jnp.where(lane == 0, (i1 - N_GROUPS).astype(F32), 0.0)
        for j, val in enumerate(((i2 - N_GROUPS).astype(F32), gate1, gate2, rank1, rank2), start=1):
            route = jnp.where(lane == j, val, route)
        route_ref[sl, :] = route

    live = [slab(sl) for sl in slabs]
    while live:
        live = [h for h in live if next(h, StopIteration) is not StopIteration]
    total = sum(col_sums, run_ref[...])
    run_ref[...] = total
    count_ref[...] = total


def _trunk(x2d, branches, ob, mem_k, mem_v, w_out, norm_mem_q, w_mq, w_mo, norm_ffn,
           w_router_g, w_router_e, bsz, seq):
    tm = TRUNK_TM
    n = bsz * seq
    nblk = seq // tm
    (o1, l1), (o2, l2), (o3, l3) = branches
    wo = w_out.astype(BF16)
    wr = jnp.concatenate([w_router_g, jnp.moveaxis(w_router_e, 0, 1).reshape(D_MODEL, N_EXPERTS)], axis=1)
    wr = jnp.pad(wr, ((0, 0), (0, LANES - ROUTE_COLS)))
    wr_hi, wr_lo = _split2(wr)
    rows = lambda w: pl.BlockSpec((tm, w), lambda b, i: (b * nblk + i, 0))
    full = lambda shape: pl.BlockSpec(shape, lambda b, i: (0, 0))
    memspec = pl.BlockSpec((None, mem_k.shape[1], mem_k.shape[2]), lambda b, i: (b, 0, 0))
    d1, d4, d16 = DILATIONS
    natural = pl.BlockSpec((None, None, tm, SWA_WIDTH), lambda b, i: (b, 0, i, 0))
    strided = lambda d: pl.BlockSpec((None, d, tm // d, SWA_WIDTH), lambda b, i: (b, 0, i, 0))
    branch_specs = [natural, strided(d4), strided(d16)]
    return pl.pallas_call(
        _trunk_body,
        grid=(bsz, nblk),
        in_specs=[rows(D_MODEL)] + branch_specs + branch_specs + [rows(GDN_WIDTH)]
                 + [full((SWA_WIDTH, D_MODEL)), full((GDN_WIDTH, D_MODEL)), full((1, D_MODEL)),
                    full((D_MODEL, D_MODEL)), memspec, memspec, full((D_MODEL, D_MODEL)),
                    full((1, D_MODEL)), full((D_MODEL, LANES)), full((D_MODEL, LANES))],
        out_specs=[rows(D_MODEL), rows(PACKED), rows(LANES), full((1, LANES))],
        out_shape=[jax.ShapeDtypeStruct((n, D_MODEL), F32), jax.ShapeDtypeStruct((n, PACKED), jnp.uint32),
                   jax.ShapeDtypeStruct((n, LANES), F32), jax.ShapeDtypeStruct((1, LANES), F32)],
        scratch_shapes=[pltpu.VMEM((1, LANES), F32), pltpu.VMEM((2 * SWA_WIDTH // LANES, tm, LANES), F32),
                        pltpu.VMEM((2 * SWA_WIDTH // LANES, tm, LANES), F32)],
        compiler_params=_params("arbitrary", "arbitrary"),
        name="trunk",
    )(x2d, o1, o2, o3, l1, l2, l3, ob, wo[:SWA_WIDTH], wo[SWA_WIDTH:], norm_mem_q.reshape(1, D_MODEL),
      w_mq.astype(BF16), mem_k, mem_v, w_mo.astype(BF16), norm_ffn.reshape(1, D_MODEL), wr_hi, wr_lo)


DMA_UNROLL = 8
PACKED = D_MODEL // 2
HI_MASK = 0xFFFF0000


def _pack_rows(x):
    bits = lambda v: lax.bitcast_convert_type(v.astype(BF16).astype(F32), jnp.uint32)
    return (bits(x[:, :PACKED]) & jnp.uint32(HI_MASK)) | (bits(x[:, PACKED:]) >> 16)


def _unpack_rows(u):
    hi = lax.bitcast_convert_type(u & jnp.uint32(HI_MASK), F32)
    lo = lax.bitcast_convert_type(u << 16, F32)
    return hi, lo


def _dispatch_body(dest_ref, h_ref, xs_in_ref, xs_ref, sem, *, tm):
    del xs_in_ref

    def issue(t, c):
        for k in range(TOP_K):
            dst = xs_ref.at[pl.ds(dest_ref[0, 0, t * TOP_K + k], 1)]
            pltpu.make_async_copy(h_ref.at[pl.ds(t, 1)], dst, sem).start(priority=k % 2)
        return c

    lax.fori_loop(0, tm, issue, 0, unroll=DMA_UNROLL)
    for _ in range(TOP_K):
        pltpu.make_async_copy(h_ref, xs_ref.at[pl.ds(0, tm)], sem).wait()


def _dispatch(h2, dest, n_slots, tm=4096):
    n = h2.shape[0]
    dest3 = dest.reshape(n // tm, 1, tm * TOP_K)
    xs0 = jnp.zeros((n_slots, PACKED), jnp.uint32)
    return pl.pallas_call(
        functools.partial(_dispatch_body, tm=tm),
        grid=(n // tm,),
        in_specs=[pl.BlockSpec((1, 1, tm * TOP_K), lambda i: (i, 0, 0), memory_space=pltpu.SMEM),
                  pl.BlockSpec((tm, PACKED), lambda i: (i, 0)),
                  pl.BlockSpec(memory_space=pl.ANY)],
        out_specs=pl.BlockSpec(memory_space=pl.ANY),
        out_shape=jax.ShapeDtypeStruct((n_slots, PACKED), jnp.uint32),
        scratch_shapes=[pltpu.SemaphoreType.DMA(())],
        input_output_aliases={2: 0},
        compiler_params=_params("arbitrary"),
        name="moe_dispatch",
    )(dest3, h2, xs0)


def _expert_body(be_ref, nused_ref, x_ref, wg_ref, wu_ref, wd_ref, y_ref):
    i = pl.program_id(0)

    @pl.when(i < nused_ref[0])
    def _():
        x = jnp.concatenate([half.astype(BF16) for half in _unpack_rows(x_ref[...])], axis=-1)
        a = _dot(x, wg_ref[...])
        b = _dot(x, wu_ref[...])
        hid = (a * jax.nn.sigmoid(a) * b).astype(BF16)
        y_ref[...] = _pack_rows(_dot(hid, wd_ref[...]))

    @pl.when(i >= nused_ref[0])
    def _():
        y_ref[...] = jnp.zeros_like(y_ref)


def _experts(xs, block_expert, n_used, w_gate, w_up, w_down):
    n_slots = xs.shape[0]
    n_blocks = n_slots // ROUTE_BLOCK
    grid_spec = pltpu.PrefetchScalarGridSpec(
        num_scalar_prefetch=2,
        grid=(n_blocks,),
        in_specs=[pl.BlockSpec((ROUTE_BLOCK, PACKED), lambda i, be, nu: (i, 0)),
                  pl.BlockSpec((None, D_MODEL, D_EXPERT), lambda i, be, nu: (be[i], 0, 0)),
                  pl.BlockSpec((None, D_MODEL, D_EXPERT), lambda i, be, nu: (be[i], 0, 0)),
                  pl.BlockSpec((None, D_EXPERT, D_MODEL), lambda i, be, nu: (be[i], 0, 0))],
        out_specs=pl.BlockSpec((ROUTE_BLOCK, PACKED), lambda i, be, nu: (i, 0)),
    )
    return pl.pallas_call(
        _expert_body,
        grid_spec=grid_spec,
        out_shape=jax.ShapeDtypeStruct((n_slots, PACKED), jnp.uint32),
        compiler_params=_params("arbitrary"),
        name="moe_experts",
    )(block_expert, n_used, xs, w_gate, w_up, w_down)


def _combine_body(dest_ref, next_ref, x_ref, route_ref, g_ref, yb_ref, o_ref, buf_ref, sems, *, tm):
    i = pl.program_id(0)

    def gather(idx_ref, slot):
        def issue(t, c):
            for k in range(TOP_K):
                src = yb_ref.at[pl.ds(idx_ref[0, 0, t * TOP_K + k], 1)]
                pltpu.make_async_copy(src, buf_ref.at[slot, k, pl.ds(t, 1)], sems.at[slot]).start(priority=k % 2)
            return c

        lax.fori_loop(0, tm, issue, 0, unroll=DMA_UNROLL)

    @pl.when(i == 0)
    def _():
        gather(dest_ref, 0)

    @pl.when(i + 1 < pl.num_programs(0))
    def _():
        gather(next_ref, (i + 1) % 2)

    slot = i % 2
    for k in range(TOP_K):
        pltpu.make_async_copy(yb_ref.at[pl.ds(0, tm)], buf_ref.at[slot, k], sems.at[slot]).wait()
    route = route_ref[...]
    hi0, lo0 = _unpack_rows(buf_ref[slot, 0])
    hi1, lo1 = _unpack_rows(buf_ref[slot, 1])
    g0, g1 = route[:, 2:3], route[:, 3:4]
    y = jnp.concatenate([g0 * hi0 + g1 * hi1, g0 * lo0 + g1 * lo1], axis=-1)
    o_ref[...] = _rms(x_ref[...] + y, g_ref[...])


def _combine(x2, route, dest, yb, norm_final, tm=1024):
    n = x2.shape[0]
    dest3 = dest.reshape(n // tm, 1, tm * TOP_K)
    last = n // tm - 1
    return pl.pallas_call(
        functools.partial(_combine_body, tm=tm),
        grid=(n // tm,),
        in_specs=[pl.BlockSpec((1, 1, tm * TOP_K), lambda i: (i, 0, 0), memory_space=pltpu.SMEM),
                  pl.BlockSpec((1, 1, tm * TOP_K), lambda i: (jnp.minimum(i + 1, last), 0, 0),
                               memory_space=pltpu.SMEM),
                  pl.BlockSpec((tm, D_MODEL), lambda i: (i, 0)),
                  pl.BlockSpec((tm, LANES), lambda i: (i, 0)),
                  pl.BlockSpec((1, D_MODEL), lambda i: (0, 0)),
                  pl.BlockSpec(memory_space=pl.ANY)],
        out_specs=pl.BlockSpec((tm, D_MODEL), lambda i: (i, 0)),
        out_shape=jax.ShapeDtypeStruct((n, D_MODEL), F32),
        scratch_shapes=[pltpu.VMEM((2, TOP_K, tm, PACKED), jnp.uint32), pltpu.SemaphoreType.DMA((2,))],
        compiler_params=_params("arbitrary"),
        name="moe_combine",
    )(dest3, dest3, x2, route, norm_final.reshape(1, D_MODEL), yb)


def _moe(x2, h2, route, counts, w_gate, w_up, w_down, norm_final):
    n = x2.shape[0]
    m_slots = n * TOP_K
    n_blocks = -(-(m_slots + N_EXPERTS * (ROUTE_BLOCK - 1)) // ROUTE_BLOCK)
    cnt = counts[0, N_GROUPS:N_GROUPS + N_EXPERTS].astype(jnp.int32)
    padded = (cnt + ROUTE_BLOCK - 1) // ROUTE_BLOCK * ROUTE_BLOCK
    pad_end = jnp.cumsum(padded)
    seg_start = pad_end - padded
    eid = route[:, 0:TOP_K].astype(jnp.int32)
    rank = route[:, 4:4 + TOP_K].astype(jnp.int32)
    dest = (seg_start[eid] + rank).reshape(-1)
    block_start = jnp.arange(n_blocks, dtype=jnp.int32) * ROUTE_BLOCK
    block_expert = jnp.minimum(jnp.sum(pad_end[None, :] <= block_start[:, None], axis=1), N_EXPERTS - 1)
    block_expert = block_expert.astype(jnp.int32)
    n_used = (pad_end[-1:] // ROUTE_BLOCK).astype(jnp.int32)
    xs = _dispatch(h2, dest, n_blocks * ROUTE_BLOCK)
    yb = _experts(xs, block_expert, n_used, w_gate, w_up, w_down)
    return _combine(x2, route, dest, yb, norm_final)


GDN_TB = 1024
GDN_HALO = 16
GDN_GROUP = 4
GDN_HPG = 2
GDN_GW = GDN_HPG * HEAD_DIM


def _gdn_body(gd_ref, prev_ref, next_ref, abr_ref, cw_ref, alog_ref, dt_ref, *rest, rev, final):
    if final:
        z_ref, oprev_ref, gn_ref, o_ref, xpad_ref, state_ref = rest
    else:
        o_ref, xpad_ref, state_ref = rest
    i = pl.program_id(1)
    nblk = pl.num_programs(1)
    n_chunks = GDN_TB // CHUNK
    n_grp = GDN_HEADS // GDN_HPG
    width3 = 3 * GDN_WIDTH

    @pl.when(i == 0)
    def _():
        state_ref[...] = jnp.zeros_like(state_ref)

    blk = (nblk - 1 - i) if rev else i
    zero_halo = jnp.zeros((GDN_HALO, width3), BF16)
    xpad_ref[:GDN_HALO, :] = jnp.where(blk == 0, zero_halo, prev_ref[...])
    xpad_ref[GDN_HALO:GDN_HALO + GDN_TB, :] = gd_ref[...]
    xpad_ref[GDN_HALO + GDN_TB:, :] = jnp.where(blk == nblk - 1, zero_halo, next_ref[...])
    win = CHUNK + 2 * GDN_HALO
    side_taps = [j for j in range(CONV_K) if j != CONV_K // 2]
    sr = lax.broadcasted_iota(jnp.int32, (len(side_taps) * CHUNK, 1), 0)
    sc = lax.broadcasted_iota(jnp.int32, (1, win), 1)
    tap = sr // CHUNK
    tap = tap + jnp.where(tap >= CONV_K // 2, 1, 0)
    shift_sel = jnp.where(sc == sr % CHUNK + tap + (GDN_HALO - CONV_K // 2), 1.0, 0.0).astype(BF16)

    lane = lax.broadcasted_iota(jnp.int32, (1, GDN_GW), 1)
    tok = lane % CHUNK
    row = lax.broadcasted_iota(jnp.int32, (CHUNK, 1), 0)
    causal = (tok >= row) if rev else (tok <= row)
    strict = (tok > row) if rev else (tok < row)
    eye = jnp.where(tok == row, 1.0, 0.0)
    r2 = lax.broadcasted_iota(jnp.int32, (GDN_GW, 1), 0)
    same_head = (r2 // CHUNK) == (lane // CHUNK)
    bd_ones = jnp.where(same_head, 1.0, 0.0).astype(BF16)
    tri_in = (r2 % CHUNK >= tok) if rev else (r2 % CHUNK <= tok)
    tri_bd = jnp.where(jnp.logical_and(same_head, tri_in), 1.0, 0.0).astype(BF16)
    tri_and_ones = jnp.concatenate([tri_bd, bd_ones], axis=1)
    row8 = lax.broadcasted_iota(jnp.int32, (8, 1), 0)

    def blockdiag(x):
        xb = x.astype(BF16)
        zero = jnp.zeros_like(xb)
        return jnp.concatenate([jnp.where(lane // CHUNK == h, xb, zero) for h in range(GDN_HPG)], axis=0)

    def wide_mm(x, y):
        return _dot(x.astype(BF16), blockdiag(y))

    def head_sums(x):
        return _dot(x.astype(BF16), bd_ones)

    def local_phase(chunks, res):
        units = [(ci, p) for ci in range(len(chunks)) for p in range(n_grp)]
        r0 = [c * CHUNK for c in chunks]

        gates, gc_rows, gl_rows = [], [], []
        for c in chunks:
            graw = abr_ref[c]
            xg = graw + dt_ref[...]
            softplus = jnp.maximum(xg, 0.0) + jnp.log(1.0 + jnp.exp(-jnp.abs(xg)))
            g = jnp.where(row8 < n_grp, jax.nn.sigmoid(graw), -jnp.exp(alog_ref[...]) * softplus)
            g3 = _split3(g)
            gates.append(g)
            sums = sum(_dot(t, tri_and_ones) for t in g3)
            gc_rows.append(sums[:, :GDN_GW])
            gl_rows.append(sums[:, GDN_GW:])
        yield

        shifted = [_dot(shift_sel, xpad_ref[r0[ci]:r0[ci] + win, :]) for ci in range(len(chunks))]
        yield

        def conv(ci, p, col0):
            cols = slice(col0 + p * GDN_GW, col0 + (p + 1) * GDN_GW)
            centre = xpad_ref[r0[ci] + GDN_HALO:r0[ci] + GDN_HALO + CHUNK, cols].astype(F32)
            acc = centre * cw_ref[CONV_K // 2:CONV_K // 2 + 1, cols]
            for i, j in enumerate(side_taps):
                acc = acc + shifted[ci][i * CHUNK:(i + 1) * CHUNK, cols] * cw_ref[j:j + 1, cols]
            return acc * jax.nn.sigmoid(acc)

        qkv = {u: [conv(*u, col0) for col0 in (0, GDN_WIDTH, 2 * GDN_WIDTH)] for u in units}
        yield
        ss = {u: head_sums(jnp.concatenate([qkv[u][0] * qkv[u][0], qkv[u][1] * qkv[u][1]], axis=0))
              for u in units}
        yield
        prep = {}
        for (ci, p) in units:
            beta_r = gates[ci][p:p + 1]
            g_r = gates[ci][n_grp + p:n_grp + p + 1]
            lhs = jnp.concatenate([jnp.where(causal, g_r, 0.0), eye * beta_r], axis=0)
            prep[(ci, p)] = sum(_dot(t, bd_ones) for t in _split2(lhs))
        yield

        kq, kn_bd, e_gc, k_upd, decay_b = {}, {}, {}, {}, {}
        for u in units:
            ci, p = u
            qn = qkv[u][0] * lax.rsqrt(ss[u][:CHUNK] + EPS) * (HEAD_DIM ** -0.5)
            kn = qkv[u][1] * lax.rsqrt(ss[u][CHUNK:] + EPS)
            kq[u] = jnp.concatenate([kn, qn], axis=0).astype(BF16)
            kn_bd[u] = blockdiag(kn)
            beta_r = gates[ci][p:p + 1]
            gc_r = gc_rows[ci][n_grp + p:n_grp + p + 1]
            gl_r = gl_rows[ci][n_grp + p:n_grp + p + 1]
            gc_b, beta_b = prep[u][:CHUNK], prep[u][CHUNK:]
            decay_b[u] = jnp.exp(jnp.where(causal, gc_b - gc_r, NEG_BIG)) * beta_r
            e_gc[u] = jnp.exp(gc_b)
            k_upd[u] = (kn * (jnp.exp(gl_r - gc_b) * beta_b)).astype(BF16)
        yield

        kk_qk = {u: _dot_nt(kq[u], kn_bd[u]) for u in units}
        yield
        lpow = {u: jnp.where(strict, kk_qk[u][:CHUNK] * decay_b[u], 0.0) for u in units}
        amat = {u: kk_qk[u][CHUNK:] * decay_b[u] for u in units}
        n_sq = 5
        lpow = {u: -lpow[u] for u in units}
        tinv = {u: eye + lpow[u] for u in units}
        lpow = {u: wide_mm(lpow[u], lpow[u]) for u in units}
        yield
        for k in range(n_sq):
            if k < n_sq - 1:
                prod = {u: _dot(lpow[u].astype(BF16),
                                jnp.concatenate([blockdiag(tinv[u]), blockdiag(lpow[u])], axis=1)) for u in units}
                lpow = {u: prod[u][:, GDN_GW:] for u in units}
            else:
                prod = {u: wide_mm(lpow[u], tinv[u]) for u in units}
            tinv = {u: tinv[u] + prod[u][:, :GDN_GW] for u in units}
            yield
        res.update(r0=r0, kq=kq, tinv=tinv, amat=amat, e_gc=e_gc, k_upd=k_upd, gl_rows=gl_rows,
                   v={u: qkv[u][2] for u in units})

    def scan_phase(res):
        for ci in range(len(res["r0"])):
            us = [(ci, p) for p in range(n_grp)]
            state = {u: state_ref[u[1]] for u in us}
            pq = {u: _dot(res["kq"][u], state[u].astype(BF16)) for u in us}
            yield
            vhat = {u: wide_mm(res["tinv"][u], res["v"][u] - res["e_gc"][u] * pq[u][:CHUNK]) for u in us}
            yield
            upd = {u: _dot_tn(res["k_upd"][u], vhat[u].astype(BF16)) for u in us}
            for u in us:
                gl_r = res["gl_rows"][ci][n_grp + u[1]:n_grp + u[1] + 1]
                state_ref[u[1]] = state[u] * jnp.exp(gl_r) + jnp.where(same_head, upd[u], 0.0)
            yield
            o = {u: res["e_gc"][u] * pq[u][CHUNK:] + wide_mm(res["amat"][u], vhat[u]) for u in us}
            for u in us:
                cs = slice(u[1] * GDN_GW, (u[1] + 1) * GDN_GW)
                rows = slice(res["r0"][ci], res["r0"][ci] + CHUNK)
                if final:
                    ob = o[u] + oprev_ref[rows, cs]
                    ms = head_sums(ob * ob) * (1.0 / HEAD_DIM)
                    zz = z_ref[rows, cs].astype(F32)
                    out = ob * lax.rsqrt(ms + EPS) * gn_ref[:, cs] * (zz * jax.nn.sigmoid(zz))
                    o_ref[rows, cs] = out.astype(o_ref.dtype)
                else:
                    o_ref[rows, cs] = o[u]
            yield

    def emit_interleaved(*gens):
        live = list(gens)
        while live:
            for g in list(live):
                if next(g, StopIteration) is StopIteration:
                    live.remove(g)

    order = list(range(n_chunks))[::-1] if rev else list(range(n_chunks))
    groups = [order[j:j + GDN_GROUP] for j in range(0, n_chunks, GDN_GROUP)]
    results = [dict() for _ in groups]
    emit_interleaved(local_phase(groups[0], results[0]))
    for gi in range(len(groups)):
        nxt = [local_phase(groups[gi + 1], results[gi + 1])] if gi + 1 < len(groups) else []
        emit_interleaved(scan_phase(results[gi]), *nxt)


def _gdn_pass(gd3, abr, conv_w, alog_row, dt_row, rev, final_args=None):
    bsz, seq, width3 = gd3.shape
    nblk = seq // GDN_TB
    hpb = GDN_TB // GDN_HALO
    blk = (lambda i: nblk - 1 - i) if rev else (lambda i: i)
    main = lambda w: pl.BlockSpec((None, GDN_TB, w), lambda b, i: (b, blk(i), 0))
    full = lambda shape: pl.BlockSpec(shape, lambda b, i: (0,) * len(shape))
    in_specs = [main(width3),
                pl.BlockSpec((None, GDN_HALO, width3), lambda b, i: (b, jnp.maximum(blk(i) * hpb - 1, 0), 0)),
                pl.BlockSpec((None, GDN_HALO, width3),
                             lambda b, i: (b, jnp.minimum((blk(i) + 1) * hpb, seq // GDN_HALO - 1), 0)),
                pl.BlockSpec((None, GDN_TB // CHUNK, 8, GDN_GW), lambda b, i: (b, blk(i), 0, 0)),
                full((CONV_K, width3)), full((8, GDN_GW)), full((8, GDN_GW))]
    args = [gd3, gd3, gd3, abr, conv_w, alog_row, dt_row]
    final = final_args is not None
    if final:
        z3, o_prev, gn = final_args
        in_specs += [main(GDN_WIDTH), main(GDN_WIDTH), full((1, GDN_WIDTH))]
        args += [z3, o_prev, gn]
    return pl.pallas_call(
        functools.partial(_gdn_body, rev=rev, final=final),
        grid=(bsz, nblk),
        in_specs=in_specs,
        out_specs=main(GDN_WIDTH),
        out_shape=jax.ShapeDtypeStruct((bsz, seq, GDN_WIDTH), BF16 if final else F32),
        scratch_shapes=[pltpu.VMEM((GDN_TB + 2 * GDN_HALO, width3), BF16),
                        pltpu.VMEM((GDN_HEADS // GDN_HPG, GDN_GW, GDN_GW), F32)],
        compiler_params=_params("parallel", "arbitrary"),
        name="gdn_bwd" if rev else "gdn_fwd",
    )(*args)


def _group_rows(x, bsz, seq):
    x = x.reshape(bsz, seq // CHUNK, CHUNK, GDN_HEADS // GDN_HPG, GDN_HPG)
    return x.transpose(0, 1, 3, 4, 2).reshape(bsz, seq // CHUNK, GDN_HEADS // GDN_HPG, GDN_GW)


def _gdn(gd, z, ab, conv_w, a_log, dt_bias, gdn_norm, bsz, seq):
    gd3 = gd.reshape(bsz, seq, 3 * GDN_WIDTH)
    z3 = z.reshape(bsz, seq, GDN_WIDTH)
    gn = jnp.tile(gdn_norm, GDN_HEADS).reshape(1, GDN_WIDTH)
    out = None
    for d in range(2):
        n_grp = GDN_HEADS // GDN_HPG
        beta = _group_rows(ab[:, d * GDN_HEADS:(d + 1) * GDN_HEADS], bsz, seq)
        araw = _group_rows(ab[:, (2 + d) * GDN_HEADS:(3 + d) * GDN_HEADS], bsz, seq)
        abr = jnp.pad(jnp.concatenate([beta, araw], axis=2), ((0, 0), (0, 0), (0, 8 - 2 * n_grp), (0, 0)))
        per_head = lambda v: jnp.pad(jnp.repeat(v[d], CHUNK).reshape(n_grp, GDN_GW), ((n_grp, 8 - 2 * n_grp), (0, 0)))
        final_args = None if d == 0 else (z3, out, gn)
        out = _gdn_pass(gd3, abr, conv_w, per_head(a_log), per_head(dt_bias), rev=(d == 1), final_args=final_args)
    return out.reshape(bsz * seq, GDN_WIDTH)


def _encoder(x, mem, norm_mix, w_in, conv_w, a_log, dt_bias, gdn_norm, w_out, norm_mem_q, norm_mem_kv,
             w_mq, w_mkv, w_mo, norm_ffn, w_router_g, w_router_e, w_gate, w_up, w_down, norm_final):
    bsz, seq, _ = x.shape
    x2d = x.reshape(bsz * seq, D_MODEL)
    qkv1, qkv4, qkv16, gd, z, ab = _inproj(x2d, norm_mix, w_in, bsz, seq)
    slopes = jnp.exp2(-8.0 * jnp.arange(1, SWA_HEADS + 1, dtype=F32) / SWA_HEADS)
    branches = [_attn_branch(qkv, slopes, d) for qkv, d in zip((qkv1, qkv4, qkv16), DILATIONS)]
    ob = _gdn(gd, z, ab, conv_w, a_log, dt_bias, gdn_norm, bsz, seq)
    mem_k, mem_v = _memkv(mem.reshape(-1, D_MODEL), norm_mem_kv, w_mkv)
    mem_k = mem_k.reshape(bsz, -1, MEM_HEADS * MEM_HEAD_DIM)
    mem_v = mem_v.reshape(bsz, -1, MEM_HEADS * MEM_HEAD_DIM)
    x2, h2, route, counts = _trunk(x2d, branches, ob, mem_k, mem_v, w_out, norm_mem_q, w_mq, w_mo,
                                   norm_ffn, w_router_g, w_router_e, bsz, seq)
    y = _moe(x2, h2, route, counts, w_gate, w_up, w_down, norm_final)
    return y.reshape(bsz, seq, D_MODEL)


def kernel(x_prompt, x_sample, mem_prompt, mem_sample, norm_mix, w_in, conv_w, a_log, dt_bias, gdn_norm,
           w_out, norm_mem_q, norm_mem_kv, w_mq, w_mkv, w_mo, norm_ffn, w_router_g, w_router_e,
           w_gate, w_up, w_down, norm_final):
    p = dict(norm_mix=norm_mix[0], w_in=w_in[0], conv_w=conv_w[0], a_log=a_log[0], dt_bias=dt_bias[0],
             gdn_norm=gdn_norm[0], w_out=w_out[0], norm_mem_q=norm_mem_q[0], norm_mem_kv=norm_mem_kv[0],
             w_mq=w_mq[0], w_mkv=w_mkv[0], w_mo=w_mo[0], norm_ffn=norm_ffn[0], w_router_g=w_router_g[0],
             w_router_e=w_router_e[0], w_gate=w_gate[0].astype(BF16), w_up=w_up[0].astype(BF16),
             w_down=w_down[0].astype(BF16), norm_final=norm_final)
    return (_encoder(x_prompt, mem_prompt, **p), _encoder(x_sample, mem_sample, **p))
```
